```python
import numpy as np
import jax
import jax.numpy as jnp
from jax import lax

D_MODEL = 4096
BATCH = 4
SEQ = 2048
DEPTH = 2

N_EVEN = (DEPTH + 1) // 2
N_ODD = DEPTH // 2

ROPE_THETA = 10000.0
NORM_EPS = 1e-6
Q_BLOCK = 64

A_HEADS = 16
A_NOPE = 128
A_ROPE = 64
A_KV_RANK = 512
A_V_DIM = 128
IDX_HEADS = 32
IDX_DIM = 64
IDX_TOPK = 256

B_HEADS = 16
B_KV_HEADS = 4
B_GROUP = B_HEADS // B_KV_HEADS
B_HEAD_DIM = 128
CMP_LEN = 32
CMP_STRIDE = 16
CMP_HIDDEN = 256
SLC_LEN = 64
SLC_COUNT = 16
WIN_LEN = 512

AB_WIDTHS = (
    A_HEADS * A_NOPE,
    A_HEADS * A_ROPE,
    A_KV_RANK,
    A_ROPE,
    IDX_HEADS * IDX_DIM,
    IDX_DIM,
    IDX_HEADS,
    B_HEADS * B_HEAD_DIM,
    6 * B_KV_HEADS * B_HEAD_DIM,
    3 * B_HEADS,
)
AB_IN = sum(AB_WIDTHS)
AB_OUT = A_HEADS * A_V_DIM + B_HEADS * B_HEAD_DIM

C_HEAD_DIM = 128
C_HEADS = D_MODEL // C_HEAD_DIM
C_WIDTH = C_HEADS * C_HEAD_DIM
C_CHUNK = 64

N_GROUPS = 8
EXPERTS_PER_GROUP = 8
N_EXPERTS = N_GROUPS * EXPERTS_PER_GROUP
GROUP_TOPK = 2
EXPERT_FF = 3 * D_MODEL // 32
MOE_BLOCK = 128

kernel_name = "hybrid_dsa_nsa_hgrn2_hmoe"


def rms_norm(x, g):
    xf = x.astype(jnp.float32)
    y = xf * lax.rsqrt(jnp.mean(xf * xf, axis=-1, keepdims=True) + NORM_EPS)
    return (y * g.astype(jnp.float32)).astype(x.dtype)


def rope(x, pos):
    d = x.shape[-1]
    inv = ROPE_THETA ** (-jnp.arange(0, d, 2, dtype=jnp.float32) / d)
    ang = pos.astype(jnp.float32)[:, None] * inv[None, :]
    cos = jnp.cos(ang)[:, None, :]
    sin = jnp.sin(ang)[:, None, :]
    xf = x.astype(jnp.float32)
    x1, x2 = xf[..., : d // 2], xf[..., d // 2:]
    return jnp.concatenate([x1 * cos - x2 * sin, x2 * cos + x1 * sin], axis=-1).astype(x.dtype)


def masked_softmax(s, mask):
    s = jnp.where(mask, s.astype(jnp.float32), -jnp.inf)
    m = jnp.max(s, axis=-1, keepdims=True)
    m = jnp.where(jnp.isfinite(m), m, 0.0)
    e = jnp.where(mask, jnp.exp(s - m), 0.0)
    den = jnp.sum(e, axis=-1, keepdims=True)
    return e / jnp.where(den > 0, den, 1.0)


def split_cols(a, widths):
    return jnp.split(a, np.cumsum(widths)[:-1].tolist(), axis=-1)


def dsa_attention(q_cat, kv_lat, idx_q, idx_k, idx_w, w_uv):
    bsz, seq = kv_lat.shape[:2]
    topk = min(IDX_TOPK, seq // 4)
    key_pos = jnp.arange(seq)
    scale = (A_NOPE + A_ROPE) ** -0.5
    idx_scale = (IDX_DIM ** -0.5) * (IDX_HEADS ** -0.5)

    def block(i):
        q0 = i * Q_BLOCK
        qpos = q0 + jnp.arange(Q_BLOCK)
        sl = lambda a: lax.dynamic_slice_in_dim(a, q0, Q_BLOCK, axis=1)
        logits = jnp.einsum('bqhd,bsd->bqhs', sl(idx_q), idx_k)
        score = jnp.einsum('bqh,bqhs->bqs', sl(idx_w), jax.nn.relu(logits)) * idx_scale
        causal = key_pos[None, :] <= qpos[:, None]
        score = jnp.where(causal[None], score.astype(jnp.float32), -jnp.inf)
        _, sel = lax.top_k(score, topk)
        valid = sel <= qpos[None, :, None]
        kv_sel = jax.vmap(lambda a, ix: a[ix])(kv_lat, sel)
        s = jnp.einsum('bqhc,bqkc->bqhk', sl(q_cat), kv_sel) * scale
        p = masked_softmax(s, valid[:, :, None, :]).astype(kv_lat.dtype)
        o_lat = jnp.einsum('bqhk,bqkc->bqhc', p, kv_sel[..., :A_KV_RANK])
        return jnp.einsum('bqhc,hcd->bqhd', o_lat, w_uv)

    out = lax.map(block, jnp.arange(seq // Q_BLOCK))
    return out.transpose(1, 0, 2, 3, 4).reshape(bsz, seq, A_HEADS * A_V_DIM)


def nsa_attention(q, k_cmp, v_cmp, k_slc, v_slc, k_win, v_win, gates, cmp_pos, cmp_w1, cmp_w2):
    bsz, seq = q.shape[:2]
    kvh, grp, hd = B_KV_HEADS, B_GROUP, B_HEAD_DIM
    scale = hd ** -0.5
    pos = jnp.arange(seq)
    qg = q.reshape(bsz, seq, kvh, grp, hd).transpose(0, 2, 3, 1, 4)

    n_cmp = (seq - CMP_LEN) // CMP_STRIDE + 1
    tok = np.arange(n_cmp)[:, None] * CMP_STRIDE + np.arange(CMP_LEN)[None, :]

    def compress(a, j):
        blk = a[:, tok] + cmp_pos[j][:, None, :]
        blk = blk.transpose(0, 1, 3, 2, 4).reshape(bsz, n_cmp, kvh, CMP_LEN * hd)
        return jax.nn.gelu(blk @ cmp_w1[j]) @ cmp_w2[j]

    kc, vc = compress(k_cmp, 0), compress(v_cmp, 1)
    cmp_start = jnp.arange(n_cmp) * CMP_STRIDE
    cmp_end = cmp_start + CMP_LEN
    cmp_mask = (cmp_end - 1)[None, :] <= pos[:, None]
    s_c = jnp.einsum('bhgtd,bchd->bhgtc', qg, kc) * scale
    p_c = masked_softmax(s_c, cmp_mask)
    o_cmp = jnp.einsum('bhgtc,bchd->bhgtd', p_c.astype(q.dtype), vc)

    n_slc = seq // SLC_LEN
    slc_start = jnp.arange(n_slc) * SLC_LEN
    overlap = ((cmp_start[:, None] < slc_start[None, :] + SLC_LEN)
               & (cmp_end[:, None] > slc_start[None, :])).astype(jnp.float32)
    imp = jnp.einsum('bhgtc,cj->bhtj', p_c, overlap)
    blk = jnp.arange(n_slc)[None, :]
    t_blk = (pos // SLC_LEN)[:, None]
    forced = (blk == 0) | (blk == t_blk) | (blk == t_blk - 1)
    imp = jnp.where(forced, jnp.inf, imp)
    imp = jnp.where(blk * SLC_LEN <= pos[:, None], imp, -jnp.inf)
    n_sel = min(SLC_COUNT, n_slc)
    _, slc_idx = lax.top_k(imp, n_sel)

    ks_b = k_slc.reshape(bsz, n_slc, SLC_LEN, kvh, hd).transpose(0, 3, 1, 2, 4)
    vs_b = v_slc.reshape(bsz, n_slc, SLC_LEN, kvh, hd).transpose(0, 3, 1, 2, 4)
    pad = ((0, 0), (WIN_LEN, 0), (0, 0), (0, 0))
    kw_pad = jnp.pad(k_win, pad).transpose(0, 2, 1, 3)
    vw_pad = jnp.pad(v_win, pad).transpose(0, 2, 1, 3)
    gather_blocks = jax.vmap(jax.vmap(lambda blocks, ix: blocks[ix]))

    def block(i):
        q0 = i * Q_BLOCK
        qpos = q0 + jnp.arange(Q_BLOCK)
        qb = lax.dynamic_slice_in_dim(qg, q0, Q_BLOCK, axis=3)
        sel = lax.dynamic_slice_in_dim(slc_idx, q0, Q_BLOCK, axis=2)
        sel_flat = sel.reshape(bsz, kvh, Q_BLOCK * n_sel)
        kg = gather_blocks(ks_b, sel_flat).reshape(bsz, kvh, Q_BLOCK, n_sel * SLC_LEN, hd)
        vg = gather_blocks(vs_b, sel_flat).reshape(bsz, kvh, Q_BLOCK, n_sel * SLC_LEN, hd)
        tpos = (sel[..., None] * SLC_LEN + jnp.arange(SLC_LEN)).reshape(bsz, kvh, Q_BLOCK, n_sel * SLC_LEN)
        s = jnp.einsum('bhgqd,bhqkd->bhgqk', qb, kg) * scale
        p = masked_softmax(s, (tpos <= qpos[:, None])[:, :, None]).astype(q.dtype)
        o_s = jnp.einsum('bhgqk,bhqkd->bhgqd', p, vg)
        kw = lax.dynamic_slice_in_dim(kw_pad, q0, WIN_LEN + Q_BLOCK, axis=2)
        vw = lax.dynamic_slice_in_dim(vw_pad, q0, WIN_LEN + Q_BLOCK, axis=2)
        wpos = q0 - WIN_LEN + jnp.arange(WIN_LEN + Q_BLOCK)
        wmask = ((wpos[None, :] <= qpos[:, None]) & (wpos[None, :] > qpos[:, None] - WIN_LEN)
                 & (wpos[None, :] >= 0))
        s = jnp.einsum('bhgqd,bhkd->bhgqk', qb, kw) * scale
        p = masked_softmax(s, wmask).astype(q.dtype)
        o_w = jnp.einsum('bhgqk,bhkd->bhgqd', p, vw)
        return o_s, o_w

    o_s, o_w = lax.map(block, jnp.arange(seq // Q_BLOCK))
    unblk = lambda o: o.transpose(1, 2, 3, 0, 4, 5).reshape(bsz, kvh, grp, seq, hd)
    o_s, o_w = unblk(o_s), unblk(o_w)
    g = jax.nn.sigmoid(gates.astype(jnp.float32)).astype(q.dtype)
    g = g.reshape(bsz, seq, kvh, grp, 3).transpose(0, 2, 3, 1, 4)
    o = g[..., 0:1] * o_cmp + g[..., 1:2] * o_s + g[..., 2:3] * o_w
    return o.transpose(0, 3, 1, 2, 4).reshape(bsz, seq, B_HEADS * hd)


def dsa_nsa_mixer(u, w_in, kv_norm, w_uk, w_uv, cmp_pos, cmp_w1, cmp_w2, w_out):
    bsz, seq, _ = u.shape
    pos = jnp.arange(seq)
    (a_q_nope, a_q_rope, a_ckv, a_k_rope, i_q, i_k, i_w, b_q, b_kv, b_gate) = split_cols(u @ w_in, AB_WIDTHS)
    q_nope = a_q_nope.reshape(bsz, seq, A_HEADS, A_NOPE)
    q_rope = rope(a_q_rope.reshape(bsz, seq, A_HEADS, A_ROPE), pos)
    q_abs = jnp.einsum('blhd,hdc->blhc', q_nope, w_uk)
    q_cat = jnp.concatenate([q_abs, q_rope], axis=-1)
    c_kv = rms_norm(a_ckv, kv_norm)
    k_rope = rope(a_k_rope[:, :, None, :], pos)[:, :, 0]
    kv_lat = jnp.concatenate([c_kv, k_rope], axis=-1)
    idx_q = rope(i_q.reshape(bsz, seq, IDX_HEADS, IDX_DIM), pos)
    idx_k = rope(i_k[:, :, None, :], pos)[:, :, 0]
    o_a = dsa_attention(q_cat, kv_lat, idx_q, idx_k, i_w, w_uv)
    bq = rope(b_q.reshape(bsz, seq, B_HEADS, B_HEAD_DIM), pos)
    kv = b_kv.reshape(bsz, seq, 6, B_KV_HEADS, B_HEAD_DIM)
    o_b = nsa_attention(bq, rope(kv[:, :, 0], pos), kv[:, :, 1],
                        rope(kv[:, :, 2], pos), kv[:, :, 3],
                        rope(kv[:, :, 4], pos), kv[:, :, 5],
                        b_gate.reshape(bsz, seq, B_HEADS, 3), cmp_pos, cmp_w1, cmp_w2)
    return jnp.concatenate([o_a, o_b], axis=-1) @ w_out


def chunk_gated_recurrence(q, k, v, log_f):
    _, bsz, nh, c, dk = q.shape
    dv = v.shape[-1]
    causal = jnp.tril(jnp.ones((c, c), dtype=bool))[None, None, :, :, None]

    def step(state, inp):
        qc, kc, vc, lf = inp
        qf, kf, vf = qc.astype(jnp.float32), kc.astype(jnp.float32), vc.astype(jnp.float32)
        b = jnp.cumsum(lf, axis=2)
        diff = jnp.where(causal, b[:, :, :, None, :] - b[:, :, None, :, :], -jnp.inf)
        att = jnp.einsum('bhtd,bhtsd,bhsd->bhts', qf, jnp.exp(diff), kf)
        o = att @ vf + jnp.einsum('bhtd,bhde->bhte', qf * jnp.exp(b), state)
        b_last = b[:, :, -1, :]
        state = (jnp.exp(b_last)[..., None] * state
                 + jnp.einsum('bhsd,bhse->bhde', kf * jnp.exp(b_last[:, :, None, :] - b), vf))
        return state, o.astype(v.dtype)

    s0 = jnp.zeros((bsz, nh, dk, dv), jnp.float32)
    _, o = lax.scan(step, s0, (q, k, v, log_f))
    return o


def hgrn2_mixer(u, w_in, lower_bound, out_norm, w_out):
    bsz, seq, _ = u.shape
    q, f, i, g = jnp.split(u @ w_in, 4, axis=-1)
    q = jax.nn.silu(q)
    fg = lower_bound + (1.0 - lower_bound) * jax.nn.sigmoid(f.astype(jnp.float32))
    k = (1.0 - fg).astype(u.dtype)
    log_f = jnp.log(fg)
    n_chunk = seq // C_CHUNK
    heads = lambda a: a.reshape(bsz, n_chunk, C_CHUNK, C_HEADS, C_HEAD_DIM).transpose(1, 0, 3, 2, 4)
    o = chunk_gated_recurrence(heads(q), heads(k), heads(i), heads(log_f))
    o = o.transpose(1, 0, 3, 2, 4).reshape(bsz, seq, C_HEADS, C_HEAD_DIM)
    o = rms_norm(o, out_norm.reshape(C_HEADS, C_HEAD_DIM)).reshape(bsz, seq, C_WIDTH)
    return (o * jax.nn.silu(g)) @ w_out


def routed_experts(xf, e_id, e_w, w_gate_up, w_down):
    n_tok = xf.shape[0]
    n_as = e_id.shape[0]
    tok = jnp.arange(n_as) // GROUP_TOPK
    counts = jnp.zeros((N_EXPERTS,), jnp.int32).at[e_id].add(1)
    padded = (counts + MOE_BLOCK - 1) // MOE_BLOCK * MOE_BLOCK
    pad_end = jnp.cumsum(padded)
    pad_start = pad_end - padded
    start = jnp.cumsum(counts) - counts
    order = jnp.argsort(e_id)
    e_sorted = e_id[order]
    dest = pad_start[e_sorted] + jnp.arange(n_as) - start[e_sorted]
    n_blk = -(-(n_as + N_EXPERTS * (MOE_BLOCK - 1)) // MOE_BLOCK)
    rows = n_blk * MOE_BLOCK
    row_tok = jnp.zeros((rows,), jnp.int32).at[dest].set(tok[order])
    row_w = jnp.zeros((rows,), xf.dtype).at[dest].set(e_w[order])
    blk_e = jnp.minimum(jnp.searchsorted(pad_end, jnp.arange(n_blk) * MOE_BLOCK, side='right'), N_EXPERTS - 1)
    xb = xf[row_tok].reshape(n_blk, MOE_BLOCK, xf.shape[-1])

    def expert_block(args):
        xe, e = args
        gate, up = jnp.split(xe @ w_gate_up[e], 2, axis=-1)
        return (jax.nn.silu(gate) * up) @ w_down[e]

    yb = lax.map(expert_block, (xb, blk_e)).reshape(rows, xf.shape[-1])
    return jax.ops.segment_sum(yb * row_w[:, None], row_tok, num_segments=n_tok)


def hier_moe(u, w_group, b_group, w_expert, b_expert, w_gate_up, w_down):
    bsz, seq, d = u.shape
    n_tok = bsz * seq
    xf = u.reshape(n_tok, d)
    g_logits = (xf @ w_group).astype(jnp.float32) + b_group.astype(jnp.float32)
    g_prob = jax.nn.softmax(g_logits, axis=-1)
    grp = jnp.argmax(g_logits, axis=-1)
    g_w = jnp.take_along_axis(g_prob, grp[:, None], axis=-1)
    e_logits = ((xf @ w_expert).astype(jnp.float32) + b_expert.astype(jnp.float32))
    e_logits = e_logits.reshape(n_tok, N_GROUPS, EXPERTS_PER_GROUP)
    e_logits = jnp.take_along_axis(e_logits, grp[:, None, None], axis=1)[:, 0]
    top_v, top_i = lax.top_k(e_logits, GROUP_TOPK)
    e_w = jax.nn.softmax(top_v, axis=-1) * g_w
    e_id = grp[:, None] * EXPERTS_PER_GROUP + top_i
    y = routed_experts(xf, e_id.reshape(-1), e_w.reshape(-1).astype(u.dtype), w_gate_up, w_down)
    return y.reshape(bsz, seq, d)


def setup_inputs(seed: int = 0) -> dict:
    key = jax.random.key(seed)
    ks = iter(jax.random.split(key, 32))

    def nrm(shape, scale):
        return jax.random.normal(next(ks), shape, jnp.float32) * scale

    def gain(shape):
        return 1.0 + nrm(shape, 0.02)

    return {
        "x": nrm((BATCH, SEQ, D_MODEL), 1.0),
        "mix_norm": gain((DEPTH, D_MODEL)),
        "ffn_norm": gain((DEPTH, D_MODEL)),
        "final_norm": gain((D_MODEL,)),
        "ab_w_in": nrm((N_EVEN, D_MODEL, AB_IN), D_MODEL ** -0.5),
        "dsa_kv_norm": gain((N_EVEN, A_KV_RANK)),
        "dsa_w_uk": nrm((N_EVEN, A_HEADS, A_NOPE, A_KV_RANK), A_KV_RANK ** -0.5),
        "dsa_w_uv": nrm((N_EVEN, A_HEADS, A_KV_RANK, A_V_DIM), A_KV_RANK ** -0.5),
        "nsa_cmp_pos": nrm((N_EVEN, 2, CMP_LEN, B_HEAD_DIM), 0.1),
        "nsa_cmp_w1": nrm((N_EVEN, 2, CMP_LEN * B_HEAD_DIM, CMP_HIDDEN), (CMP_LEN * B_HEAD_DIM) ** -0.5),
        "nsa_cmp_w2": nrm((N_EVEN, 2, CMP_HIDDEN, B_HEAD_DIM), 1.5 * CMP_HIDDEN ** -0.5),
        "ab_w_out": nrm((N_EVEN, AB_OUT, D_MODEL), AB_OUT ** -0.5),
        "hgrn_w_in": nrm((N_ODD, D_MODEL, 4 * C_WIDTH), D_MODEL ** -0.5),
        "hgrn_lb_logits": nrm((DEPTH, C_WIDTH), 0.5),
        "hgrn_out_norm": gain((N_ODD, C_WIDTH)),
        "hgrn_w_out": nrm((N_ODD, C_WIDTH, D_MODEL), C_WIDTH ** -0.5),
        "moe_w_group": nrm((DEPTH, D_MODEL, N_GROUPS), D_MODEL ** -0.5),
        "moe_b_group": nrm((DEPTH, N_GROUPS), 0.01),
        "moe_w_expert": nrm((DEPTH, D_MODEL, N_EXPERTS), D_MODEL ** -0.5),
        "moe_b_expert": nrm((DEPTH, N_EXPERTS), 0.01),
        "moe_w_gate_up": nrm((DEPTH, N_EXPERTS, D_MODEL, 2 * EXPERT_FF), D_MODEL ** -0.5),
        "moe_w_down": nrm((DEPTH, N_EXPERTS, EXPERT_FF, D_MODEL), EXPERT_FF ** -0.5),
    }


def reference(x, mix_norm, ffn_norm, final_norm, ab_w_in, dsa_kv_norm, dsa_w_uk, dsa_w_uv,
              nsa_cmp_pos, nsa_cmp_w1, nsa_cmp_w2, ab_w_out, hgrn_w_in, hgrn_lb_logits,
              hgrn_out_norm, hgrn_w_out, moe_w_group, moe_b_group, moe_w_expert, moe_b_expert,
              moe_w_gate_up, moe_w_down):
    lb_p = jax.nn.softmax(hgrn_lb_logits.astype(jnp.float32), axis=0)
    lower_bounds = jnp.cumsum(lb_p, axis=0) - lb_p[0]
    h = x
    for layer in range(DEPTH):
        j = layer // 2
        u = rms_norm(h, mix_norm[layer])
        if layer % 2 == 0:
            h = h + dsa_nsa_mixer(u, ab_w_in[j], dsa_kv_norm[j], dsa_w_uk[j], dsa_w_uv[j],
                                  nsa_cmp_pos[j], nsa_cmp_w1[j], nsa_cmp_w2[j], ab_w_out[j])
        else:
            h = h + hgrn2_mixer(u, hgrn_w_in[j], lower_bounds[layer], hgrn_out_norm[j], hgrn_w_out[j])
        u = rms_norm(h, ffn_norm[layer])
        h = h + hier_moe(u, moe_w_group[layer], moe_b_group[layer], moe_w_expert[layer],
                         moe_b_expert[layer], moe_w_gate_up[layer], moe_w_down[layer])
    return rms_norm(h, final_norm)
```

```python
import functools

import numpy as np
import jax
import jax.numpy as jnp
from jax import lax
from jax.experimental import pallas as pl
from jax.experimental.pallas import tpu as pltpu

D_MODEL = 4096
BATCH = 4
SEQ = 2048
DEPTH = 2

ROPE_THETA = 10000.0
NORM_EPS = 1e-6
Q_BLOCK = 64

A_HEADS = 16
A_NOPE = 128
A_ROPE = 64
A_KV_RANK = 512
A_V_DIM = 128
IDX_HEADS = 32
IDX_DIM = 64
IDX_TOPK = 256

B_HEADS = 16
B_KV_HEADS = 4
B_GROUP = B_HEADS // B_KV_HEADS
B_HEAD_DIM = 128
CMP_LEN = 32
CMP_STRIDE = 16
CMP_HIDDEN = 256
SLC_LEN = 64
SLC_COUNT = 16
WIN_LEN = 512

AB_WIDTHS = (
    A_HEADS * A_NOPE,
    A_HEADS * A_ROPE,
    A_KV_RANK,
    A_ROPE,
    IDX_HEADS * IDX_DIM,
    IDX_DIM,
    IDX_HEADS,
    B_HEADS * B_HEAD_DIM,
    6 * B_KV_HEADS * B_HEAD_DIM,
    3 * B_HEADS,
)
AB_IN = sum(AB_WIDTHS)
AB_OUT = A_HEADS * A_V_DIM + B_HEADS * B_HEAD_DIM

C_HEAD_DIM = 128
C_HEADS = D_MODEL // C_HEAD_DIM
C_WIDTH = C_HEADS * C_HEAD_DIM
C_CHUNK = 64

N_GROUPS = 8
EXPERTS_PER_GROUP = 8
N_EXPERTS = N_GROUPS * EXPERTS_PER_GROUP
GROUP_TOPK = 2
EXPERT_FF = 3 * D_MODEL // 32
MOE_BLOCK = 128

VMEM_LIMIT_BYTES = 48 * 1024 * 1024


def _mm_kernel(x_ref, w_ref, o_ref, acc_ref):
    k = pl.program_id(2)

    @pl.when(k == 0)
    def _():
        acc_ref[...] = jnp.zeros_like(acc_ref)

    acc_ref[...] += jnp.dot(x_ref[...].astype(jnp.bfloat16), w_ref[...].astype(jnp.bfloat16),
                            preferred_element_type=jnp.float32)

    @pl.when(k == pl.num_programs(2) - 1)
    def _():
        o_ref[...] = acc_ref[...]


def _matmul(x, w, tm=1024, tn=512, tk=1024):
    m, kd = x.shape
    n = w.shape[1]
    n_pad = -(-n // tn) * tn
    if n_pad != n:
        w = jnp.pad(w, ((0, 0), (0, n_pad - n)))
    tm = min(tm, m)
    tk = min(tk, kd)
    assert m % tm == 0 and kd % tk == 0
    out = pl.pallas_call(
        _mm_kernel,
        grid=(m // tm, n_pad // tn, kd // tk),
        in_specs=[pl.BlockSpec((tm, tk), lambda i, j, k: (i, k)),
                  pl.BlockSpec((tk, tn), lambda i, j, k: (k, j))],
        out_specs=pl.BlockSpec((tm, tn), lambda i, j, k: (i, j)),
        out_shape=jax.ShapeDtypeStruct((m, n_pad), jnp.float32),
        scratch_shapes=[pltpu.VMEM((tm, tn), jnp.float32)],
        compiler_params=pltpu.CompilerParams(
            dimension_semantics=("parallel", "parallel", "arbitrary"),
            vmem_limit_bytes=VMEM_LIMIT_BYTES),
        name="matmul",
    )(x, w)
    return out[:, :n] if n_pad != n else out


def _proj(u, w):
    lead = u.shape[:-1]
    return _matmul(u.reshape(-1, u.shape[-1]), w).reshape(*lead, w.shape[1])


def rms_norm(x, g):
    xf = x.astype(jnp.float32)
    y = xf * lax.rsqrt(jnp.mean(xf * xf, axis=-1, keepdims=True) + NORM_EPS)
    return (y * g.astype(jnp.float32)).astype(x.dtype)


def rope(x, pos):
    d = x.shape[-1]
    inv = ROPE_THETA ** (-jnp.arange(0, d, 2, dtype=jnp.float32) / d)
    ang = pos.astype(jnp.float32)[:, None] * inv[None, :]
    cos = jnp.cos(ang)[:, None, :]
    sin = jnp.sin(ang)[:, None, :]
    xf = x.astype(jnp.float32)
    x1, x2 = xf[..., : d // 2], xf[..., d // 2:]
    return jnp.concatenate([x1 * cos - x2 * sin, x2 * cos + x1 * sin], axis=-1).astype(x.dtype)


def masked_softmax(s, mask):
    s = jnp.where(mask, s.astype(jnp.float32), -jnp.inf)
    m = jnp.max(s, axis=-1, keepdims=True)
    m = jnp.where(jnp.isfinite(m), m, 0.0)
    e = jnp.where(mask, jnp.exp(s - m), 0.0)
    den = jnp.sum(e, axis=-1, keepdims=True)
    return e / jnp.where(den > 0, den, 1.0)


def split_cols(a, widths):
    return jnp.split(a, np.cumsum(widths)[:-1].tolist(), axis=-1)


def dsa_attention(q_cat, kv_lat, idx_q, idx_k, idx_w, w_uv):
    bsz, seq = kv_lat.shape[:2]
    topk = min(IDX_TOPK, seq // 4)
    key_pos = jnp.arange(seq)
    scale = (A_NOPE + A_ROPE) ** -0.5
    idx_scale = (IDX_DIM ** -0.5) * (IDX_HEADS ** -0.5)

    def block(i):
        q0 = i * Q_BLOCK
        qpos = q0 + jnp.arange(Q_BLOCK)
        sl = lambda a: lax.dynamic_slice_in_dim(a, q0, Q_BLOCK, axis=1)
        logits = jnp.einsum('bqhd,bsd->bqhs', sl(idx_q), idx_k)
        score = jnp.einsum('bqh,bqhs->bqs', sl(idx_w), jax.nn.relu(logits)) * idx_scale
        causal = key_pos[None, :] <= qpos[:, None]
        score = jnp.where(causal[None], score.astype(jnp.float32), -jnp.inf)
        _, sel = lax.top_k(score, topk)
        valid = sel <= qpos[None, :, None]
        kv_sel = jax.vmap(lambda a, ix: a[ix])(kv_lat, sel)
        s = jnp.einsum('bqhc,bqkc->bqhk', sl(q_cat), kv_sel) * scale
        p = masked_softmax(s, valid[:, :, None, :]).astype(kv_lat.dtype)
        o_lat = jnp.einsum('bqhk,bqkc->bqhc', p, kv_sel[..., :A_KV_RANK])
        return jnp.einsum('bqhc,hcd->bqhd', o_lat, w_uv)

    out = lax.map(block, jnp.arange(seq // Q_BLOCK))
    return out.transpose(1, 0, 2, 3, 4).reshape(bsz, seq, A_HEADS * A_V_DIM)


def nsa_attention(q, k_cmp, v_cmp, k_slc, v_slc, k_win, v_win, gates, cmp_pos, cmp_w1, cmp_w2):
    bsz, seq = q.shape[:2]
    kvh, grp, hd = B_KV_HEADS, B_GROUP, B_HEAD_DIM
    scale = hd ** -0.5
    pos = jnp.arange(seq)
    qg = q.reshape(bsz, seq, kvh, grp, hd).transpose(0, 2, 3, 1, 4)

    n_cmp = (seq - CMP_LEN) // CMP_STRIDE + 1
    tok = np.arange(n_cmp)[:, None] * CMP_STRIDE + np.arange(CMP_LEN)[None, :]

    def compress(a, j):
        blk = a[:, tok] + cmp_pos[j][:, None, :]
        blk = blk.transpose(0, 1, 3, 2, 4).reshape(bsz, n_cmp, kvh, CMP_LEN * hd)
        return jax.nn.gelu(blk @ cmp_w1[j]) @ cmp_w2[j]

    kc, vc = compress(k_cmp, 0), compress(v_cmp, 1)
    cmp_start = jnp.arange(n_cmp) * CMP_STRIDE
    cmp_end = cmp_start + CMP_LEN
    cmp_mask = (cmp_end - 1)[None, :] <= pos[:, None]
    s_c = jnp.einsum('bhgtd,bchd->bhgtc', qg, kc) * scale
    p_c = masked_softmax(s_c, cmp_mask)
    o_cmp = jnp.einsum('bhgtc,bchd->bhgtd', p_c.astype(q.dtype), vc)

    n_slc = seq // SLC_LEN
    slc_start = jnp.arange(n_slc) * SLC_LEN
    overlap = ((cmp_start[:, None] < slc_start[None, :] + SLC_LEN)
               & (cmp_end[:, None] > slc_start[None, :])).astype(jnp.float32)
    imp = jnp.einsum('bhgtc,cj->bhtj', p_c, overlap)
    blk = jnp.arange(n_slc)[None, :]
    t_blk = (pos // SLC_LEN)[:, None]
    forced = (blk == 0) | (blk == t_blk) | (blk == t_blk - 1)
    imp = jnp.where(forced, jnp.inf, imp)
    imp = jnp.where(blk * SLC_LEN <= pos[:, None], imp, -jnp.inf)
    n_sel = min(SLC_COUNT, n_slc)
    _, slc_idx = lax.top_k(imp, n_sel)

    ks_b = k_slc.reshape(bsz, n_slc, SLC_LEN, kvh, hd).transpose(0, 3, 1, 2, 4)
    vs_b = v_slc.reshape(bsz, n_slc, SLC_LEN, kvh, hd).transpose(0, 3, 1, 2, 4)
    pad = ((0, 0), (WIN_LEN, 0), (0, 0), (0, 0))
    kw_pad = jnp.pad(k_win, pad).transpose(0, 2, 1, 3)
    vw_pad = jnp.pad(v_win, pad).transpose(0, 2, 1, 3)
    gather_blocks = jax.vmap(jax.vmap(lambda blocks, ix: blocks[ix]))

    def block(i):
        q0 = i * Q_BLOCK
        qpos = q0 + jnp.arange(Q_BLOCK)
        qb = lax.dynamic_slice_in_dim(qg, q0, Q_BLOCK, axis=3)
        sel = lax.dynamic_slice_in_dim(slc_idx, q0, Q_BLOCK, axis=2)
        sel_flat = sel.reshape(bsz, kvh, Q_BLOCK * n_sel)
        kg = gather_blocks(ks_b, sel_flat).reshape(bsz, kvh, Q_BLOCK, n_sel * SLC_LEN, hd)
        vg = gather_blocks(vs_b, sel_flat).reshape(bsz, kvh, Q_BLOCK, n_sel * SLC_LEN, hd)
        tpos = (sel[..., None] * SLC_LEN + jnp.arange(SLC_LEN)).reshape(bsz, kvh, Q_BLOCK, n_sel * SLC_LEN)
        s = jnp.einsum('bhgqd,bhqkd->bhgqk', qb, kg) * scale
        p = masked_softmax(s, (tpos <= qpos[:, None])[:, :, None]).astype(q.dtype)
        o_s = jnp.einsum('bhgqk,bhqkd->bhgqd', p, vg)
        kw = lax.dynamic_slice_in_dim(kw_pad, q0, WIN_LEN + Q_BLOCK, axis=2)
        vw = lax.dynamic_slice_in_dim(vw_pad, q0, WIN_LEN + Q_BLOCK, axis=2)
        wpos = q0 - WIN_LEN + jnp.arange(WIN_LEN + Q_BLOCK)
        wmask = ((wpos[None, :] <= qpos[:, None]) & (wpos[None, :] > qpos[:, None] - WIN_LEN)
                 & (wpos[None, :] >= 0))
        s = jnp.einsum('bhgqd,bhkd->bhgqk', qb, kw) * scale
        p = masked_softmax(s, wmask).astype(q.dtype)
        o_w = jnp.einsum('bhgqk,bhkd->bhgqd', p, vw)
        return o_s, o_w

    o_s, o_w = lax.map(block, jnp.arange(seq // Q_BLOCK))
    unblk = lambda o: o.transpose(1, 2, 3, 0, 4, 5).reshape(bsz, kvh, grp, seq, hd)
    o_s, o_w = unblk(o_s), unblk(o_w)
    g = jax.nn.sigmoid(gates.astype(jnp.float32)).astype(q.dtype)
    g = g.reshape(bsz, seq, kvh, grp, 3).transpose(0, 2, 3, 1, 4)
    o = g[..., 0:1] * o_cmp + g[..., 1:2] * o_s + g[..., 2:3] * o_w
    return o.transpose(0, 3, 1, 2, 4).reshape(bsz, seq, B_HEADS * hd)


def dsa_nsa_mixer(u, w_in, kv_norm, w_uk, w_uv, cmp_pos, cmp_w1, cmp_w2, w_out):
    bsz, seq, _ = u.shape
    pos = jnp.arange(seq)
    (a_q_nope, a_q_rope, a_ckv, a_k_rope, i_q, i_k, i_w, b_q, b_kv, b_gate) = split_cols(_proj(u, w_in), AB_WIDTHS)
    q_nope = a_q_nope.reshape(bsz, seq, A_HEADS, A_NOPE)
    q_rope = rope(a_q_rope.reshape(bsz, seq, A_HEADS, A_ROPE), pos)
    q_abs = jnp.einsum('blhd,hdc->blhc', q_nope, w_uk)
    q_cat = jnp.concatenate([q_abs, q_rope], axis=-1)
    c_kv = rms_norm(a_ckv, kv_norm)
    k_rope = rope(a_k_rope[:, :, None, :], pos)[:, :, 0]
    kv_lat = jnp.concatenate([c_kv, k_rope], axis=-1)
    idx_q = rope(i_q.reshape(bsz, seq, IDX_HEADS, IDX_DIM), pos)
    idx_k = rope(i_k[:, :, None, :], pos)[:, :, 0]
    o_a = dsa_attention(q_cat, kv_lat, idx_q, idx_k, i_w, w_uv)
    bq = rope(b_q.reshape(bsz, seq, B_HEADS, B_HEAD_DIM), pos)
    kv = b_kv.reshape(bsz, seq, 6, B_KV_HEADS, B_HEAD_DIM)
    o_b = nsa_attention(bq, rope(kv[:, :, 0], pos), kv[:, :, 1],
                        rope(kv[:, :, 2], pos), kv[:, :, 3],
                        rope(kv[:, :, 4], pos), kv[:, :, 5],
                        b_gate.reshape(bsz, seq, B_HEADS, 3), cmp_pos, cmp_w1, cmp_w2)
    return _proj(jnp.concatenate([o_a, o_b], axis=-1), w_out)


def chunk_gated_recurrence(q, k, v, log_f):
    _, bsz, nh, c, dk = q.shape
    causal = jnp.tril(jnp.ones((c, c), dtype=bool))[None, None, :, :, None]

    def step(state, inp):
        qc, kc, vc, lf = inp
        qf, kf, vf = qc.astype(jnp.float32), kc.astype(jnp.float32), vc.astype(jnp.float32)
        b = jnp.cumsum(lf, axis=2)
        diff = jnp.where(causal, b[:, :, :, None, :] - b[:, :, None, :, :], -jnp.inf)
        att = jnp.einsum('bhtd,bhtsd,bhsd->bhts', qf, jnp.exp(diff), kf)
        o = att @ vf + jnp.einsum('bhtd,bhde->bhte', qf * jnp.exp(b), state)
        b_last = b[:, :, -1, :]
        state = (jnp.exp(b_last)[..., None] * state
                 + jnp.einsum('bhsd,bhse->bhde', kf * jnp.exp(b_last[:, :, None, :] - b), vf))
        return state, o.astype(v.dtype)

    s0 = jnp.zeros((bsz, nh, dk, v.shape[-1]), jnp.float32)
    _, o = lax.scan(step, s0, (q, k, v, log_f))
    return o


def hgrn2_mixer(u, w_in, lower_bound, out_norm, w_out):
    bsz, seq, _ = u.shape
    q, f, i, g = jnp.split(_proj(u, w_in), 4, axis=-1)
    q = jax.nn.silu(q)
    fg = lower_bound + (1.0 - lower_bound) * jax.nn.sigmoid(f.astype(jnp.float32))
    k = (1.0 - fg).astype(u.dtype)
    log_f = jnp.log(fg)
    n_chunk = seq // C_CHUNK
    heads = lambda a: a.reshape(bsz, n_chunk, C_CHUNK, C_HEADS, C_HEAD_DIM).transpose(1, 0, 3, 2, 4)
    o = chunk_gated_recurrence(heads(q), heads(k), heads(i), heads(log_f))
    o = o.transpose(1, 0, 3, 2, 4).reshape(bsz, seq, C_HEADS, C_HEAD_DIM)
    o = rms_norm(o, out_norm.reshape(C_HEADS, C_HEAD_DIM)).reshape(bsz, seq, C_WIDTH)
    return _proj(o * jax.nn.silu(g), w_out)


def routed_experts(xf, e_id, e_w, w_gate_up, w_down):
    n_tok = xf.shape[0]
    n_as = e_id.shape[0]
    tok = jnp.arange(n_as) // GROUP_TOPK
    counts = jnp.zeros((N_EXPERTS,), jnp.int32).at[e_id].add(1)
    padded = (counts + MOE_BLOCK - 1) // MOE_BLOCK * MOE_BLOCK
    pad_end = jnp.cumsum(padded)
    pad_start = pad_end - padded
    start = jnp.cumsum(counts) - counts
    order = jnp.argsort(e_id)
    e_sorted = e_id[order]
    dest = pad_start[e_sorted] + jnp.arange(n_as) - start[e_sorted]
    n_blk = -(-(n_as + N_EXPERTS * (MOE_BLOCK - 1)) // MOE_BLOCK)
    rows = n_blk * MOE_BLOCK
    row_tok = jnp.zeros((rows,), jnp.int32).at[dest].set(tok[order])
    row_w = jnp.zeros((rows,), xf.dtype).at[dest].set(e_w[order])
    blk_e = jnp.minimum(jnp.searchsorted(pad_end, jnp.arange(n_blk) * MOE_BLOCK, side='right'), N_EXPERTS - 1)
    xb = xf[row_tok].reshape(n_blk, MOE_BLOCK, xf.shape[-1])

    def expert_block(args):
        xe, e = args
        gate, up = jnp.split(xe @ w_gate_up[e], 2, axis=-1)
        return (jax.nn.silu(gate) * up) @ w_down[e]

    yb = lax.map(expert_block, (xb, blk_e)).reshape(rows, xf.shape[-1])
    return jax.ops.segment_sum(yb * row_w[:, None], row_tok, num_segments=n_tok)


def hier_moe(u, w_group, b_group, w_expert, b_expert, w_gate_up, w_down):
    bsz, seq, d = u.shape
    n_tok = bsz * seq
    xf = u.reshape(n_tok, d)
    g_logits = (xf @ w_group).astype(jnp.float32) + b_group.astype(jnp.float32)
    g_prob = jax.nn.softmax(g_logits, axis=-1)
    grp = jnp.argmax(g_logits, axis=-1)
    g_w = jnp.take_along_axis(g_prob, grp[:, None], axis=-1)
    e_logits = ((xf @ w_expert).astype(jnp.float32) + b_expert.astype(jnp.float32))
    e_logits = e_logits.reshape(n_tok, N_GROUPS, EXPERTS_PER_GROUP)
    e_logits = jnp.take_along_axis(e_logits, grp[:, None, None], axis=1)[:, 0]
    top_v, top_i = lax.top_k(e_logits, GROUP_TOPK)
    e_w = jax.nn.softmax(top_v, axis=-1) * g_w
    e_id = grp[:, None] * EXPERTS_PER_GROUP + top_i
    y = routed_experts(xf, e_id.reshape(-1), e_w.reshape(-1).astype(u.dtype), w_gate_up, w_down)
    return y.reshape(bsz, seq, d)


def kernel(x, mix_norm, ffn_norm, final_norm, ab_w_in, dsa_kv_norm, dsa_w_uk, dsa_w_uv, nsa_cmp_pos, nsa_cmp_w1, nsa_cmp_w2, ab_w_out, hgrn_w_in, hgrn_lb_logits, hgrn_out_norm, hgrn_w_out, moe_w_group, moe_b_group, moe_w_expert, moe_b_expert, moe_w_gate_up, moe_w_down):
    lb_p = jax.nn.softmax(hgrn_lb_logits.astype(jnp.float32), axis=0)
    lower_bounds = jnp.cumsum(lb_p, axis=0) - lb_p[0]
    h = x
    for layer in range(DEPTH):
        j = layer // 2
        u = rms_norm(h, mix_norm[layer])
        if layer % 2 == 0:
            h = h + dsa_nsa_mixer(u, ab_w_in[j], dsa_kv_norm[j], dsa_w_uk[j], dsa_w_uv[j],
                                  nsa_cmp_pos[j], nsa_cmp_w1[j], nsa_cmp_w2[j], ab_w_out[j])
        else:
            h = h + hgrn2_mixer(u, hgrn_w_in[j], lower_bounds[layer], hgrn_out_norm[j], hgrn_w_out[j])
        u = rms_norm(h, ffn_norm[layer])
        h = h + hier_moe(u, moe_w_group[layer], moe_b_group[layer], moe_w_expert[layer],
                         moe_b_expert[layer], moe_w_gate_up[layer], moe_w_down[layer])
    return rms_norm(h, final_norm)
```

```python
import functools

import numpy as np
import jax
import jax.numpy as jnp
from jax import lax
from jax.experimental import pallas as pl
from jax.experimental.pallas import tpu as pltpu

D_MODEL = 4096
BATCH = 4
SEQ = 2048
DEPTH = 2

ROPE_THETA = 10000.0
NORM_EPS = 1e-6
Q_BLOCK = 64

A_HEADS = 16
A_NOPE = 128
A_ROPE = 64
A_KV_RANK = 512
A_V_DIM = 128
IDX_HEADS = 32
IDX_DIM = 64
IDX_TOPK = 256

B_HEADS = 16
B_KV_HEADS = 4
B_GROUP = B_HEADS // B_KV_HEADS
B_HEAD_DIM = 128
CMP_LEN = 32
CMP_STRIDE = 16
CMP_HIDDEN = 256
SLC_LEN = 64
SLC_COUNT = 16
WIN_LEN = 512

AB_WIDTHS = (
    A_HEADS * A_NOPE,
    A_HEADS * A_ROPE,
    A_KV_RANK,
    A_ROPE,
    IDX_HEADS * IDX_DIM,
    IDX_DIM,
    IDX_HEADS,
    B_HEADS * B_HEAD_DIM,
    6 * B_KV_HEADS * B_HEAD_DIM,
    3 * B_HEADS,
)
AB_IN = sum(AB_WIDTHS)
AB_OUT = A_HEADS * A_V_DIM + B_HEADS * B_HEAD_DIM

C_HEAD_DIM = 128
C_HEADS = D_MODEL // C_HEAD_DIM
C_WIDTH = C_HEADS * C_HEAD_DIM
C_CHUNK = 64

N_GROUPS = 8
EXPERTS_PER_GROUP = 8
N_EXPERTS = N_GROUPS * EXPERTS_PER_GROUP
GROUP_TOPK = 2
EXPERT_FF = 3 * D_MODEL // 32
MOE_BLOCK = 128

VMEM_LIMIT_BYTES = 48 * 1024 * 1024


def _mm_kernel(x_ref, w_ref, o_ref, acc_ref):
    k = pl.program_id(2)

    @pl.when(k == 0)
    def _():
        acc_ref[...] = jnp.zeros_like(acc_ref)

    acc_ref[...] += jnp.dot(x_ref[...].astype(jnp.bfloat16), w_ref[...].astype(jnp.bfloat16),
                            preferred_element_type=jnp.float32)

    @pl.when(k == pl.num_programs(2) - 1)
    def _():
        o_ref[...] = acc_ref[...]


def _matmul(x, w, tm=1024, tn=512, tk=1024):
    m, kd = x.shape
    n = w.shape[1]
    n_pad = -(-n // tn) * tn
    if n_pad != n:
        w = jnp.pad(w, ((0, 0), (0, n_pad - n)))
    tm = min(tm, m)
    tk = min(tk, kd)
    assert m % tm == 0 and kd % tk == 0
    out = pl.pallas_call(
        _mm_kernel,
        grid=(m // tm, n_pad // tn, kd // tk),
        in_specs=[pl.BlockSpec((tm, tk), lambda i, j, k: (i, k)),
                  pl.BlockSpec((tk, tn), lambda i, j, k: (k, j))],
        out_specs=pl.BlockSpec((tm, tn), lambda i, j, k: (i, j)),
        out_shape=jax.ShapeDtypeStruct((m, n_pad), jnp.float32),
        scratch_shapes=[pltpu.VMEM((tm, tn), jnp.float32)],
        compiler_params=pltpu.CompilerParams(
            dimension_semantics=("parallel", "parallel", "arbitrary"),
            vmem_limit_bytes=VMEM_LIMIT_BYTES),
        name="matmul",
    )(x, w)
    return out[:, :n] if n_pad != n else out


def _proj(u, w):
    lead = u.shape[:-1]
    return _matmul(u.reshape(-1, u.shape[-1]), w).reshape(*lead, w.shape[1])


TQ = 256
TK = 256
SUBLANES = 8
NEG_INF = float("-inf")
INT_MIN = -2 ** 31


def _fold8(x, op):
    r, c = x.shape
    return op(x.reshape(r // SUBLANES, SUBLANES, c), axis=0)


def _attend(k_ref, vt_ref, qt, bias_fn, c_lo, c_hi, s_scr, acc_scr, scale):
    tq = qt.shape[1]

    def pass1(c, m8):
        k0 = pl.multiple_of(c * TK, TK)
        s = jnp.dot(k_ref[pl.ds(k0, TK), :], qt, preferred_element_type=jnp.float32) * scale + bias_fn(c)
        s_scr[pl.ds(k0, TK), :] = s
        return jnp.maximum(m8, _fold8(s, jnp.max))

    m8 = lax.fori_loop(c_lo, c_hi, pass1, jnp.full((SUBLANES, tq), NEG_INF, jnp.float32))
    m = jnp.max(m8, axis=0, keepdims=True)
    m = jnp.where(m == NEG_INF, 0.0, m)
    acc_scr[...] = jnp.zeros_like(acc_scr)

    def pass2(c, d8):
        k0 = pl.multiple_of(c * TK, TK)
        p = jnp.exp(s_scr[pl.ds(k0, TK), :] - m)
        acc_scr[...] += jnp.dot(vt_ref[c], p.astype(jnp.bfloat16), preferred_element_type=jnp.float32)
        return d8 + _fold8(p, jnp.sum)

    d8 = lax.fori_loop(c_lo, c_hi, pass2, jnp.zeros((SUBLANES, tq), jnp.float32))
    den = jnp.sum(d8, axis=0, keepdims=True)
    return acc_scr[...] * (1.0 / jnp.where(den > 0, den, 1.0))


def _dsa_kernel(iq_ref, ik_ref, iw_ref, q_ref, kv_ref, ckvt_ref, wuvt_ref, o_ref,
                key_scr, bias_scr, s_scr, acc_scr, *, topk, idx_scale, scale):
    qb = pl.program_id(1)
    nk = qb + 1
    tpos = qb * TQ + lax.broadcasted_iota(jnp.int32, (1, TQ), 1)
    row = lax.broadcasted_iota(jnp.int32, (TK, 1), 0)
    w_rows = iw_ref[...] * idx_scale
    n_idx_heads = iq_ref.shape[0]

    def index_chunk(c, carry):
        k0 = pl.multiple_of(c * TK, TK)
        kblk = ik_ref[pl.ds(k0, TK), :]
        acc = jnp.zeros((TK, TQ), jnp.float32)
        for h in range(n_idx_heads):
            lg = jnp.dot(kblk, iq_ref[h], preferred_element_type=jnp.float32)
            acc = acc + w_rows[h:h + 1, :] * jnp.maximum(lg, 0.0)
        acc = acc + 0.0
        bits = lax.bitcast_convert_type(acc, jnp.int32)
        key = bits ^ ((bits >> 31) & 0x7FFFFFFF)
        key_scr[pl.ds(k0, TK), :] = jnp.where(k0 + row <= tpos, key, INT_MIN)
        return carry

    lax.fori_loop(0, nk, index_chunk, 0)

    def count(pred_fn):
        def body(c, cnt8):
            k0 = pl.multiple_of(c * TK, TK)
            hit = jnp.where(pred_fn(key_scr[pl.ds(k0, TK), :]), 1, 0).astype(jnp.int32)
            return cnt8 + _fold8(hit, jnp.sum)
        cnt8 = lax.fori_loop(0, nk, body, jnp.zeros((SUBLANES, TQ), jnp.int32))
        return jnp.sum(cnt8, axis=0, keepdims=True)

    thr = jnp.where(count(lambda kk: kk >= 0) >= topk, 0, INT_MIN).astype(jnp.int32) + jnp.zeros((1, TQ), jnp.int32)

    def bit_step(i, thr):
        cand = thr + jnp.left_shift(jnp.int32(1), 30 - i)
        return jnp.where(count(lambda kk: kk >= cand) >= topk, cand, thr)

    thr = lax.fori_loop(0, 31, bit_step, thr)
    thr = jnp.maximum(thr, INT_MIN + 1)
    need = (topk - count(lambda kk: kk > thr)).astype(jnp.float32)

    tri = jnp.where(lax.broadcasted_iota(jnp.int32, (TK, TK), 1) <= lax.broadcasted_iota(jnp.int32, (TK, TK), 0),
                    1.0, 0.0).astype(jnp.bfloat16)

    def bias_chunk(c, run):
        k0 = pl.multiple_of(c * TK, TK)
        kk = key_scr[pl.ds(k0, TK), :]
        eq = jnp.where(kk == thr, 1.0, 0.0)
        pref = jnp.dot(tri, eq.astype(jnp.bfloat16), preferred_element_type=jnp.float32) + run
        tie_ok = jnp.where(pref <= need, 0.0, NEG_INF)
        bias_scr[pl.ds(k0, TK), :] = jnp.where(kk > thr, 0.0, jnp.where(kk == thr, tie_ok, NEG_INF))
        return pref[TK - 1:TK, :]

    lax.fori_loop(0, nk, bias_chunk, jnp.zeros((1, TQ), jnp.float32))

    v_dim = wuvt_ref.shape[1]

    def head(h, carry):
        o_lat = _attend(kv_ref, ckvt_ref, q_ref[h], lambda c: bias_scr[pl.ds(pl.multiple_of(c * TK, TK), TK), :],
                        0, nk, s_scr, acc_scr, scale)
        o_ref[pl.ds(pl.multiple_of(h * v_dim, v_dim), v_dim), :] = jnp.dot(
            wuvt_ref[h], o_lat.astype(jnp.bfloat16), preferred_element_type=jnp.float32)
        return carry

    lax.fori_loop(0, q_ref.shape[0], head, 0)


def _dsa_attention_t(idx_qt, idx_k, idx_wt, q_catt, kv_lat, ckvt, w_uvt):
    bsz, n_ih, d_i, seq = idx_qt.shape
    n_h, d_qk = q_catt.shape[1:3]
    d_v, rank = w_uvt.shape[1:]
    topk = min(IDX_TOPK, seq // 4)
    kern = functools.partial(_dsa_kernel, topk=topk, idx_scale=(IDX_DIM ** -0.5) * (IDX_HEADS ** -0.5),
                             scale=(A_NOPE + A_ROPE) ** -0.5)
    return pl.pallas_call(
        kern,
        grid=(bsz, seq // TQ),
        in_specs=[
            pl.BlockSpec((None, n_ih, d_i, TQ), lambda b, q: (b, 0, 0, q)),
            pl.BlockSpec((None, seq, d_i), lambda b, q: (b, 0, 0)),
            pl.BlockSpec((None, n_ih, TQ), lambda b, q: (b, 0, q)),
            pl.BlockSpec((None, n_h, d_qk, TQ), lambda b, q: (b, 0, 0, q)),
            pl.BlockSpec((None, seq, d_qk), lambda b, q: (b, 0, 0)),
            pl.BlockSpec((None, seq // TK, rank, TK), lambda b, q: (b, 0, 0, 0)),
            pl.BlockSpec((n_h, d_v, rank), lambda b, q: (0, 0, 0)),
        ],
        out_specs=pl.BlockSpec((None, n_h * d_v, TQ), lambda b, q: (b, 0, q)),
        out_shape=jax.ShapeDtypeStruct((bsz, n_h * d_v, seq), jnp.float32),
        scratch_shapes=[pltpu.VMEM((seq, TQ), jnp.int32), pltpu.VMEM((seq, TQ), jnp.float32),
                        pltpu.VMEM((seq, TQ), jnp.float32), pltpu.VMEM((rank, TQ), jnp.float32)],
        compiler_params=pltpu.CompilerParams(dimension_semantics=("parallel", "arbitrary"),
                                             vmem_limit_bytes=VMEM_LIMIT_BYTES),
        name="dsa_attention",
    )(idx_qt, idx_k, idx_wt, q_catt, kv_lat, ckvt, w_uvt)


def _nsa_kernel(q_ref, kc_ref, vct_ref, ovt_ref, ks_ref, vst_ref, kw_ref, vwt_ref, g_ref, o_ref,
                blk_scr, bias_scr, s_scr, acc_scr, *, n_sel, scale):
    qb = pl.program_id(2)
    nk = qb + 1
    tpos = qb * TQ + lax.broadcasted_iota(jnp.int32, (1, TQ), 1)
    row = lax.broadcasted_iota(jnp.int32, (TK, 1), 0)
    n_grp, hd = q_ref.shape[0], q_ref.shape[1]
    n_cmp_pad = kc_ref.shape[0]
    n_slc = ovt_ref.shape[0]
    gate = lambda g, j: jax.nn.sigmoid(g_ref[3 * g + j:3 * g + j + 1, :])

    c_last = lax.broadcasted_iota(jnp.int32, (n_cmp_pad, 1), 0) * CMP_STRIDE + (CMP_LEN - 1)
    cbias = jnp.where(c_last <= tpos, 0.0, NEG_INF)
    psum = jnp.zeros((n_cmp_pad, TQ), jnp.float32)
    for g in range(n_grp):
        s = jnp.dot(kc_ref[...], q_ref[g], preferred_element_type=jnp.float32) * scale + cbias
        m = jnp.max(s, axis=0, keepdims=True)
        m = jnp.where(m == NEG_INF, 0.0, m)
        e = jnp.exp(s - m)
        den = jnp.sum(e, axis=0, keepdims=True)
        p = e * (1.0 / jnp.where(den > 0, den, 1.0))
        psum = psum + p
        o_cmp = jnp.dot(vct_ref[...], p.astype(jnp.bfloat16), preferred_element_type=jnp.float32)
        o_ref[pl.ds(g * hd, hd), :] = gate(g, 0) * o_cmp

    hi = psum.astype(jnp.bfloat16)
    r1 = psum - hi.astype(jnp.float32)
    mid = r1.astype(jnp.bfloat16)
    lo = (r1 - mid.astype(jnp.float32)).astype(jnp.bfloat16)
    ovt = ovt_ref[...]
    imp = (jnp.dot(ovt, hi, preferred_element_type=jnp.float32)
           + jnp.dot(ovt, mid, preferred_element_type=jnp.float32)
           + jnp.dot(ovt, lo, preferred_element_type=jnp.float32))
    jidx = lax.broadcasted_iota(jnp.int32, (n_slc, 1), 0)
    t_blk = tpos // SLC_LEN
    forced = (jidx == 0) | (jidx == t_blk) | (jidx == t_blk - 1)
    imp = jnp.where(forced, jnp.inf, imp)
    imp = jnp.where(jidx * SLC_LEN <= tpos, imp, NEG_INF)
    rank = jnp.zeros((n_slc, TQ), jnp.int32)
    for jp in range(n_slc):
        r = imp[jp:jp + 1, :]
        rank = rank + jnp.where(r > imp, 1, jnp.where(r == imp, jnp.where(jp < jidx, 1, 0), 0))
    blk_scr[...] = jnp.where(rank < n_sel, 0.0, NEG_INF)

    blocks_per_chunk = TK // SLC_LEN

    def slc_bias_chunk(c, carry):
        k0 = pl.multiple_of(c * TK, TK)
        rows = [jnp.broadcast_to(blk_scr[pl.ds(c * blocks_per_chunk + i, 1), :], (SLC_LEN, TQ))
                for i in range(blocks_per_chunk)]
        bias_scr[pl.ds(k0, TK), :] = jnp.where(k0 + row <= tpos, jnp.concatenate(rows, axis=0), NEG_INF)
        return carry

    lax.fori_loop(0, nk, slc_bias_chunk, 0)

    def slc_bias(c):
        return bias_scr[pl.ds(pl.multiple_of(c * TK, TK), TK), :]

    def win_bias(c):
        spos = c * TK + row
        return jnp.where(spos <= tpos, jnp.where(spos > tpos - WIN_LEN, 0.0, NEG_INF), NEG_INF)

    w_lo = jnp.maximum(qb - WIN_LEN // TK, 0)
    for g in range(n_grp):
        qt = q_ref[g]
        o_s = _attend(ks_ref, vst_ref, qt, slc_bias, 0, nk, s_scr, acc_scr, scale)
        o_ref[pl.ds(g * hd, hd), :] += gate(g, 1) * o_s
        o_w = _attend(kw_ref, vwt_ref, qt, win_bias, w_lo, nk, s_scr, acc_scr, scale)
        o_ref[pl.ds(g * hd, hd), :] += gate(g, 2) * o_w


def _nsa_attention_t(qt, kc, vct, ovt, ks, vst, kw, vwt, gates_t):
    bsz, kvh, grp, hd, seq = qt.shape
    ncp = kc.shape[2]
    n_slc = ovt.shape[0]
    kern = functools.partial(_nsa_kernel, n_sel=min(SLC_COUNT, n_slc), scale=hd ** -0.5)
    kv_spec = pl.BlockSpec((None, None, seq, hd), lambda b, h, q: (b, h, 0, 0))
    vt_spec = pl.BlockSpec((None, None, seq // TK, hd, TK), lambda b, h, q: (b, h, 0, 0, 0))
    return pl.pallas_call(
        kern,
        grid=(bsz, kvh, seq // TQ),
        in_specs=[
            pl.BlockSpec((None, None, grp, hd, TQ), lambda b, h, q: (b, h, 0, 0, q)),
            pl.BlockSpec((None, None, ncp, hd), lambda b, h, q: (b, h, 0, 0)),
            pl.BlockSpec((None, None, hd, ncp), lambda b, h, q: (b, h, 0, 0)),
            pl.BlockSpec((n_slc, ncp), lambda b, h, q: (0, 0)),
            kv_spec, vt_spec, kv_spec, vt_spec,
            pl.BlockSpec((None, None, 3 * grp, TQ), lambda b, h, q: (b, h, 0, q)),
        ],
        out_specs=pl.BlockSpec((None, grp * hd, TQ), lambda b, h, q: (b, h, q)),
        out_shape=jax.ShapeDtypeStruct((bsz, kvh * grp * hd, seq), jnp.float32),
        scratch_shapes=[pltpu.VMEM((n_slc, TQ), jnp.float32), pltpu.VMEM((seq, TQ), jnp.float32),
                        pltpu.VMEM((seq, TQ), jnp.float32), pltpu.VMEM((hd, TQ), jnp.float32)],
        compiler_params=pltpu.CompilerParams(dimension_semantics=("parallel", "parallel", "arbitrary"),
                                             vmem_limit_bytes=VMEM_LIMIT_BYTES),
        name="nsa_attention",
    )(qt, kc, vct, ovt, ks, vst, kw, vwt, gates_t)


def rms_norm(x, g):
    xf = x.astype(jnp.float32)
    y = xf * lax.rsqrt(jnp.mean(xf * xf, axis=-1, keepdims=True) + NORM_EPS)
    return (y * g.astype(jnp.float32)).astype(x.dtype)


def rope(x, pos):
    d = x.shape[-1]
    inv = ROPE_THETA ** (-jnp.arange(0, d, 2, dtype=jnp.float32) / d)
    ang = pos.astype(jnp.float32)[:, None] * inv[None, :]
    cos = jnp.cos(ang)[:, None, :]
    sin = jnp.sin(ang)[:, None, :]
    xf = x.astype(jnp.float32)
    x1, x2 = xf[..., : d // 2], xf[..., d // 2:]
    return jnp.concatenate([x1 * cos - x2 * sin, x2 * cos + x1 * sin], axis=-1).astype(x.dtype)


def masked_softmax(s, mask):
    s = jnp.where(mask, s.astype(jnp.float32), -jnp.inf)
    m = jnp.max(s, axis=-1, keepdims=True)
    m = jnp.where(jnp.isfinite(m), m, 0.0)
    e = jnp.where(mask, jnp.exp(s - m), 0.0)
    den = jnp.sum(e, axis=-1, keepdims=True)
    return e / jnp.where(den > 0, den, 1.0)


def split_cols(a, widths):
    return jnp.split(a, np.cumsum(widths)[:-1].tolist(), axis=-1)


def dsa_attention(q_cat, kv_lat, idx_q, idx_k, idx_w, w_uv):
    bsz, seq = kv_lat.shape[:2]
    topk = min(IDX_TOPK, seq // 4)
    key_pos = jnp.arange(seq)
    scale = (A_NOPE + A_ROPE) ** -0.5
    idx_scale = (IDX_DIM ** -0.5) * (IDX_HEADS ** -0.5)

    def block(i):
        q0 = i * Q_BLOCK
        qpos = q0 + jnp.arange(Q_BLOCK)
        sl = lambda a: lax.dynamic_slice_in_dim(a, q0, Q_BLOCK, axis=1)
        logits = jnp.einsum('bqhd,bsd->bqhs', sl(idx_q), idx_k)
        score = jnp.einsum('bqh,bqhs->bqs', sl(idx_w), jax.nn.relu(logits)) * idx_scale
        causal = key_pos[None, :] <= qpos[:, None]
        score = jnp.where(causal[None], score.astype(jnp.float32), -jnp.inf)
        _, sel = lax.top_k(score, topk)
        valid = sel <= qpos[None, :, None]
        kv_sel = jax.vmap(lambda a, ix: a[ix])(kv_lat, sel)
        s = jnp.einsum('bqhc,bqkc->bqhk', sl(q_cat), kv_sel) * scale
        p = masked_softmax(s, valid[:, :, None, :]).astype(kv_lat.dtype)
        o_lat = jnp.einsum('bqhk,bqkc->bqhc', p, kv_sel[..., :A_KV_RANK])
        return jnp.einsum('bqhc,hcd->bqhd', o_lat, w_uv)

    out = lax.map(block, jnp.arange(seq // Q_BLOCK))
    return out.transpose(1, 0, 2, 3, 4).reshape(bsz, seq, A_HEADS * A_V_DIM)


def nsa_attention(q, k_cmp, v_cmp, k_slc, v_slc, k_win, v_win, gates, cmp_pos, cmp_w1, cmp_w2):
    bsz, seq = q.shape[:2]
    kvh, grp, hd = B_KV_HEADS, B_GROUP, B_HEAD_DIM
    scale = hd ** -0.5
    pos = jnp.arange(seq)
    qg = q.reshape(bsz, seq, kvh, grp, hd).transpose(0, 2, 3, 1, 4)

    n_cmp = (seq - CMP_LEN) // CMP_STRIDE + 1
    tok = np.arange(n_cmp)[:, None] * CMP_STRIDE + np.arange(CMP_LEN)[None, :]

    def compress(a, j):
        blk = a[:, tok] + cmp_pos[j][:, None, :]
        blk = blk.transpose(0, 1, 3, 2, 4).reshape(bsz, n_cmp, kvh, CMP_LEN * hd)
        return jax.nn.gelu(blk @ cmp_w1[j]) @ cmp_w2[j]

    kc, vc = compress(k_cmp, 0), compress(v_cmp, 1)
    cmp_start = jnp.arange(n_cmp) * CMP_STRIDE
    cmp_end = cmp_start + CMP_LEN
    cmp_mask = (cmp_end - 1)[None, :] <= pos[:, None]
    s_c = jnp.einsum('bhgtd,bchd->bhgtc', qg, kc) * scale
    p_c = masked_softmax(s_c, cmp_mask)
    o_cmp = jnp.einsum('bhgtc,bchd->bhgtd', p_c.astype(q.dtype), vc)

    n_slc = seq // SLC_LEN
    slc_start = jnp.arange(n_slc) * SLC_LEN
    overlap = ((cmp_start[:, None] < slc_start[None, :] + SLC_LEN)
               & (cmp_end[:, None] > slc_start[None, :])).astype(jnp.float32)
    imp = jnp.einsum('bhgtc,cj->bhtj', p_c, overlap)
    blk = jnp.arange(n_slc)[None, :]
    t_blk = (pos // SLC_LEN)[:, None]
    forced = (blk == 0) | (blk == t_blk) | (blk == t_blk - 1)
    imp = jnp.where(forced, jnp.inf, imp)
    imp = jnp.where(blk * SLC_LEN <= pos[:, None], imp, -jnp.inf)
    n_sel = min(SLC_COUNT, n_slc)
    _, slc_idx = lax.top_k(imp, n_sel)

    ks_b = k_slc.reshape(bsz, n_slc, SLC_LEN, kvh, hd).transpose(0, 3, 1, 2, 4)
    vs_b = v_slc.reshape(bsz, n_slc, SLC_LEN, kvh, hd).transpose(0, 3, 1, 2, 4)
    pad = ((0, 0), (WIN_LEN, 0), (0, 0), (0, 0))
    kw_pad = jnp.pad(k_win, pad).transpose(0, 2, 1, 3)
    vw_pad = jnp.pad(v_win, pad).transpose(0, 2, 1, 3)
    gather_blocks = jax.vmap(jax.vmap(lambda blocks, ix: blocks[ix]))

    def block(i):
        q0 = i * Q_BLOCK
        qpos = q0 + jnp.arange(Q_BLOCK)
        qb = lax.dynamic_slice_in_dim(qg, q0, Q_BLOCK, axis=3)
        sel = lax.dynamic_slice_in_dim(slc_idx, q0, Q_BLOCK, axis=2)
        sel_flat = sel.reshape(bsz, kvh, Q_BLOCK * n_sel)
        kg = gather_blocks(ks_b, sel_flat).reshape(bsz, kvh, Q_BLOCK, n_sel * SLC_LEN, hd)
        vg = gather_blocks(vs_b, sel_flat).reshape(bsz, kvh, Q_BLOCK, n_sel * SLC_LEN, hd)
        tpos = (sel[..., None] * SLC_LEN + jnp.arange(SLC_LEN)).reshape(bsz, kvh, Q_BLOCK, n_sel * SLC_LEN)
        s = jnp.einsum('bhgqd,bhqkd->bhgqk', qb, kg) * scale
        p = masked_softmax(s, (tpos <= qpos[:, None])[:, :, None]).astype(q.dtype)
        o_s = jnp.einsum('bhgqk,bhqkd->bhgqd', p, vg)
        kw = lax.dynamic_slice_in_dim(kw_pad, q0, WIN_LEN + Q_BLOCK, axis=2)
        vw = lax.dynamic_slice_in_dim(vw_pad, q0, WIN_LEN + Q_BLOCK, axis=2)
        wpos = q0 - WIN_LEN + jnp.arange(WIN_LEN + Q_BLOCK)
        wmask = ((wpos[None, :] <= qpos[:, None]) & (wpos[None, :] > qpos[:, None] - WIN_LEN)
                 & (wpos[None, :] >= 0))
        s = jnp.einsum('bhgqd,bhkd->bhgqk', qb, kw) * scale
        p = masked_softmax(s, wmask).astype(q.dtype)
        o_w = jnp.einsum('bhgqk,bhkd->bhgqd', p, vw)
        return o_s, o_w

    o_s, o_w = lax.map(block, jnp.arange(seq // Q_BLOCK))
    unblk = lambda o: o.transpose(1, 2, 3, 0, 4, 5).reshape(bsz, kvh, grp, seq, hd)
    o_s, o_w = unblk(o_s), unblk(o_w)
    g = jax.nn.sigmoid(gates.astype(jnp.float32)).astype(q.dtype)
    g = g.reshape(bsz, seq, kvh, grp, 3).transpose(0, 2, 3, 1, 4)
    o = g[..., 0:1] * o_cmp + g[..., 1:2] * o_s + g[..., 2:3] * o_w
    return o.transpose(0, 3, 1, 2, 4).reshape(bsz, seq, B_HEADS * hd)


def dsa_nsa_mixer(u, w_in, kv_norm, w_uk, w_uv, cmp_pos, cmp_w1, cmp_w2, w_out):
    bsz, seq, _ = u.shape
    pos = jnp.arange(seq)
    (a_q_nope, a_q_rope, a_ckv, a_k_rope, i_q, i_k, i_w, b_q, b_kv, b_gate) = split_cols(_proj(u, w_in), AB_WIDTHS)
    q_nope = a_q_nope.reshape(bsz, seq, A_HEADS, A_NOPE)
    q_rope = rope(a_q_rope.reshape(bsz, seq, A_HEADS, A_ROPE), pos)
    q_abs = jnp.einsum('blhd,hdc->blhc', q_nope, w_uk)
    q_cat = jnp.concatenate([q_abs, q_rope], axis=-1)
    c_kv = rms_norm(a_ckv, kv_norm)
    k_rope = rope(a_k_rope[:, :, None, :], pos)[:, :, 0]
    kv_lat = jnp.concatenate([c_kv, k_rope], axis=-1)
    idx_q = rope(i_q.reshape(bsz, seq, IDX_HEADS, IDX_DIM), pos)
    idx_k = rope(i_k[:, :, None, :], pos)[:, :, 0]
    bf = jnp.bfloat16
    o_a_t = _dsa_attention_t(
        idx_q.transpose(0, 2, 3, 1).astype(bf), idx_k.astype(bf), i_w.transpose(0, 2, 1),
        q_cat.transpose(0, 2, 3, 1).astype(bf), kv_lat.astype(bf),
        c_kv.reshape(bsz, seq // TK, TK, A_KV_RANK).transpose(0, 1, 3, 2).astype(bf),
        w_uv.transpose(0, 2, 1).astype(bf))
    bq = rope(b_q.reshape(bsz, seq, B_HEADS, B_HEAD_DIM), pos)
    kv = b_kv.reshape(bsz, seq, 6, B_KV_HEADS, B_HEAD_DIM)
    o_b_t = nsa_attention_t(bq, rope(kv[:, :, 0], pos), kv[:, :, 1],
                            rope(kv[:, :, 2], pos), kv[:, :, 3],
                            rope(kv[:, :, 4], pos), kv[:, :, 5],
                            b_gate, cmp_pos, cmp_w1, cmp_w2)
    o_t = jnp.concatenate([o_a_t, o_b_t], axis=1)
    return _proj(o_t.transpose(0, 2, 1), w_out)


def nsa_attention_t(q, k_cmp, v_cmp, k_slc, v_slc, k_win, v_win, gates, cmp_pos, cmp_w1, cmp_w2):
    bsz, seq = q.shape[:2]
    kvh, grp, hd = B_KV_HEADS, B_GROUP, B_HEAD_DIM
    bf = jnp.bfloat16
    n_cmp = (seq - CMP_LEN) // CMP_STRIDE + 1
    n_cmp_pad = -(-n_cmp // 128) * 128
    tok = np.arange(n_cmp)[:, None] * CMP_STRIDE + np.arange(CMP_LEN)[None, :]

    def compress(a, j):
        blk = a[:, tok] + cmp_pos[j][:, None, :]
        blk = blk.transpose(0, 1, 3, 2, 4).reshape(bsz, n_cmp, kvh, CMP_LEN * hd)
        out = jax.nn.gelu(blk @ cmp_w1[j]) @ cmp_w2[j]
        return jnp.pad(out, ((0, 0), (0, n_cmp_pad - n_cmp), (0, 0), (0, 0)))

    kc = compress(k_cmp, 0).transpose(0, 2, 1, 3).astype(bf)
    vct = compress(v_cmp, 1).transpose(0, 2, 3, 1).astype(bf)
    n_slc = seq // SLC_LEN
    cmp_start = np.arange(n_cmp_pad) * CMP_STRIDE
    slc_start = np.arange(n_slc) * SLC_LEN
    ovt = ((cmp_start[None, :] < slc_start[:, None] + SLC_LEN)
           & (cmp_start[None, :] + CMP_LEN > slc_start[:, None]) & (np.arange(n_cmp_pad)[None, :] < n_cmp))
    ovt = jnp.asarray(ovt, bf)
    qt = q.reshape(bsz, seq, kvh, grp, hd).transpose(0, 2, 3, 4, 1).astype(bf)
    k_t = lambda a: a.transpose(0, 2, 1, 3).astype(bf)
    v_t = lambda a: a.reshape(bsz, seq // TK, TK, kvh, hd).transpose(0, 3, 1, 4, 2).astype(bf)
    gates_t = gates.reshape(bsz, seq, kvh, 3 * grp).transpose(0, 2, 3, 1)
    return _nsa_attention_t(qt, kc, vct, ovt, k_t(k_slc), v_t(v_slc), k_t(k_win), v_t(v_win), gates_t)


def chunk_gated_recurrence(q, k, v, log_f):
    _, bsz, nh, c, dk = q.shape
    causal = jnp.tril(jnp.ones((c, c), dtype=bool))[None, None, :, :, None]

    def step(state, inp):
        qc, kc, vc, lf = inp
        qf, kf, vf = qc.astype(jnp.float32), kc.astype(jnp.float32), vc.astype(jnp.float32)
        b = jnp.cumsum(lf, axis=2)
        diff = jnp.where(causal, b[:, :, :, None, :] - b[:, :, None, :, :], -jnp.inf)
        att = jnp.einsum('bhtd,bhtsd,bhsd->bhts', qf, jnp.exp(diff), kf)
        o = att @ vf + jnp.einsum('bhtd,bhde->bhte', qf * jnp.exp(b), state)
        b_last = b[:, :, -1, :]
        state = (jnp.exp(b_last)[..., None] * state
                 + jnp.einsum('bhsd,bhse->bhde', kf * jnp.exp(b_last[:, :, None, :] - b), vf))
        return state, o.astype(v.dtype)

    s0 = jnp.zeros((bsz, nh, dk, v.shape[-1]), jnp.float32)
    _, o = lax.scan(step, s0, (q, k, v, log_f))
    return o


def hgrn2_mixer(u, w_in, lower_bound, out_norm, w_out):
    bsz, seq, _ = u.shape
    q, f, i, g = jnp.split(_proj(u, w_in), 4, axis=-1)
    q = jax.nn.silu(q)
    fg = lower_bound + (1.0 - lower_bound) * jax.nn.sigmoid(f.astype(jnp.float32))
    k = (1.0 - fg).astype(u.dtype)
    log_f = jnp.log(fg)
    n_chunk = seq // C_CHUNK
    heads = lambda a: a.reshape(bsz, n_chunk, C_CHUNK, C_HEADS, C_HEAD_DIM).transpose(1, 0, 3, 2, 4)
    o = chunk_gated_recurrence(heads(q), heads(k), heads(i), heads(log_f))
    o = o.transpose(1, 0, 3, 2, 4).reshape(bsz, seq, C_HEADS, C_HEAD_DIM)
    o = rms_norm(o, out_norm.reshape(C_HEADS, C_HEAD_DIM)).reshape(bsz, seq, C_WIDTH)
    return _proj(o * jax.nn.silu(g), w_out)


def routed_experts(xf, e_id, e_w, w_gate_up, w_down):
    n_tok = xf.shape[0]
    n_as = e_id.shape[0]
    tok = jnp.arange(n_as) // GROUP_TOPK
    counts = jnp.zeros((N_EXPERTS,), jnp.int32).at[e_id].add(1)
    padded = (counts + MOE_BLOCK - 1) // MOE_BLOCK * MOE_BLOCK
    pad_end = jnp.cumsum(padded)
    pad_start = pad_end - padded
    start = jnp.cumsum(counts) - counts
    order = jnp.argsort(e_id)
    e_sorted = e_id[order]
    dest = pad_start[e_sorted] + jnp.arange(n_as) - start[e_sorted]
    n_blk = -(-(n_as + N_EXPERTS * (MOE_BLOCK - 1)) // MOE_BLOCK)
    rows = n_blk * MOE_BLOCK
    row_tok = jnp.zeros((rows,), jnp.int32).at[dest].set(tok[order])
    row_w = jnp.zeros((rows,), xf.dtype).at[dest].set(e_w[order])
    blk_e = jnp.minimum(jnp.searchsorted(pad_end, jnp.arange(n_blk) * MOE_BLOCK, side='right'), N_EXPERTS - 1)
    xb = xf[row_tok].reshape(n_blk, MOE_BLOCK, xf.shape[-1])

    def expert_block(args):
        xe, e = args
        gate, up = jnp.split(xe @ w_gate_up[e], 2, axis=-1)
        return (jax.nn.silu(gate) * up) @ w_down[e]

    yb = lax.map(expert_block, (xb, blk_e)).reshape(rows, xf.shape[-1])
    return jax.ops.segment_sum(yb * row_w[:, None], row_tok, num_segments=n_tok)


def hier_moe(u, w_group, b_group, w_expert, b_expert, w_gate_up, w_down):
    bsz, seq, d = u.shape
    n_tok = bsz * seq
    xf = u.reshape(n_tok, d)
    g_logits = (xf @ w_group).astype(jnp.float32) + b_group.astype(jnp.float32)
    g_prob = jax.nn.softmax(g_logits, axis=-1)
    grp = jnp.argmax(g_logits, axis=-1)
    g_w = jnp.take_along_axis(g_prob, grp[:, None], axis=-1)
    e_logits = ((xf @ w_expert).astype(jnp.float32) + b_expert.astype(jnp.float32))
    e_logits = e_logits.reshape(n_tok, N_GROUPS, EXPERTS_PER_GROUP)
    e_logits = jnp.take_along_axis(e_logits, grp[:, None, None], axis=1)[:, 0]
    top_v, top_i = lax.top_k(e_logits, GROUP_TOPK)
    e_w = jax.nn.softmax(top_v, axis=-1) * g_w
    e_id = grp[:, None] * EXPERTS_PER_GROUP + top_i
    y = routed_experts(xf, e_id.reshape(-1), e_w.reshape(-1).astype(u.dtype), w_gate_up, w_down)
    return y.reshape(bsz, seq, d)


def kernel(x, mix_norm, ffn_norm, final_norm, ab_w_in, dsa_kv_norm, dsa_w_uk, dsa_w_uv, nsa_cmp_pos, nsa_cmp_w1, nsa_cmp_w2, ab_w_out, hgrn_w_in, hgrn_lb_logits, hgrn_out_norm, hgrn_w_out, moe_w_group, moe_b_group, moe_w_expert, moe_b_expert, moe_w_gate_up, moe_w_down):
    lb_p = jax.nn.softmax(hgrn_lb_logits.astype(jnp.float32), axis=0)
    lower_bounds = jnp.cumsum(lb_p, axis=0) - lb_p[0]
    h = x
    for layer in range(DEPTH):
        j = layer // 2
        u = rms_norm(h, mix_norm[layer])
        if layer % 2 == 0:
            h = h + dsa_nsa_mixer(u, ab_w_in[j], dsa_kv_norm[j], dsa_w_uk[j], dsa_w_uv[j],
                                  nsa_cmp_pos[j], nsa_cmp_w1[j], nsa_cmp_w2[j], ab_w_out[j])
        else:
            h = h + hgrn2_mixer(u, hgrn_w_in[j], lower_bounds[layer], hgrn_out_norm[j], hgrn_w_out[j])
        u = rms_norm(h, ffn_norm[layer])
        h = h + hier_moe(u, moe_w_group[layer], moe_b_group[layer], moe_w_expert[layer],
                         moe_b_expert[layer], moe_w_gate_up[layer], moe_w_down[layer])
    return rms_norm(h, final_norm)
```

```python
import functools

import numpy as np
import jax
import jax.numpy as jnp
from jax import lax
from jax.experimental import pallas as pl
from jax.experimental.pallas import tpu as pltpu

D_MODEL = 4096
BATCH = 4
SEQ = 2048
DEPTH = 2

ROPE_THETA = 10000.0
NORM_EPS = 1e-6
Q_BLOCK = 64

A_HEADS = 16
A_NOPE = 128
A_ROPE = 64
A_KV_RANK = 512
A_V_DIM = 128
IDX_HEADS = 32
IDX_DIM = 64
IDX_TOPK = 256

B_HEADS = 16
B_KV_HEADS = 4
B_GROUP = B_HEADS // B_KV_HEADS
B_HEAD_DIM = 128
CMP_LEN = 32
CMP_STRIDE = 16
CMP_HIDDEN = 256
SLC_LEN = 64
SLC_COUNT = 16
WIN_LEN = 512

AB_WIDTHS = (
    A_HEADS * A_NOPE,
    A_HEADS * A_ROPE,
    A_KV_RANK,
    A_ROPE,
    IDX_HEADS * IDX_DIM,
    IDX_DIM,
    IDX_HEADS,
    B_HEADS * B_HEAD_DIM,
    6 * B_KV_HEADS * B_HEAD_DIM,
    3 * B_HEADS,
)
AB_IN = sum(AB_WIDTHS)
AB_OUT = A_HEADS * A_V_DIM + B_HEADS * B_HEAD_DIM

C_HEAD_DIM = 128
C_HEADS = D_MODEL // C_HEAD_DIM
C_WIDTH = C_HEADS * C_HEAD_DIM
C_CHUNK = 64

N_GROUPS = 8
EXPERTS_PER_GROUP = 8
N_EXPERTS = N_GROUPS * EXPERTS_PER_GROUP
GROUP_TOPK = 2
EXPERT_FF = 3 * D_MODEL // 32
MOE_BLOCK = 128

VMEM_LIMIT_BYTES = 48 * 1024 * 1024


MM_TM = 1024
MM_TN = 512


def _mm_kernel(x_ref, w_ref, o_ref):
    o_ref[...] = jnp.dot(x_ref[...], w_ref[...].astype(jnp.bfloat16), preferred_element_type=jnp.float32)


def _matmul(x, w):
    m, kd = x.shape
    n = w.shape[1]
    assert m % MM_TM == 0
    return pl.pallas_call(
        _mm_kernel,
        grid=(pl.cdiv(n, MM_TN), m // MM_TM),
        in_specs=[pl.BlockSpec((MM_TM, kd), lambda j, i: (i, 0)),
                  pl.BlockSpec((kd, MM_TN), lambda j, i: (0, j))],
        out_specs=pl.BlockSpec((MM_TM, MM_TN), lambda j, i: (i, j)),
        out_shape=jax.ShapeDtypeStruct((m, n), jnp.float32),
        compiler_params=pltpu.CompilerParams(
            dimension_semantics=("parallel", "parallel"),
            vmem_limit_bytes=VMEM_LIMIT_BYTES),
        name="matmul",
    )(x, w)


def _proj(u, w):
    lead = u.shape[:-1]
    return _matmul(u.reshape(-1, u.shape[-1]).astype(jnp.bfloat16), w).reshape(*lead, w.shape[1])


MOE_KC = 1024
MOE_VMEM_LIMIT_BYTES = 58 * 1024 * 1024


def _moe_kernel(blk_e_ref, n_used_ref, x_ref, wgu_ref, wd_ref, rw_ref, o_ref):
    del blk_e_ref
    i = pl.program_id(0)
    d_model, ff2 = wgu_ref.shape
    ff = ff2 // 2

    @pl.when(i < n_used_ref[0])
    def _():
        gu = jnp.zeros((x_ref.shape[0], ff2), jnp.float32)
        for k0 in range(0, d_model, MOE_KC):
            gu = gu + jnp.dot(x_ref[:, k0:k0 + MOE_KC], wgu_ref[k0:k0 + MOE_KC, :].astype(jnp.bfloat16),
                              preferred_element_type=jnp.float32)
        hid = (jax.nn.silu(gu[:, :ff]) * gu[:, ff:]).astype(jnp.bfloat16)
        rw = rw_ref[...]
        for n0 in range(0, d_model, MOE_KC):
            o_ref[:, n0:n0 + MOE_KC] = rw * jnp.dot(hid, wd_ref[:, n0:n0 + MOE_KC].astype(jnp.bfloat16),
                                                    preferred_element_type=jnp.float32)

    @pl.when(i >= n_used_ref[0])
    def _():
        o_ref[...] = jnp.zeros_like(o_ref)


def _moe_blocks(blk_e, n_used, xb, w_gate_up, w_down, row_w):
    rows, d = xb.shape
    n_blk = rows // MOE_BLOCK
    ff2 = w_gate_up.shape[-1]
    grid_spec = pltpu.PrefetchScalarGridSpec(
        num_scalar_prefetch=2,
        grid=(n_blk,),
        in_specs=[pl.BlockSpec((MOE_BLOCK, d), lambda i, be, nu: (i, 0)),
                  pl.BlockSpec((None, d, ff2), lambda i, be, nu: (be[i], 0, 0)),
                  pl.BlockSpec((None, ff2 // 2, d), lambda i, be, nu: (be[i], 0, 0)),
                  pl.BlockSpec((MOE_BLOCK, 1), lambda i, be, nu: (i, 0))],
        out_specs=pl.BlockSpec((MOE_BLOCK, d), lambda i, be, nu: (i, 0)),
    )
    return pl.pallas_call(
        _moe_kernel,
        grid_spec=grid_spec,
        out_shape=jax.ShapeDtypeStruct((rows, d), jnp.float32),
        compiler_params=pltpu.CompilerParams(dimension_semantics=("arbitrary",),
                                             vmem_limit_bytes=MOE_VMEM_LIMIT_BYTES),
        name="moe_experts",
    )(blk_e, n_used, xb, w_gate_up, w_down, row_w)


TQ = 256
TK = 256
SUBLANES = 8
NEG_INF = float("-inf")
INT_MIN = -2 ** 31


def _fold8(x, op):
    r, c = x.shape
    return op(x.reshape(r // SUBLANES, SUBLANES, c), axis=0)


def _attend(k_ref, vt_ref, qt, bias_fn, c_lo, c_hi, s_scr, acc_scr, scale):
    tq = qt.shape[1]

    def pass1(c, m8):
        k0 = pl.multiple_of(c * TK, TK)
        s = jnp.dot(k_ref[pl.ds(k0, TK), :], qt, preferred_element_type=jnp.float32) * scale + bias_fn(c)
        s_scr[pl.ds(k0, TK), :] = s
        return jnp.maximum(m8, _fold8(s, jnp.max))

    m8 = lax.fori_loop(c_lo, c_hi, pass1, jnp.full((SUBLANES, tq), NEG_INF, jnp.float32))
    m = jnp.max(m8, axis=0, keepdims=True)
    m = jnp.where(m == NEG_INF, 0.0, m)
    acc_scr[...] = jnp.zeros_like(acc_scr)

    def pass2(c, d8):
        k0 = pl.multiple_of(c * TK, TK)
        p = jnp.exp(s_scr[pl.ds(k0, TK), :] - m)
        acc_scr[...] += jnp.dot(vt_ref[c], p.astype(jnp.bfloat16), preferred_element_type=jnp.float32)
        return d8 + _fold8(p, jnp.sum)

    d8 = lax.fori_loop(c_lo, c_hi, pass2, jnp.zeros((SUBLANES, tq), jnp.float32))
    den = jnp.sum(d8, axis=0, keepdims=True)
    return acc_scr[...] * (1.0 / jnp.where(den > 0, den, 1.0))


def _dsa_kernel(iq_ref, ik_ref, iw_ref, q_ref, kv_ref, ckvt_ref, wuvt_ref, o_ref,
                key_scr, bias_scr, s_scr, acc_scr, *, topk, idx_scale, scale):
    qb = pl.program_id(1)
    nk = qb + 1
    tpos = qb * TQ + lax.broadcasted_iota(jnp.int32, (1, TQ), 1)
    row = lax.broadcasted_iota(jnp.int32, (TK, 1), 0)
    w_rows = iw_ref[...] * idx_scale
    n_idx_heads = iq_ref.shape[0]

    def index_chunk(c, carry):
        k0 = pl.multiple_of(c * TK, TK)
        kblk = ik_ref[pl.ds(k0, TK), :]
        acc = jnp.zeros((TK, TQ), jnp.float32)
        for h in range(n_idx_heads):
            lg = jnp.dot(kblk, iq_ref[h], preferred_element_type=jnp.float32)
            acc = acc + w_rows[h:h + 1, :] * jnp.maximum(lg, 0.0)
        acc = acc + 0.0
        bits = lax.bitcast_convert_type(acc, jnp.int32)
        key = bits ^ ((bits >> 31) & 0x7FFFFFFF)
        key_scr[pl.ds(k0, TK), :] = jnp.where(k0 + row <= tpos, key, INT_MIN)
        return carry

    lax.fori_loop(0, nk, index_chunk, 0)

    def count(pred_fn):
        def body(c, cnt8):
            k0 = pl.multiple_of(c * TK, TK)
            hit = jnp.where(pred_fn(key_scr[pl.ds(k0, TK), :]), 1, 0).astype(jnp.int32)
            return cnt8 + _fold8(hit, jnp.sum)
        cnt8 = lax.fori_loop(0, nk, body, jnp.zeros((SUBLANES, TQ), jnp.int32))
        return jnp.sum(cnt8, axis=0, keepdims=True)

    thr = jnp.where(count(lambda kk: kk >= 0) >= topk, 0, INT_MIN).astype(jnp.int32) + jnp.zeros((1, TQ), jnp.int32)

    def bit_step(i, thr):
        cand = thr + jnp.left_shift(jnp.int32(1), 30 - i)
        return jnp.where(count(lambda kk: kk >= cand) >= topk, cand, thr)

    thr = lax.fori_loop(0, 31, bit_step, thr)
    thr = jnp.maximum(thr, INT_MIN + 1)
    need = (topk - count(lambda kk: kk > thr)).astype(jnp.float32)

    tri = jnp.where(lax.broadcasted_iota(jnp.int32, (TK, TK), 1) <= lax.broadcasted_iota(jnp.int32, (TK, TK), 0),
                    1.0, 0.0).astype(jnp.bfloat16)

    def bias_chunk(c, run):
        k0 = pl.multiple_of(c * TK, TK)
        kk = key_scr[pl.ds(k0, TK), :]
        eq = jnp.where(kk == thr, 1.0, 0.0)
        pref = jnp.dot(tri, eq.astype(jnp.bfloat16), preferred_element_type=jnp.float32) + run
        tie_ok = jnp.where(pref <= need, 0.0, NEG_INF)
        bias_scr[pl.ds(k0, TK), :] = jnp.where(kk > thr, 0.0, jnp.where(kk == thr, tie_ok, NEG_INF))
        return pref[TK - 1:TK, :]

    lax.fori_loop(0, nk, bias_chunk, jnp.zeros((1, TQ), jnp.float32))

    v_dim = wuvt_ref.shape[1]

    def head(h, carry):
        o_lat = _attend(kv_ref, ckvt_ref, q_ref[h], lambda c: bias_scr[pl.ds(pl.multiple_of(c * TK, TK), TK), :],
                        0, nk, s_scr, acc_scr, scale)
        o_ref[pl.ds(pl.multiple_of(h * v_dim, v_dim), v_dim), :] = jnp.dot(
            wuvt_ref[h], o_lat.astype(jnp.bfloat16), preferred_element_type=jnp.float32)
        return carry

    lax.fori_loop(0, q_ref.shape[0], head, 0)


def _dsa_attention_t(idx_qt, idx_k, idx_wt, q_catt, kv_lat, ckvt, w_uvt):
    bsz, n_ih, d_i, seq = idx_qt.shape
    n_h, d_qk = q_catt.shape[1:3]
    d_v, rank = w_uvt.shape[1:]
    topk = min(IDX_TOPK, seq // 4)
    kern = functools.partial(_dsa_kernel, topk=topk, idx_scale=(IDX_DIM ** -0.5) * (IDX_HEADS ** -0.5),
                             scale=(A_NOPE + A_ROPE) ** -0.5)
    return pl.pallas_call(
        kern,
        grid=(bsz, seq // TQ),
        in_specs=[
            pl.BlockSpec((None, n_ih, d_i, TQ), lambda b, q: (b, 0, 0, q)),
            pl.BlockSpec((None, seq, d_i), lambda b, q: (b, 0, 0)),
            pl.BlockSpec((None, n_ih, TQ), lambda b, q: (b, 0, q)),
            pl.BlockSpec((None, n_h, d_qk, TQ), lambda b, q: (b, 0, 0, q)),
            pl.BlockSpec((None, seq, d_qk), lambda b, q: (b, 0, 0)),
            pl.BlockSpec((None, seq // TK, rank, TK), lambda b, q: (b, 0, 0, 0)),
            pl.BlockSpec((n_h, d_v, rank), lambda b, q: (0, 0, 0)),
        ],
        out_specs=pl.BlockSpec((None, n_h * d_v, TQ), lambda b, q: (b, 0, q)),
        out_shape=jax.ShapeDtypeStruct((bsz, n_h * d_v, seq), jnp.float32),
        scratch_shapes=[pltpu.VMEM((seq, TQ), jnp.int32), pltpu.VMEM((seq, TQ), jnp.float32),
                        pltpu.VMEM((seq, TQ), jnp.float32), pltpu.VMEM((rank, TQ), jnp.float32)],
        compiler_params=pltpu.CompilerParams(dimension_semantics=("parallel", "arbitrary"),
                                             vmem_limit_bytes=VMEM_LIMIT_BYTES),
        name="dsa_attention",
    )(idx_qt, idx_k, idx_wt, q_catt, kv_lat, ckvt, w_uvt)


def _nsa_kernel(q_ref, kc_ref, vct_ref, ovt_ref, ks_ref, vst_ref, kw_ref, vwt_ref, g_ref, o_ref,
                blk_scr, bias_scr, s_scr, acc_scr, *, n_sel, scale):
    qb = pl.program_id(2)
    nk = qb + 1
    tpos = qb * TQ + lax.broadcasted_iota(jnp.int32, (1, TQ), 1)
    row = lax.broadcasted_iota(jnp.int32, (TK, 1), 0)
    n_grp, hd = q_ref.shape[0], q_ref.shape[1]
    n_cmp_pad = kc_ref.shape[0]
    n_slc = ovt_ref.shape[0]
    gate = lambda g, j: jax.nn.sigmoid(g_ref[3 * g + j:3 * g + j + 1, :])

    c_last = lax.broadcasted_iota(jnp.int32, (n_cmp_pad, 1), 0) * CMP_STRIDE + (CMP_LEN - 1)
    cbias = jnp.where(c_last <= tpos, 0.0, NEG_INF)
    psum = jnp.zeros((n_cmp_pad, TQ), jnp.float32)
    for g in range(n_grp):
        s = jnp.dot(kc_ref[...], q_ref[g], preferred_element_type=jnp.float32) * scale + cbias
        m = jnp.max(s, axis=0, keepdims=True)
        m = jnp.where(m == NEG_INF, 0.0, m)
        e = jnp.exp(s - m)
        den = jnp.sum(e, axis=0, keepdims=True)
        p = e * (1.0 / jnp.where(den > 0, den, 1.0))
        psum = psum + p
        o_cmp = jnp.dot(vct_ref[...], p.astype(jnp.bfloat16), preferred_element_type=jnp.float32)
        o_ref[pl.ds(g * hd, hd), :] = gate(g, 0) * o_cmp

    hi = psum.astype(jnp.bfloat16)
    r1 = psum - hi.astype(jnp.float32)
    mid = r1.astype(jnp.bfloat16)
    lo = (r1 - mid.astype(jnp.float32)).astype(jnp.bfloat16)
    ovt = ovt_ref[...]
    imp = (jnp.dot(ovt, hi, preferred_element_type=jnp.float32)
           + jnp.dot(ovt, mid, preferred_element_type=jnp.float32)
           + jnp.dot(ovt, lo, preferred_element_type=jnp.float32))
    jidx = lax.broadcasted_iota(jnp.int32, (n_slc, 1), 0)
    t_blk = tpos // SLC_LEN
    forced = (jidx == 0) | (jidx == t_blk) | (jidx == t_blk - 1)
    imp = jnp.where(forced, jnp.inf, imp)
    imp = jnp.where(jidx * SLC_LEN <= tpos, imp, NEG_INF)
    rank = jnp.zeros((n_slc, TQ), jnp.int32)
    for jp in range(n_slc):
        r = imp[jp:jp + 1, :]
        rank = rank + jnp.where(r > imp, 1, jnp.where(r == imp, jnp.where(jp < jidx, 1, 0), 0))
    blk_scr[...] = jnp.where(rank < n_sel, 0.0, NEG_INF)

    blocks_per_chunk = TK // SLC_LEN

    def slc_bias_chunk(c, carry):
        k0 = pl.multiple_of(c * TK, TK)
        rows = [jnp.broadcast_to(blk_scr[pl.ds(c * blocks_per_chunk + i, 1), :], (SLC_LEN, TQ))
                for i in range(blocks_per_chunk)]
        bias_scr[pl.ds(k0, TK), :] = jnp.where(k0 + row <= tpos, jnp.concatenate(rows, axis=0), NEG_INF)
        return carry

    lax.fori_loop(0, nk, slc_bias_chunk, 0)

    def slc_bias(c):
        return bias_scr[pl.ds(pl.multiple_of(c * TK, TK), TK), :]

    def win_bias(c):
        spos = c * TK + row
        return jnp.where(spos <= tpos, jnp.where(spos > tpos - WIN_LEN, 0.0, NEG_INF), NEG_INF)

    w_lo = jnp.maximum(qb - WIN_LEN // TK, 0)
    for g in range(n_grp):
        qt = q_ref[g]
        o_s = _attend(ks_ref, vst_ref, qt, slc_bias, 0, nk, s_scr, acc_scr, scale)
        o_ref[pl.ds(g * hd, hd), :] += gate(g, 1) * o_s
        o_w = _attend(kw_ref, vwt_ref, qt, win_bias, w_lo, nk, s_scr, acc_scr, scale)
        o_ref[pl.ds(g * hd, hd), :] += gate(g, 2) * o_w


def _nsa_attention_t(qt, kc, vct, ovt, ks, vst, kw, vwt, gates_t):
    bsz, kvh, grp, hd, seq = qt.shape
    ncp = kc.shape[2]
    n_slc = ovt.shape[0]
    kern = functools.partial(_nsa_kernel, n_sel=min(SLC_COUNT, n_slc), scale=hd ** -0.5)
    kv_spec = pl.BlockSpec((None, None, seq, hd), lambda b, h, q: (b, h, 0, 0))
    vt_spec = pl.BlockSpec((None, None, seq // TK, hd, TK), lambda b, h, q: (b, h, 0, 0, 0))
    return pl.pallas_call(
        kern,
        grid=(bsz, kvh, seq // TQ),
        in_specs=[
            pl.BlockSpec((None, None, grp, hd, TQ), lambda b, h, q: (b, h, 0, 0, q)),
            pl.BlockSpec((None, None, ncp, hd), lambda b, h, q: (b, h, 0, 0)),
            pl.BlockSpec((None, None, hd, ncp), lambda b, h, q: (b, h, 0, 0)),
            pl.BlockSpec((n_slc, ncp), lambda b, h, q: (0, 0)),
            kv_spec, vt_spec, kv_spec, vt_spec,
            pl.BlockSpec((None, None, 3 * grp, TQ), lambda b, h, q: (b, h, 0, q)),
        ],
        out_specs=pl.BlockSpec((None, grp * hd, TQ), lambda b, h, q: (b, h, q)),
        out_shape=jax.ShapeDtypeStruct((bsz, kvh * grp * hd, seq), jnp.float32),
        scratch_shapes=[pltpu.VMEM((n_slc, TQ), jnp.float32), pltpu.VMEM((seq, TQ), jnp.float32),
                        pltpu.VMEM((seq, TQ), jnp.float32), pltpu.VMEM((hd, TQ), jnp.float32)],
        compiler_params=pltpu.CompilerParams(dimension_semantics=("parallel", "parallel", "arbitrary"),
                                             vmem_limit_bytes=VMEM_LIMIT_BYTES),
        name="nsa_attention",
    )(qt, kc, vct, ovt, ks, vst, kw, vwt, gates_t)


def rms_norm(x, g):
    xf = x.astype(jnp.float32)
    y = xf * lax.rsqrt(jnp.mean(xf * xf, axis=-1, keepdims=True) + NORM_EPS)
    return (y * g.astype(jnp.float32)).astype(x.dtype)


def rope(x, pos):
    d = x.shape[-1]
    inv = ROPE_THETA ** (-jnp.arange(0, d, 2, dtype=jnp.float32) / d)
    ang = pos.astype(jnp.float32)[:, None] * inv[None, :]
    cos = jnp.cos(ang)[:, None, :]
    sin = jnp.sin(ang)[:, None, :]
    xf = x.astype(jnp.float32)
    x1, x2 = xf[..., : d // 2], xf[..., d // 2:]
    return jnp.concatenate([x1 * cos - x2 * sin, x2 * cos + x1 * sin], axis=-1).astype(x.dtype)


def split_cols(a, widths):
    return jnp.split(a, np.cumsum(widths)[:-1].tolist(), axis=-1)


def dsa_nsa_mixer(u, w_in, kv_norm, w_uk, w_uv, cmp_pos, cmp_w1, cmp_w2, w_out):
    bsz, seq, _ = u.shape
    pos = jnp.arange(seq)
    (a_q_nope, a_q_rope, a_ckv, a_k_rope, i_q, i_k, i_w, b_q, b_kv, b_gate) = split_cols(_proj(u, w_in), AB_WIDTHS)
    q_nope = a_q_nope.reshape(bsz, seq, A_HEADS, A_NOPE)
    q_rope = rope(a_q_rope.reshape(bsz, seq, A_HEADS, A_ROPE), pos)
    q_abs = jnp.einsum('blhd,hdc->blhc', q_nope, w_uk)
    q_cat = jnp.concatenate([q_abs, q_rope], axis=-1)
    c_kv = rms_norm(a_ckv, kv_norm)
    k_rope = rope(a_k_rope[:, :, None, :], pos)[:, :, 0]
    kv_lat = jnp.concatenate([c_kv, k_rope], axis=-1)
    idx_q = rope(i_q.reshape(bsz, seq, IDX_HEADS, IDX_DIM), pos)
    idx_k = rope(i_k[:, :, None, :], pos)[:, :, 0]
    bf = jnp.bfloat16
    o_a_t = _dsa_attention_t(
        idx_q.transpose(0, 2, 3, 1).astype(bf), idx_k.astype(bf), i_w.transpose(0, 2, 1),
        q_cat.transpose(0, 2, 3, 1).astype(bf), kv_lat.astype(bf),
        c_kv.reshape(bsz, seq // TK, TK, A_KV_RANK).transpose(0, 1, 3, 2).astype(bf),
        w_uv.transpose(0, 2, 1).astype(bf))
    bq = rope(b_q.reshape(bsz, seq, B_HEADS, B_HEAD_DIM), pos)
    kv = b_kv.reshape(bsz, seq, 6, B_KV_HEADS, B_HEAD_DIM)
    o_b_t = nsa_attention_t(bq, rope(kv[:, :, 0], pos), kv[:, :, 1],
                            rope(kv[:, :, 2], pos), kv[:, :, 3],
                            rope(kv[:, :, 4], pos), kv[:, :, 5],
                            b_gate, cmp_pos, cmp_w1, cmp_w2)
    o_t = jnp.concatenate([o_a_t, o_b_t], axis=1)
    return _proj(o_t.transpose(0, 2, 1), w_out)


def nsa_attention_t(q, k_cmp, v_cmp, k_slc, v_slc, k_win, v_win, gates, cmp_pos, cmp_w1, cmp_w2):
    bsz, seq = q.shape[:2]
    kvh, grp, hd = B_KV_HEADS, B_GROUP, B_HEAD_DIM
    bf = jnp.bfloat16
    n_cmp = (seq - CMP_LEN) // CMP_STRIDE + 1
    n_cmp_pad = -(-n_cmp // 128) * 128
    tok = np.arange(n_cmp)[:, None] * CMP_STRIDE + np.arange(CMP_LEN)[None, :]

    def compress(a, j):
        blk = a[:, tok] + cmp_pos[j][:, None, :]
        blk = blk.transpose(0, 1, 3, 2, 4).reshape(bsz, n_cmp, kvh, CMP_LEN * hd)
        out = jax.nn.gelu(blk @ cmp_w1[j]) @ cmp_w2[j]
        return jnp.pad(out, ((0, 0), (0, n_cmp_pad - n_cmp), (0, 0), (0, 0)))

    kc = compress(k_cmp, 0).transpose(0, 2, 1, 3).astype(bf)
    vct = compress(v_cmp, 1).transpose(0, 2, 3, 1).astype(bf)
    n_slc = seq // SLC_LEN
    cmp_start = np.arange(n_cmp_pad) * CMP_STRIDE
    slc_start = np.arange(n_slc) * SLC_LEN
    ovt = ((cmp_start[None, :] < slc_start[:, None] + SLC_LEN)
           & (cmp_start[None, :] + CMP_LEN > slc_start[:, None]) & (np.arange(n_cmp_pad)[None, :] < n_cmp))
    ovt = jnp.asarray(ovt, bf)
    qt = q.reshape(bsz, seq, kvh, grp, hd).transpose(0, 2, 3, 4, 1).astype(bf)
    k_t = lambda a: a.transpose(0, 2, 1, 3).astype(bf)
    v_t = lambda a: a.reshape(bsz, seq // TK, TK, kvh, hd).transpose(0, 3, 1, 4, 2).astype(bf)
    gates_t = gates.reshape(bsz, seq, kvh, 3 * grp).transpose(0, 2, 3, 1)
    return _nsa_attention_t(qt, kc, vct, ovt, k_t(k_slc), v_t(v_slc), k_t(k_win), v_t(v_win), gates_t)


def chunk_gated_recurrence(q, k, v, log_f):
    _, bsz, nh, c, dk = q.shape
    causal = jnp.tril(jnp.ones((c, c), dtype=bool))[None, None, :, :, None]

    def step(state, inp):
        qc, kc, vc, lf = inp
        qf, kf, vf = qc.astype(jnp.float32), kc.astype(jnp.float32), vc.astype(jnp.float32)
        b = jnp.cumsum(lf, axis=2)
        diff = jnp.where(causal, b[:, :, :, None, :] - b[:, :, None, :, :], -jnp.inf)
        att = jnp.einsum('bhtd,bhtsd,bhsd->bhts', qf, jnp.exp(diff), kf)
        o = att @ vf + jnp.einsum('bhtd,bhde->bhte', qf * jnp.exp(b), state)
        b_last = b[:, :, -1, :]
        state = (jnp.exp(b_last)[..., None] * state
                 + jnp.einsum('bhsd,bhse->bhde', kf * jnp.exp(b_last[:, :, None, :] - b), vf))
        return state, o.astype(v.dtype)

    s0 = jnp.zeros((bsz, nh, dk, v.shape[-1]), jnp.float32)
    _, o = lax.scan(step, s0, (q, k, v, log_f))
    return o


def hgrn2_mixer(u, w_in, lower_bound, out_norm, w_out):
    bsz, seq, _ = u.shape
    q, f, i, g = jnp.split(_proj(u, w_in), 4, axis=-1)
    q = jax.nn.silu(q)
    fg = lower_bound + (1.0 - lower_bound) * jax.nn.sigmoid(f.astype(jnp.float32))
    k = (1.0 - fg).astype(u.dtype)
    log_f = jnp.log(fg)
    n_chunk = seq // C_CHUNK
    heads = lambda a: a.reshape(bsz, n_chunk, C_CHUNK, C_HEADS, C_HEAD_DIM).transpose(1, 0, 3, 2, 4)
    o = chunk_gated_recurrence(heads(q), heads(k), heads(i), heads(log_f))
    o = o.transpose(1, 0, 3, 2, 4).reshape(bsz, seq, C_HEADS, C_HEAD_DIM)
    o = rms_norm(o, out_norm.reshape(C_HEADS, C_HEAD_DIM)).reshape(bsz, seq, C_WIDTH)
    return _proj(o * jax.nn.silu(g), w_out)


def routed_experts(xf, e_id, e_w, w_gate_up, w_down):
    n_tok = xf.shape[0]
    n_as = e_id.shape[0]
    tok = jnp.arange(n_as, dtype=jnp.int32) // GROUP_TOPK
    onehot = (e_id[:, None] == jnp.arange(N_EXPERTS, dtype=e_id.dtype)[None, :]).astype(jnp.int32)
    csum = jnp.cumsum(onehot, axis=0)
    counts = csum[-1]
    rank = jnp.take_along_axis(csum, e_id[:, None], axis=1)[:, 0] - 1
    padded = (counts + MOE_BLOCK - 1) // MOE_BLOCK * MOE_BLOCK
    pad_end = jnp.cumsum(padded)
    pad_start = pad_end - padded
    dest = pad_start[e_id] + rank
    n_blk = -(-(n_as + N_EXPERTS * (MOE_BLOCK - 1)) // MOE_BLOCK)
    rows = n_blk * MOE_BLOCK
    row_tok = jnp.zeros((rows,), jnp.int32).at[dest].set(tok)
    row_w = jnp.zeros((rows,), jnp.float32).at[dest].set(e_w)
    blk_e = jnp.minimum(jnp.searchsorted(pad_end, jnp.arange(n_blk) * MOE_BLOCK, side='right'),
                        N_EXPERTS - 1).astype(jnp.int32)
    n_used = (pad_end[-1:] // MOE_BLOCK).astype(jnp.int32)
    xb = xf.astype(jnp.bfloat16)[row_tok]
    yb = _moe_blocks(blk_e, n_used, xb, w_gate_up, w_down, row_w[:, None])
    return yb[dest].reshape(n_tok, GROUP_TOPK, xf.shape[-1]).sum(axis=1)


def hier_moe(u, w_group, b_group, w_expert, b_expert, w_gate_up, w_down):
    bsz, seq, d = u.shape
    n_tok = bsz * seq
    xf = u.reshape(n_tok, d)
    hp = lax.Precision.HIGHEST
    g_logits = jnp.dot(xf, w_group, precision=hp).astype(jnp.float32) + b_group.astype(jnp.float32)
    g_prob = jax.nn.softmax(g_logits, axis=-1)
    grp = jnp.argmax(g_logits, axis=-1)
    g_w = jnp.take_along_axis(g_prob, grp[:, None], axis=-1)
    e_logits = (jnp.dot(xf, w_expert, precision=hp).astype(jnp.float32) + b_expert.astype(jnp.float32))
    e_logits = e_logits.reshape(n_tok, N_GROUPS, EXPERTS_PER_GROUP)
    e_logits = jnp.take_along_axis(e_logits, grp[:, None, None], axis=1)[:, 0]
    top_v, top_i = lax.top_k(e_logits, GROUP_TOPK)
    e_w = jax.nn.softmax(top_v, axis=-1) * g_w
    e_id = (grp[:, None] * EXPERTS_PER_GROUP + top_i).astype(jnp.int32)
    y = routed_experts(xf, e_id.reshape(-1), e_w.reshape(-1).astype(jnp.float32), w_gate_up, w_down)
    return y.reshape(bsz, seq, d)


def kernel(x, mix_norm, ffn_norm, final_norm, ab_w_in, dsa_kv_norm, dsa_w_uk, dsa_w_uv, nsa_cmp_pos, nsa_cmp_w1, nsa_cmp_w2, ab_w_out, hgrn_w_in, hgrn_lb_logits, hgrn_out_norm, hgrn_w_out, moe_w_group, moe_b_group, moe_w_expert, moe_b_expert, moe_w_gate_up, moe_w_down):
    lb_p = jax.nn.softmax(hgrn_lb_logits.astype(jnp.float32), axis=0)
    lower_bounds = jnp.cumsum(lb_p, axis=0) - lb_p[0]
    h = x
    for layer in range(DEPTH):
        j = layer // 2
        u = rms_norm(h, mix_norm[layer])
        if layer % 2 == 0:
            h = h + dsa_nsa_mixer(u, ab_w_in[j], dsa_kv_norm[j], dsa_w_uk[j], dsa_w_uv[j],
                                  nsa_cmp_pos[j], nsa_cmp_w1[j], nsa_cmp_w2[j], ab_w_out[j])
        else:
            h = h + hgrn2_mixer(u, hgrn_w_in[j], lower_bounds[layer], hgrn_out_norm[j], hgrn_w_out[j])
        u = rms_norm(h, ffn_norm[layer])
        h = h + hier_moe(u, moe_w_group[layer], moe_b_group[layer], moe_w_expert[layer],
                         moe_b_expert[layer], moe_w_gate_up[layer], moe_w_down[layer])
    return rms_norm(h, final_norm)
```

```python
import functools

import numpy as np
import jax
import jax.numpy as jnp
from jax import lax
from jax.experimental import pallas as pl
from jax.experimental.pallas import tpu as pltpu

D_MODEL = 4096
BATCH = 4
SEQ = 2048
DEPTH = 2

ROPE_THETA = 10000.0
NORM_EPS = 1e-6
Q_BLOCK = 64

A_HEADS = 16
A_NOPE = 128
A_ROPE = 64
A_KV_RANK = 512
A_V_DIM = 128
IDX_HEADS = 32
IDX_DIM = 64
IDX_TOPK = 256

B_HEADS = 16
B_KV_HEADS = 4
B_GROUP = B_HEADS // B_KV_HEADS
B_HEAD_DIM = 128
CMP_LEN = 32
CMP_STRIDE = 16
CMP_HIDDEN = 256
SLC_LEN = 64
SLC_COUNT = 16
WIN_LEN = 512

AB_WIDTHS = (
    A_HEADS * A_NOPE,
    A_HEADS * A_ROPE,
    A_KV_RANK,
    A_ROPE,
    IDX_HEADS * IDX_DIM,
    IDX_DIM,
    IDX_HEADS,
    B_HEADS * B_HEAD_DIM,
    6 * B_KV_HEADS * B_HEAD_DIM,
    3 * B_HEADS,
)
AB_IN = sum(AB_WIDTHS)
AB_OUT = A_HEADS * A_V_DIM + B_HEADS * B_HEAD_DIM

C_HEAD_DIM = 128
C_HEADS = D_MODEL // C_HEAD_DIM
C_WIDTH = C_HEADS * C_HEAD_DIM
C_CHUNK = 64

N_GROUPS = 8
EXPERTS_PER_GROUP = 8
N_EXPERTS = N_GROUPS * EXPERTS_PER_GROUP
GROUP_TOPK = 2
EXPERT_FF = 3 * D_MODEL // 32
MOE_BLOCK = 128

VMEM_LIMIT_BYTES = 48 * 1024 * 1024


MM_TM = 1024
MM_TN = 512


def _mm_kernel(x_ref, w_ref, o_ref):
    o_ref[...] = jnp.dot(x_ref[...], w_ref[...].astype(jnp.bfloat16), preferred_element_type=jnp.float32)


def _matmul(x, w):
    m, kd = x.shape
    n = w.shape[1]
    assert m % MM_TM == 0
    return pl.pallas_call(
        _mm_kernel,
        grid=(pl.cdiv(n, MM_TN), m // MM_TM),
        in_specs=[pl.BlockSpec((MM_TM, kd), lambda j, i: (i, 0)),
                  pl.BlockSpec((kd, MM_TN), lambda j, i: (0, j))],
        out_specs=pl.BlockSpec((MM_TM, MM_TN), lambda j, i: (i, j)),
        out_shape=jax.ShapeDtypeStruct((m, n), jnp.float32),
        compiler_params=pltpu.CompilerParams(
            dimension_semantics=("parallel", "parallel"),
            vmem_limit_bytes=VMEM_LIMIT_BYTES),
        name="matmul",
    )(x, w)


def _proj(u, w):
    lead = u.shape[:-1]
    return _matmul(u.reshape(-1, u.shape[-1]).astype(jnp.bfloat16), w).reshape(*lead, w.shape[1])


MOE_KC = 1024
MOE_VMEM_LIMIT_BYTES = 58 * 1024 * 1024


def _moe_kernel(blk_e_ref, n_used_ref, x_ref, wgu_ref, wd_ref, rw_ref, o_ref):
    del blk_e_ref
    i = pl.program_id(0)
    d_model, ff2 = wgu_ref.shape
    ff = ff2 // 2

    @pl.when(i < n_used_ref[0])
    def _():
        gu = jnp.zeros((x_ref.shape[0], ff2), jnp.float32)
        for k0 in range(0, d_model, MOE_KC):
            gu = gu + jnp.dot(x_ref[:, k0:k0 + MOE_KC], wgu_ref[k0:k0 + MOE_KC, :].astype(jnp.bfloat16),
                              preferred_element_type=jnp.float32)
        hid = (jax.nn.silu(gu[:, :ff]) * gu[:, ff:]).astype(jnp.bfloat16)
        rw = rw_ref[...]
        for n0 in range(0, d_model, MOE_KC):
            o_ref[:, n0:n0 + MOE_KC] = rw * jnp.dot(hid, wd_ref[:, n0:n0 + MOE_KC].astype(jnp.bfloat16),
                                                    preferred_element_type=jnp.float32)

    @pl.when(i >= n_used_ref[0])
    def _():
        o_ref[...] = jnp.zeros_like(o_ref)


def _moe_blocks(blk_e, n_used, xb, w_gate_up, w_down, row_w):
    rows, d = xb.shape
    n_blk = rows // MOE_BLOCK
    ff2 = w_gate_up.shape[-1]
    grid_spec = pltpu.PrefetchScalarGridSpec(
        num_scalar_prefetch=2,
        grid=(n_blk,),
        in_specs=[pl.BlockSpec((MOE_BLOCK, d), lambda i, be, nu: (i, 0)),
                  pl.BlockSpec((None, d, ff2), lambda i, be, nu: (be[i], 0, 0)),
                  pl.BlockSpec((None, ff2 // 2, d), lambda i, be, nu: (be[i], 0, 0)),
                  pl.BlockSpec((MOE_BLOCK, 1), lambda i, be, nu: (i, 0))],
        out_specs=pl.BlockSpec((MOE_BLOCK, d), lambda i, be, nu: (i, 0)),
    )
    return pl.pallas_call(
        _moe_kernel,
        grid_spec=grid_spec,
        out_shape=jax.ShapeDtypeStruct((rows, d), jnp.float32),
        compiler_params=pltpu.CompilerParams(dimension_semantics=("arbitrary",),
                                             vmem_limit_bytes=MOE_VMEM_LIMIT_BYTES),
        name="moe_experts",
    )(blk_e, n_used, xb, w_gate_up, w_down, row_w)


TQ = 256
TK = 512
IK = 256
SUBLANES = 8
NEG_INF = float("-inf")
INT_MIN = -2 ** 31


def _fold8(x, op):
    r, c = x.shape
    return op(x.reshape(r // SUBLANES, SUBLANES, c), axis=0)


def _attend(k_ref, vt_ref, qt, bias_fn, c_lo, c_hi, s_scr, acc_scr, scale):
    tq = qt.shape[1]

    def pass1(c, m8):
        k0 = pl.multiple_of(c * TK, TK)
        s = jnp.dot(k_ref[pl.ds(k0, TK), :], qt, preferred_element_type=jnp.float32) * scale + bias_fn(c)
        s_scr[pl.ds(k0, TK), :] = s
        return jnp.maximum(m8, _fold8(s, jnp.max))

    m8 = lax.fori_loop(c_lo, c_hi, pass1, jnp.full((SUBLANES, tq), NEG_INF, jnp.float32))
    m = jnp.max(m8, axis=0, keepdims=True)
    m = jnp.where(m == NEG_INF, 0.0, m)
    acc_scr[...] = jnp.zeros_like(acc_scr)

    def pass2(c, d8):
        k0 = pl.multiple_of(c * TK, TK)
        p = jnp.exp(s_scr[pl.ds(k0, TK), :] - m)
        acc_scr[...] += jnp.dot(vt_ref[c], p.astype(jnp.bfloat16), preferred_element_type=jnp.float32)
        return d8 + _fold8(p, jnp.sum)

    d8 = lax.fori_loop(c_lo, c_hi, pass2, jnp.zeros((SUBLANES, tq), jnp.float32))
    den = jnp.sum(d8, axis=0, keepdims=True)
    return acc_scr[...] * (1.0 / jnp.where(den > 0, den, 1.0))


def _dsa_kernel(iq_ref, ik_ref, iw_ref, q_ref, kv_ref, ckvt_ref, wuvt_ref, o_ref,
                key_scr, bias_scr, s_scr, acc_scr, *, topk, idx_scale, scale):
    qb = pl.program_id(1)
    nk = ((qb + 1) * TQ + TK - 1) // TK
    n_ik = nk * (TK // IK)
    tpos = qb * TQ + lax.broadcasted_iota(jnp.int32, (1, TQ), 1)
    row = lax.broadcasted_iota(jnp.int32, (IK, 1), 0)
    w_rows = iw_ref[...] * idx_scale
    n_idx_heads = iq_ref.shape[0]

    def index_chunk(c, carry):
        k0 = pl.multiple_of(c * IK, IK)
        kblk = ik_ref[pl.ds(k0, IK), :]
        acc = jnp.zeros((IK, TQ), jnp.float32)
        for h in range(n_idx_heads):
            lg = jnp.dot(kblk, iq_ref[h], preferred_element_type=jnp.float32)
            acc = acc + w_rows[h:h + 1, :] * jnp.maximum(lg, 0.0)
        acc = acc + 0.0
        bits = lax.bitcast_convert_type(acc, jnp.int32)
        key = bits ^ ((bits >> 31) & 0x7FFFFFFF)
        key_scr[pl.ds(k0, IK), :] = jnp.where(k0 + row <= tpos, key, INT_MIN)
        return carry

    lax.fori_loop(0, n_ik, index_chunk, 0)

    def count(pred_fn):
        def body(c, cnt8):
            k0 = pl.multiple_of(c * IK, IK)
            hit = jnp.where(pred_fn(key_scr[pl.ds(k0, IK), :]), 1, 0).astype(jnp.int32)
            return cnt8 + _fold8(hit, jnp.sum)
        cnt8 = lax.fori_loop(0, n_ik, body, jnp.zeros((SUBLANES, TQ), jnp.int32))
        return jnp.sum(cnt8, axis=0, keepdims=True)

    thr = jnp.where(count(lambda kk: kk >= 0) >= topk, 0, INT_MIN).astype(jnp.int32) + jnp.zeros((1, TQ), jnp.int32)

    def bit_step(i, thr):
        cand = thr + jnp.left_shift(jnp.int32(1), 30 - i)
        return jnp.where(count(lambda kk: kk >= cand) >= topk, cand, thr)

    thr = lax.fori_loop(0, 31, bit_step, thr)
    thr = jnp.maximum(thr, INT_MIN + 1)
    need = (topk - count(lambda kk: kk > thr)).astype(jnp.float32)

    tri = jnp.where(lax.broadcasted_iota(jnp.int32, (IK, IK), 1) <= lax.broadcasted_iota(jnp.int32, (IK, IK), 0),
                    1.0, 0.0).astype(jnp.bfloat16)

    def bias_chunk(c, run):
        k0 = pl.multiple_of(c * IK, IK)
        kk = key_scr[pl.ds(k0, IK), :]
        eq = jnp.where(kk == thr, 1.0, 0.0)
        pref = jnp.dot(tri, eq.astype(jnp.bfloat16), preferred_element_type=jnp.float32) + run
        tie_ok = jnp.where(pref <= need, 0.0, NEG_INF)
        bias_scr[pl.ds(k0, IK), :] = jnp.where(kk > thr, 0.0, jnp.where(kk == thr, tie_ok, NEG_INF))
        return pref[IK - 1:IK, :]

    lax.fori_loop(0, n_ik, bias_chunk, jnp.zeros((1, TQ), jnp.float32))

    v_dim = wuvt_ref.shape[1]

    def head(h, carry):
        o_lat = _attend(kv_ref, ckvt_ref, q_ref[h], lambda c: bias_scr[pl.ds(pl.multiple_of(c * TK, TK), TK), :],
                        0, nk, s_scr, acc_scr, scale)
        o_ref[pl.ds(pl.multiple_of(h * v_dim, v_dim), v_dim), :] = jnp.dot(
            wuvt_ref[h], o_lat.astype(jnp.bfloat16), preferred_element_type=jnp.float32)
        return carry

    lax.fori_loop(0, q_ref.shape[0], head, 0)


def _dsa_attention_t(idx_qt, idx_k, idx_wt, q_catt, kv_lat, ckvt, w_uvt):
    bsz, n_ih, d_i, seq = idx_qt.shape
    n_h, d_qk = q_catt.shape[1:3]
    d_v, rank = w_uvt.shape[1:]
    topk = min(IDX_TOPK, seq // 4)
    kern = functools.partial(_dsa_kernel, topk=topk, idx_scale=(IDX_DIM ** -0.5) * (IDX_HEADS ** -0.5),
                             scale=(A_NOPE + A_ROPE) ** -0.5)
    return pl.pallas_call(
        kern,
        grid=(bsz, seq // TQ),
        in_specs=[
            pl.BlockSpec((None, n_ih, d_i, TQ), lambda b, q: (b, 0, 0, q)),
            pl.BlockSpec((None, seq, d_i), lambda b, q: (b, 0, 0)),
            pl.BlockSpec((None, n_ih, TQ), lambda b, q: (b, 0, q)),
            pl.BlockSpec((None, n_h, d_qk, TQ), lambda b, q: (b, 0, 0, q)),
            pl.BlockSpec((None, seq, d_qk), lambda b, q: (b, 0, 0)),
            pl.BlockSpec((None, seq // TK, rank, TK), lambda b, q: (b, 0, 0, 0)),
            pl.BlockSpec((n_h, d_v, rank), lambda b, q: (0, 0, 0)),
        ],
        out_specs=pl.BlockSpec((None, n_h * d_v, TQ), lambda b, q: (b, 0, q)),
        out_shape=jax.ShapeDtypeStruct((bsz, n_h * d_v, seq), jnp.float32),
        scratch_shapes=[pltpu.VMEM((seq, TQ), jnp.int32), pltpu.VMEM((seq, TQ), jnp.float32),
                        pltpu.VMEM((seq, TQ), jnp.float32), pltpu.VMEM((rank, TQ), jnp.float32)],
        compiler_params=pltpu.CompilerParams(dimension_semantics=("parallel", "arbitrary"),
                                             vmem_limit_bytes=VMEM_LIMIT_BYTES),
        name="dsa_attention",
    )(idx_qt, idx_k, idx_wt, q_catt, kv_lat, ckvt, w_uvt)


def _nsa_kernel(q_ref, kc_ref, vct_ref, ovt_ref, ks_ref, vst_ref, kw_ref, vwt_ref, g_ref, o_ref,
                blk_scr, bias_scr, s_scr, acc_scr, *, n_sel, scale):
    qb = pl.program_id(2)
    nk = ((qb + 1) * TQ + TK - 1) // TK
    tpos = qb * TQ + lax.broadcasted_iota(jnp.int32, (1, TQ), 1)
    row = lax.broadcasted_iota(jnp.int32, (TK, 1), 0)
    n_grp, hd = q_ref.shape[0], q_ref.shape[1]
    n_cmp_pad = kc_ref.shape[0]
    n_slc = ovt_ref.shape[0]
    gate = lambda g, j: jax.nn.sigmoid(g_ref[3 * g + j:3 * g + j + 1, :])

    c_last = lax.broadcasted_iota(jnp.int32, (n_cmp_pad, 1), 0) * CMP_STRIDE + (CMP_LEN - 1)
    cbias = jnp.where(c_last <= tpos, 0.0, NEG_INF)
    psum = jnp.zeros((n_cmp_pad, TQ), jnp.float32)
    for g in range(n_grp):
        s = jnp.dot(kc_ref[...], q_ref[g], preferred_element_type=jnp.float32) * scale + cbias
        m = jnp.max(s, axis=0, keepdims=True)
        m = jnp.where(m == NEG_INF, 0.0, m)
        e = jnp.exp(s - m)
        den = jnp.sum(e, axis=0, keepdims=True)
        p = e * (1.0 / jnp.where(den > 0, den, 1.0))
        psum = psum + p
        o_cmp = jnp.dot(vct_ref[...], p.astype(jnp.bfloat16), preferred_element_type=jnp.float32)
        o_ref[pl.ds(g * hd, hd), :] = gate(g, 0) * o_cmp

    hi = psum.astype(jnp.bfloat16)
    r1 = psum - hi.astype(jnp.float32)
    mid = r1.astype(jnp.bfloat16)
    lo = (r1 - mid.astype(jnp.float32)).astype(jnp.bfloat16)
    ovt = ovt_ref[...]
    imp = (jnp.dot(ovt, hi, preferred_element_type=jnp.float32)
           + jnp.dot(ovt, mid, preferred_element_type=jnp.float32)
           + jnp.dot(ovt, lo, preferred_element_type=jnp.float32))
    jidx = lax.broadcasted_iota(jnp.int32, (n_slc, 1), 0)
    t_blk = tpos // SLC_LEN
    forced = (jidx == 0) | (jidx == t_blk) | (jidx == t_blk - 1)
    imp = jnp.where(forced, jnp.inf, imp)
    imp = jnp.where(jidx * SLC_LEN <= tpos, imp, NEG_INF)
    rank = jnp.zeros((n_slc, TQ), jnp.int32)
    for jp in range(n_slc):
        r = imp[jp:jp + 1, :]
        rank = rank + jnp.where(r > imp, 1, jnp.where(r == imp, jnp.where(jp < jidx, 1, 0), 0))
    blk_scr[...] = jnp.where(rank < n_sel, 0.0, NEG_INF)

    blocks_per_chunk = TK // SLC_LEN

    def slc_bias_chunk(c, carry):
        k0 = pl.multiple_of(c * TK, TK)
        rows = [jnp.broadcast_to(blk_scr[pl.ds(c * blocks_per_chunk + i, 1), :], (SLC_LEN, TQ))
                for i in range(blocks_per_chunk)]
        bias_scr[pl.ds(k0, TK), :] = jnp.where(k0 + row <= tpos, jnp.concatenate(rows, axis=0), NEG_INF)
        return carry

    lax.fori_loop(0, nk, slc_bias_chunk, 0)

    def slc_bias(c):
        return bias_scr[pl.ds(pl.multiple_of(c * TK, TK), TK), :]

    def win_bias(c):
        spos = c * TK + row
        return jnp.where(spos <= tpos, jnp.where(spos > tpos - WIN_LEN, 0.0, NEG_INF), NEG_INF)

    w_lo = jnp.maximum(qb * TQ - WIN_LEN + 1, 0) // TK
    for g in range(n_grp):
        qt = q_ref[g]
        o_s = _attend(ks_ref, vst_ref, qt, slc_bias, 0, nk, s_scr, acc_scr, scale)
        o_ref[pl.ds(g * hd, hd), :] += gate(g, 1) * o_s
        o_w = _attend(kw_ref, vwt_ref, qt, win_bias, w_lo, nk, s_scr, acc_scr, scale)
        o_ref[pl.ds(g * hd, hd), :] += gate(g, 2) * o_w


def _nsa_attention_t(qt, kc, vct, ovt, ks, vst, kw, vwt, gates_t):
    bsz, kvh, grp, hd, seq = qt.shape
    ncp = kc.shape[2]
    n_slc = ovt.shape[0]
    kern = functools.partial(_nsa_kernel, n_sel=min(SLC_COUNT, n_slc), scale=hd ** -0.5)
    kv_spec = pl.BlockSpec((None, None, seq, hd), lambda b, h, q: (b, h, 0, 0))
    vt_spec = pl.BlockSpec((None, None, seq // TK, hd, TK), lambda b, h, q: (b, h, 0, 0, 0))
    return pl.pallas_call(
        kern,
        grid=(bsz, kvh, seq // TQ),
        in_specs=[
            pl.BlockSpec((None, None, grp, hd, TQ), lambda b, h, q: (b, h, 0, 0, q)),
            pl.BlockSpec((None, None, ncp, hd), lambda b, h, q: (b, h, 0, 0)),
            pl.BlockSpec((None, None, hd, ncp), lambda b, h, q: (b, h, 0, 0)),
            pl.BlockSpec((n_slc, ncp), lambda b, h, q: (0, 0)),
            kv_spec, vt_spec, kv_spec, vt_spec,
            pl.BlockSpec((None, None, 3 * grp, TQ), lambda b, h, q: (b, h, 0, q)),
        ],
        out_specs=pl.BlockSpec((None, grp * hd, TQ), lambda b, h, q: (b, h, q)),
        out_shape=jax.ShapeDtypeStruct((bsz, kvh * grp * hd, seq), jnp.float32),
        scratch_shapes=[pltpu.VMEM((n_slc, TQ), jnp.float32), pltpu.VMEM((seq, TQ), jnp.float32),
                        pltpu.VMEM((seq, TQ), jnp.float32), pltpu.VMEM((hd, TQ), jnp.float32)],
        compiler_params=pltpu.CompilerParams(dimension_semantics=("parallel", "parallel", "arbitrary"),
                                             vmem_limit_bytes=VMEM_LIMIT_BYTES),
        name="nsa_attention",
    )(qt, kc, vct, ovt, ks, vst, kw, vwt, gates_t)


HG_C = 64
HG_SB = 8
HG_CT = 1024


def _cumsum_rows(x):
    n = x.shape[0]
    row = lax.broadcasted_iota(jnp.int32, (n, 1), 0)
    s = 1
    while s < n:
        x = x + jnp.where(row >= s, pltpu.roll(x, s, axis=0), 0.0)
        s *= 2
    return x


def _bcast_rows(ref, first, period, n):
    return jnp.concatenate(
        [jnp.broadcast_to(ref[pl.ds(first + p * period, 1), :], (period, ref.shape[1])) for p in range(n // period)],
        axis=0)


def _hgrn_kernel(q_ref, f_ref, i_ref, g_ref, lb_ref, gn_ref, o_ref, st_scr, k_scr, b_scr, v_scr):
    @pl.when(pl.program_id(2) == 0)
    def _():
        st_scr[...] = jnp.zeros_like(st_scr)

    c_len, d = HG_C, q_ref.shape[1]
    bf = jnp.bfloat16
    lb = lb_ref[...]
    gain = gn_ref[...]
    row = lax.broadcasted_iota(jnp.int32, (c_len, 1), 0)
    col = lax.broadcasted_iota(jnp.int32, (1, c_len), 1)
    ones_b = jnp.ones((d, d), bf)
    nt = (((1,), (1,)), ((), ()))
    tn = (((0,), (0,)), ((), ()))

    def chunk(c, carry):
        sl = pl.ds(pl.multiple_of(c * c_len, c_len), c_len)
        qr, fr, v, gr = q_ref[sl, :], f_ref[sl, :], i_ref[sl, :], g_ref[sl, :]
        qv = qr * jax.nn.sigmoid(qr)
        fg = lb + (1.0 - lb) * jax.nn.sigmoid(fr)
        kk = 1.0 - fg
        b = _cumsum_rows(jnp.log(fg))
        k_scr[...] = kk
        b_scr[...] = b
        v_scr[...] = v
        v_b = v.astype(bf)

        att = jnp.zeros((c_len, c_len), jnp.float32)
        size = c_len // 2
        while size >= HG_SB:
            ref = _bcast_rows(b_scr, size - 1, 2 * size, c_len)
            is_q = ((row // size) % 2) == 1
            a_q = qv * jnp.exp(jnp.where(is_q, b - ref, NEG_INF))
            k_f = kk * jnp.exp(jnp.where(is_q, NEG_INF, ref - b))
            att_l = lax.dot_general(a_q.astype(bf), k_f.astype(bf), nt, preferred_element_type=jnp.float32)
            if 2 * size < c_len:
                att_l = jnp.where((row // (2 * size)) == (col // (2 * size)), att_l, 0.0)
            att = att + att_l
            size //= 2
        o = jnp.dot(att.astype(bf), v_b, preferred_element_type=jnp.float32)

        rmod = row % HG_SB
        for j in range(HG_SB):
            kb = _bcast_rows(k_scr, j, HG_SB, c_len)
            bb = _bcast_rows(b_scr, j, HG_SB, c_len)
            vb = _bcast_rows(v_scr, j, HG_SB, c_len)
            x = qv * kb * jnp.exp(jnp.where(rmod >= j, b - bb, NEG_INF))
            o = o + jnp.dot(x.astype(bf), ones_b, preferred_element_type=jnp.float32) * vb

        st = st_scr[...]
        o = o + lax.dot_general((qv * jnp.exp(b)).astype(bf), st.astype(bf), nt, preferred_element_type=jnp.float32)
        b_last = b[c_len - 1:c_len, :]
        k_l = kk * jnp.exp(b_last - b)
        st_scr[...] = st * jnp.exp(b_last) + lax.dot_general(v_b, k_l.astype(bf), tn,
                                                            preferred_element_type=jnp.float32)

        ms = jnp.mean(o * o, axis=-1, keepdims=True)
        o_ref[sl, :] = o * lax.rsqrt(ms + NORM_EPS) * gain * (gr * jax.nn.sigmoid(gr))
        return carry

    lax.fori_loop(0, q_ref.shape[0] // c_len, chunk, 0, unroll=4)


def _hgrn_recurrence(proj, lower_bound, out_norm):
    bsz, seq, w4 = proj.shape
    width = w4 // 4
    n_heads = width // C_HEAD_DIM
    ct = min(HG_CT, seq)
    assert seq % ct == 0 and ct % HG_C == 0
    part = lambda j: pl.BlockSpec((None, ct, C_HEAD_DIM), lambda b, h, t: (b, t, j * n_heads + h))
    vec = pl.BlockSpec((None, 1, C_HEAD_DIM), lambda b, h, t: (h, 0, 0))
    return pl.pallas_call(
        _hgrn_kernel,
        grid=(bsz, n_heads, seq // ct),
        in_specs=[part(0), part(1), part(2), part(3), vec, vec],
        out_specs=pl.BlockSpec((None, ct, C_HEAD_DIM), lambda b, h, t: (b, t, h)),
        out_shape=jax.ShapeDtypeStruct((bsz, seq, width), jnp.float32),
        scratch_shapes=[pltpu.VMEM((C_HEAD_DIM, C_HEAD_DIM), jnp.float32)]
        + [pltpu.VMEM((HG_C, C_HEAD_DIM), jnp.float32)] * 3,
        compiler_params=pltpu.CompilerParams(dimension_semantics=("parallel", "parallel", "arbitrary"),
                                             vmem_limit_bytes=VMEM_LIMIT_BYTES),
        name="hgrn2_recurrence",
    )(proj, proj, proj, proj, lower_bound.reshape(n_heads, 1, C_HEAD_DIM).astype(jnp.float32),
      out_norm.reshape(n_heads, 1, C_HEAD_DIM).astype(jnp.float32))


def rms_norm(x, g):
    xf = x.astype(jnp.float32)
    y = xf * lax.rsqrt(jnp.mean(xf * xf, axis=-1, keepdims=True) + NORM_EPS)
    return (y * g.astype(jnp.float32)).astype(x.dtype)


def rope(x, pos):
    d = x.shape[-1]
    inv = ROPE_THETA ** (-jnp.arange(0, d, 2, dtype=jnp.float32) / d)
    ang = pos.astype(jnp.float32)[:, None] * inv[None, :]
    cos = jnp.cos(ang)[:, None, :]
    sin = jnp.sin(ang)[:, None, :]
    xf = x.astype(jnp.float32)
    x1, x2 = xf[..., : d // 2], xf[..., d // 2:]
    return jnp.concatenate([x1 * cos - x2 * sin, x2 * cos + x1 * sin], axis=-1).astype(x.dtype)


def split_cols(a, widths):
    return jnp.split(a, np.cumsum(widths)[:-1].tolist(), axis=-1)


def dsa_nsa_mixer(u, w_in, kv_norm, w_uk, w_uv, cmp_pos, cmp_w1, cmp_w2, w_out):
    bsz, seq, _ = u.shape
    pos = jnp.arange(seq)
    (a_q_nope, a_q_rope, a_ckv, a_k_rope, i_q, i_k, i_w, b_q, b_kv, b_gate) = split_cols(_proj(u, w_in), AB_WIDTHS)
    q_nope = a_q_nope.reshape(bsz, seq, A_HEADS, A_NOPE)
    q_rope = rope(a_q_rope.reshape(bsz, seq, A_HEADS, A_ROPE), pos)
    q_abs = jnp.einsum('blhd,hdc->blhc', q_nope, w_uk)
    q_cat = jnp.concatenate([q_abs, q_rope], axis=-1)
    c_kv = rms_norm(a_ckv, kv_norm)
    k_rope = rope(a_k_rope[:, :, None, :], pos)[:, :, 0]
    kv_lat = jnp.concatenate([c_kv, k_rope], axis=-1)
    idx_q = rope(i_q.reshape(bsz, seq, IDX_HEADS, IDX_DIM), pos)
    idx_k = rope(i_k[:, :, None, :], pos)[:, :, 0]
    bf = jnp.bfloat16
    o_a_t = _dsa_attention_t(
        idx_q.transpose(0, 2, 3, 1).astype(bf), idx_k.astype(bf), i_w.transpose(0, 2, 1),
        q_cat.transpose(0, 2, 3, 1).astype(bf), kv_lat.astype(bf),
        c_kv.reshape(bsz, seq // TK, TK, A_KV_RANK).transpose(0, 1, 3, 2).astype(bf),
        w_uv.transpose(0, 2, 1).astype(bf))
    bq = rope(b_q.reshape(bsz, seq, B_HEADS, B_HEAD_DIM), pos)
    kv = b_kv.reshape(bsz, seq, 6, B_KV_HEADS, B_HEAD_DIM)
    o_b_t = nsa_attention_t(bq, rope(kv[:, :, 0], pos), kv[:, :, 1],
                            rope(kv[:, :, 2], pos), kv[:, :, 3],
                            rope(kv[:, :, 4], pos), kv[:, :, 5],
                            b_gate, cmp_pos, cmp_w1, cmp_w2)
    o_t = jnp.concatenate([o_a_t, o_b_t], axis=1)
    return _proj(o_t.transpose(0, 2, 1), w_out)


def nsa_attention_t(q, k_cmp, v_cmp, k_slc, v_slc, k_win, v_win, gates, cmp_pos, cmp_w1, cmp_w2):
    bsz, seq = q.shape[:2]
    kvh, grp, hd = B_KV_HEADS, B_GROUP, B_HEAD_DIM
    bf = jnp.bfloat16
    n_cmp = (seq - CMP_LEN) // CMP_STRIDE + 1
    n_cmp_pad = -(-n_cmp // 128) * 128
    tok = np.arange(n_cmp)[:, None] * CMP_STRIDE + np.arange(CMP_LEN)[None, :]

    def compress(a, j):
        blk = a[:, tok] + cmp_pos[j][:, None, :]
        blk = blk.transpose(0, 1, 3, 2, 4).reshape(bsz, n_cmp, kvh, CMP_LEN * hd)
        out = jax.nn.gelu(blk @ cmp_w1[j]) @ cmp_w2[j]
        return jnp.pad(out, ((0, 0), (0, n_cmp_pad - n_cmp), (0, 0), (0, 0)))

    kc = compress(k_cmp, 0).transpose(0, 2, 1, 3).astype(bf)
    vct = compress(v_cmp, 1).transpose(0, 2, 3, 1).astype(bf)
    n_slc = seq // SLC_LEN
    cmp_start = np.arange(n_cmp_pad) * CMP_STRIDE
    slc_start = np.arange(n_slc) * SLC_LEN
    ovt = ((cmp_start[None, :] < slc_start[:, None] + SLC_LEN)
           & (cmp_start[None, :] + CMP_LEN > slc_start[:, None]) & (np.arange(n_cmp_pad)[None, :] < n_cmp))
    ovt = jnp.asarray(ovt, bf)
    qt = q.reshape(bsz, seq, kvh, grp, hd).transpose(0, 2, 3, 4, 1).astype(bf)
    k_t = lambda a: a.transpose(0, 2, 1, 3).astype(bf)
    v_t = lambda a: a.reshape(bsz, seq // TK, TK, kvh, hd).transpose(0, 3, 1, 4, 2).astype(bf)
    gates_t = gates.reshape(bsz, seq, kvh, 3 * grp).transpose(0, 2, 3, 1)
    return _nsa_attention_t(qt, kc, vct, ovt, k_t(k_slc), v_t(v_slc), k_t(k_win), v_t(v_win), gates_t)


def hgrn2_mixer(u, w_in, lower_bound, out_norm, w_out):
    return _proj(_hgrn_recurrence(_proj(u, w_in), lower_bound, out_norm), w_out)


def routed_experts(xf, e_id, e_w, w_gate_up, w_down):
    n_tok = xf.shape[0]
    n_as = e_id.shape[0]
    tok = jnp.arange(n_as, dtype=jnp.int32) // GROUP_TOPK
    onehot = (e_id[:, None] == jnp.arange(N_EXPERTS, dtype=e_id.dtype)[None, :]).astype(jnp.int32)
    csum = jnp.cumsum(onehot, axis=0)
    counts = csum[-1]
    rank = jnp.take_along_axis(csum, e_id[:, None], axis=1)[:, 0] - 1
    padded = (counts + MOE_BLOCK - 1) // MOE_BLOCK * MOE_BLOCK
    pad_end = jnp.cumsum(padded)
    pad_start = pad_end - padded
    dest = pad_start[e_id] + rank
    n_blk = -(-(n_as + N_EXPERTS * (MOE_BLOCK - 1)) // MOE_BLOCK)
    rows = n_blk * MOE_BLOCK
    row_tok = jnp.zeros((rows,), jnp.int32).at[dest].set(tok)
    row_w = jnp.zeros((rows,), jnp.float32).at[dest].set(e_w)
    blk_e = jnp.minimum(jnp.searchsorted(pad_end, jnp.arange(n_blk) * MOE_BLOCK, side='right'),
                        N_EXPERTS - 1).astype(jnp.int32)
    n_used = (pad_end[-1:] // MOE_BLOCK).astype(jnp.int32)
    xb = xf.astype(jnp.bfloat16)[row_tok]
    yb = _moe_blocks(blk_e, n_used, xb, w_gate_up, w_down, row_w[:, None])
    return yb[dest].reshape(n_tok, GROUP_TOPK, xf.shape[-1]).sum(axis=1)


def hier_moe(u, w_group, b_group, w_expert, b_expert, w_gate_up, w_down):
    bsz, seq, d = u.shape
    n_tok = bsz * seq
    xf = u.reshape(n_tok, d)
    hp = lax.Precision.HIGHEST
    g_logits = jnp.dot(xf, w_group, precision=hp).astype(jnp.float32) + b_group.astype(jnp.float32)
    g_prob = jax.nn.softmax(g_logits, axis=-1)
    grp = jnp.argmax(g_logits, axis=-1)
    g_w = jnp.take_along_axis(g_prob, grp[:, None], axis=-1)
    e_logits = (jnp.dot(xf, w_expert, precision=hp).astype(jnp.float32) + b_expert.astype(jnp.float32))
    e_logits = e_logits.reshape(n_tok, N_GROUPS, EXPERTS_PER_GROUP)
    e_logits = jnp.take_along_axis(e_logits, grp[:, None, None], axis=1)[:, 0]
    top_v, top_i = lax.top_k(e_logits, GROUP_TOPK)
    e_w = jax.nn.softmax(top_v, axis=-1) * g_w
    e_id = (grp[:, None] * EXPERTS_PER_GROUP + top_i).astype(jnp.int32)
    y = routed_experts(xf, e_id.reshape(-1), e_w.reshape(-1).astype(jnp.float32), w_gate_up, w_down)
    return y.reshape(bsz, seq, d)


def kernel(x, mix_norm, ffn_norm, final_norm, ab_w_in, dsa_kv_norm, dsa_w_uk, dsa_w_uv, nsa_cmp_pos, nsa_cmp_w1, nsa_cmp_w2, ab_w_out, hgrn_w_in, hgrn_lb_logits, hgrn_out_norm, hgrn_w_out, moe_w_group, moe_b_group, moe_w_expert, moe_b_expert, moe_w_gate_up, moe_w_down):
    lb_p = jax.nn.softmax(hgrn_lb_logits.astype(jnp.float32), axis=0)
    lower_bounds = jnp.cumsum(lb_p, axis=0) - lb_p[0]
    h = x
    for layer in range(DEPTH):
        j = layer // 2
        u = rms_norm(h, mix_norm[layer])
        if layer % 2 == 0:
            h = h + dsa_nsa_mixer(u, ab_w_in[j], dsa_kv_norm[j], dsa_w_uk[j], dsa_w_uv[j],
                                  nsa_cmp_pos[j], nsa_cmp_w1[j], nsa_cmp_w2[j], ab_w_out[j])
        else:
            h = h + hgrn2_mixer(u, hgrn_w_in[j], lower_bounds[layer], hgrn_out_norm[j], hgrn_w_out[j])
        u = rms_norm(h, ffn_norm[layer])
        h = h + hier_moe(u, moe_w_group[layer], moe_b_group[layer], moe_w_expert[layer],
                         moe_b_expert[layer], moe_w_gate_up[layer], moe_w_down[layer])
    return rms_norm(h, final_norm)
```

```python
import functools

import numpy as np
import jax
import jax.numpy as jnp
from jax import lax
from jax.experimental import pallas as pl
from jax.experimental.pallas import tpu as pltpu

D_MODEL = 4096
BATCH = 4
SEQ = 2048
DEPTH = 2

ROPE_THETA = 10000.0
NORM_EPS = 1e-6
Q_BLOCK = 64

A_HEADS = 16
A_NOPE = 128
A_ROPE = 64
A_KV_RANK = 512
A_V_DIM = 128
IDX_HEADS = 32
IDX_DIM = 64
IDX_TOPK = 256

B_HEADS = 16
B_KV_HEADS = 4
B_GROUP = B_HEADS // B_KV_HEADS
B_HEAD_DIM = 128
CMP_LEN = 32
CMP_STRIDE = 16
CMP_HIDDEN = 256
SLC_LEN = 64
SLC_COUNT = 16
WIN_LEN = 512

AB_WIDTHS = (
    A_HEADS * A_NOPE,
    A_HEADS * A_ROPE,
    A_KV_RANK,
    A_ROPE,
    IDX_HEADS * IDX_DIM,
    IDX_DIM,
    IDX_HEADS,
    B_HEADS * B_HEAD_DIM,
    6 * B_KV_HEADS * B_HEAD_DIM,
    3 * B_HEADS,
)
AB_IN = sum(AB_WIDTHS)
AB_OUT = A_HEADS * A_V_DIM + B_HEADS * B_HEAD_DIM

C_HEAD_DIM = 128
C_HEADS = D_MODEL // C_HEAD_DIM
C_WIDTH = C_HEADS * C_HEAD_DIM
C_CHUNK = 64

N_GROUPS = 8
EXPERTS_PER_GROUP = 8
N_EXPERTS = N_GROUPS * EXPERTS_PER_GROUP
GROUP_TOPK = 2
EXPERT_FF = 3 * D_MODEL // 32
MOE_BLOCK = 128

VMEM_LIMIT_BYTES = 48 * 1024 * 1024


MM_TM = 1024
MM_TN = 512


def _mm_kernel(x_ref, w_ref, o_ref):
    o_ref[...] = jnp.dot(x_ref[...], w_ref[...].astype(jnp.bfloat16), preferred_element_type=jnp.float32)


def _matmul(x, w):
    m, kd = x.shape
    n = w.shape[1]
    assert m % MM_TM == 0
    return pl.pallas_call(
        _mm_kernel,
        grid=(pl.cdiv(n, MM_TN), m // MM_TM),
        in_specs=[pl.BlockSpec((MM_TM, kd), lambda j, i: (i, 0)),
                  pl.BlockSpec((kd, MM_TN), lambda j, i: (0, j))],
        out_specs=pl.BlockSpec((MM_TM, MM_TN), lambda j, i: (i, j)),
        out_shape=jax.ShapeDtypeStruct((m, n), jnp.float32),
        compiler_params=pltpu.CompilerParams(
            dimension_semantics=("parallel", "parallel"),
            vmem_limit_bytes=VMEM_LIMIT_BYTES),
        name="matmul",
    )(x, w)


def _proj(u, w):
    lead = u.shape[:-1]
    return _matmul(u.reshape(-1, u.shape[-1]).astype(jnp.bfloat16), w).reshape(*lead, w.shape[1])


MOE_KC = 1024
MOE_VMEM_LIMIT_BYTES = 58 * 1024 * 1024


def _moe_kernel(blk_e_ref, n_used_ref, x_ref, wgu_ref, wd_ref, rw_ref, o_ref):
    del blk_e_ref
    i = pl.program_id(0)
    d_model, ff2 = wgu_ref.shape
    ff = ff2 // 2

    @pl.when(i < n_used_ref[0])
    def _():
        gu = jnp.zeros((x_ref.shape[0], ff2), jnp.float32)
        for k0 in range(0, d_model, MOE_KC):
            gu = gu + jnp.dot(x_ref[:, k0:k0 + MOE_KC], wgu_ref[k0:k0 + MOE_KC, :].astype(jnp.bfloat16),
                              preferred_element_type=jnp.float32)
        hid = (jax.nn.silu(gu[:, :ff]) * gu[:, ff:]).astype(jnp.bfloat16)
        rw = rw_ref[...]
        for n0 in range(0, d_model, MOE_KC):
            o_ref[:, n0:n0 + MOE_KC] = rw * jnp.dot(hid, wd_ref[:, n0:n0 + MOE_KC].astype(jnp.bfloat16),
                                                    preferred_element_type=jnp.float32)

    @pl.when(i >= n_used_ref[0])
    def _():
        o_ref[...] = jnp.zeros_like(o_ref)


def _moe_blocks(blk_e, n_used, xb, w_gate_up, w_down, row_w, layer):
    rows, d = xb.shape
    n_blk = rows // MOE_BLOCK
    ff2 = w_gate_up.shape[-1]
    grid_spec = pltpu.PrefetchScalarGridSpec(
        num_scalar_prefetch=2,
        grid=(n_blk,),
        in_specs=[pl.BlockSpec((MOE_BLOCK, d), lambda i, be, nu: (i, 0)),
                  pl.BlockSpec((None, None, d, ff2), lambda i, be, nu: (layer, be[i], 0, 0)),
                  pl.BlockSpec((None, None, ff2 // 2, d), lambda i, be, nu: (layer, be[i], 0, 0)),
                  pl.BlockSpec((MOE_BLOCK, 1), lambda i, be, nu: (i, 0))],
        out_specs=pl.BlockSpec((MOE_BLOCK, d), lambda i, be, nu: (i, 0)),
    )
    return pl.pallas_call(
        _moe_kernel,
        grid_spec=grid_spec,
        out_shape=jax.ShapeDtypeStruct((rows, d), jnp.float32),
        compiler_params=pltpu.CompilerParams(dimension_semantics=("arbitrary",),
                                             vmem_limit_bytes=MOE_VMEM_LIMIT_BYTES),
        name="moe_experts",
    )(blk_e, n_used, xb, w_gate_up, w_down, row_w)


NORM_TM = 256
ROUTER_PAD = 128


def _add_norm_kernel(*refs, n_add, has_router, write_h):
    it = iter(refs)
    h_ref = next(it)
    add_refs = [next(it) for _ in range(n_add)]
    g_ref = next(it)
    wr_ref = next(it) if has_router else None
    h_out = next(it) if write_h else None
    u_ref = next(it)
    lg_ref = next(it) if has_router else None
    hv = h_ref[...]
    for a_ref in add_refs:
        hv = hv + a_ref[...]
    if write_h:
        h_out[...] = hv
    y = hv * lax.rsqrt(jnp.mean(hv * hv, axis=-1, keepdims=True) + NORM_EPS) * g_ref[...]
    u_ref[...] = y.astype(u_ref.dtype)
    if has_router:
        lg_ref[...] = jnp.dot(y, wr_ref[...], precision=lax.Precision.HIGHEST, preferred_element_type=jnp.float32)


def _add_norm(h, adds, gain, out_dtype, w_router=None, write_h=True):
    t, d = h.shape
    row = pl.BlockSpec((NORM_TM, d), lambda i: (i, 0))
    operands = [h, *adds, gain.reshape(1, d).astype(jnp.float32)]
    in_specs = [row] * (1 + len(adds)) + [pl.BlockSpec((1, d), lambda i: (0, 0))]
    out_shape, out_specs = [], []
    if w_router is not None:
        operands.append(w_router)
        in_specs.append(pl.BlockSpec(w_router.shape, lambda i: (0, 0)))
    if write_h:
        out_shape.append(jax.ShapeDtypeStruct((t, d), jnp.float32))
        out_specs.append(row)
    out_shape.append(jax.ShapeDtypeStruct((t, d), out_dtype))
    out_specs.append(row)
    if w_router is not None:
        out_shape.append(jax.ShapeDtypeStruct((t, w_router.shape[1]), jnp.float32))
        out_specs.append(pl.BlockSpec((NORM_TM, w_router.shape[1]), lambda i: (i, 0)))
    return pl.pallas_call(
        functools.partial(_add_norm_kernel, n_add=len(adds), has_router=w_router is not None, write_h=write_h),
        grid=(t // NORM_TM,),
        in_specs=in_specs,
        out_specs=out_specs,
        out_shape=out_shape,
        compiler_params=pltpu.CompilerParams(dimension_semantics=("parallel",),
                                             vmem_limit_bytes=VMEM_LIMIT_BYTES),
        name="add_norm",
    )(*operands)


TQ = 256
TK = 512
IK = 256
SUBLANES = 8
NEG_INF = float("-inf")
INT_MIN = -2 ** 31


def _fold8(x, op):
    r, c = x.shape
    return op(x.reshape(r // SUBLANES, SUBLANES, c), axis=0)


def _attend(k_ref, vt_ref, qt, bias_fn, c_lo, c_hi, s_scr, acc_scr, scale):
    tq = qt.shape[1]

    def pass1(c, m8):
        k0 = pl.multiple_of(c * TK, TK)
        s = jnp.dot(k_ref[pl.ds(k0, TK), :], qt, preferred_element_type=jnp.float32) * scale + bias_fn(c)
        s_scr[pl.ds(k0, TK), :] = s
        return jnp.maximum(m8, _fold8(s, jnp.max))

    m8 = lax.fori_loop(c_lo, c_hi, pass1, jnp.full((SUBLANES, tq), NEG_INF, jnp.float32))
    m = jnp.max(m8, axis=0, keepdims=True)
    m = jnp.where(m == NEG_INF, 0.0, m)
    acc_scr[...] = jnp.zeros_like(acc_scr)

    def pass2(c, d8):
        k0 = pl.multiple_of(c * TK, TK)
        p = jnp.exp(s_scr[pl.ds(k0, TK), :] - m)
        acc_scr[...] += jnp.dot(vt_ref[c], p.astype(jnp.bfloat16), preferred_element_type=jnp.float32)
        return d8 + _fold8(p, jnp.sum)

    d8 = lax.fori_loop(c_lo, c_hi, pass2, jnp.zeros((SUBLANES, tq), jnp.float32))
    den = jnp.sum(d8, axis=0, keepdims=True)
    return acc_scr[...] * (1.0 / jnp.where(den > 0, den, 1.0))


def _dsa_kernel(iq_ref, ik_ref, iw_ref, q_ref, kv_ref, ckvt_ref, wuvt_ref, o_ref,
                key_scr, bias_scr, s_scr, acc_scr, *, topk, idx_scale, scale):
    qb = pl.program_id(1)
    nk = ((qb + 1) * TQ + TK - 1) // TK
    n_ik = nk * (TK // IK)
    tpos = qb * TQ + lax.broadcasted_iota(jnp.int32, (1, TQ), 1)
    row = lax.broadcasted_iota(jnp.int32, (IK, 1), 0)
    w_rows = iw_ref[...] * idx_scale
    n_idx_heads = iq_ref.shape[0]

    def index_chunk(c, carry):
        k0 = pl.multiple_of(c * IK, IK)
        kblk = ik_ref[pl.ds(k0, IK), :]
        acc = jnp.zeros((IK, TQ), jnp.float32)
        for h in range(n_idx_heads):
            lg = jnp.dot(kblk, iq_ref[h], preferred_element_type=jnp.float32)
            acc = acc + w_rows[h:h + 1, :] * jnp.maximum(lg, 0.0)
        acc = acc + 0.0
        bits = lax.bitcast_convert_type(acc, jnp.int32)
        key = bits ^ ((bits >> 31) & 0x7FFFFFFF)
        key_scr[pl.ds(k0, IK), :] = jnp.where(k0 + row <= tpos, key, INT_MIN)
        return carry

    lax.fori_loop(0, n_ik, index_chunk, 0)

    def count(pred_fn):
        def body(c, cnt8):
            k0 = pl.multiple_of(c * IK, IK)
            hit = jnp.where(pred_fn(key_scr[pl.ds(k0, IK), :]), 1, 0).astype(jnp.int32)
            return cnt8 + _fold8(hit, jnp.sum)
        cnt8 = lax.fori_loop(0, n_ik, body, jnp.zeros((SUBLANES, TQ), jnp.int32))
        return jnp.sum(cnt8, axis=0, keepdims=True)

    thr = jnp.where(count(lambda kk: kk >= 0) >= topk, 0, INT_MIN).astype(jnp.int32) + jnp.zeros((1, TQ), jnp.int32)

    def bit_step(i, thr):
        cand = thr + jnp.left_shift(jnp.int32(1), 30 - i)
        return jnp.where(count(lambda kk: kk >= cand) >= topk, cand, thr)

    thr = lax.fori_loop(0, 31, bit_step, thr)
    thr = jnp.maximum(thr, INT_MIN + 1)
    need = (topk - count(lambda kk: kk > thr)).astype(jnp.float32)

    tri = jnp.where(lax.broadcasted_iota(jnp.int32, (IK, IK), 1) <= lax.broadcasted_iota(jnp.int32, (IK, IK), 0),
                    1.0, 0.0).astype(jnp.bfloat16)

    def bias_chunk(c, run):
        k0 = pl.multiple_of(c * IK, IK)
        kk = key_scr[pl.ds(k0, IK), :]
        eq = jnp.where(kk == thr, 1.0, 0.0)
        pref = jnp.dot(tri, eq.astype(jnp.bfloat16), preferred_element_type=jnp.float32) + run
        tie_ok = jnp.where(pref <= need, 0.0, NEG_INF)
        bias_scr[pl.ds(k0, IK), :] = jnp.where(kk > thr, 0.0, jnp.where(kk == thr, tie_ok, NEG_INF))
        return pref[IK - 1:IK, :]

    lax.fori_loop(0, n_ik, bias_chunk, jnp.zeros((1, TQ), jnp.float32))

    v_dim = wuvt_ref.shape[1]

    def head(h, carry):
        o_lat = _attend(kv_ref, ckvt_ref, q_ref[h], lambda c: bias_scr[pl.ds(pl.multiple_of(c * TK, TK), TK), :],
                        0, nk, s_scr, acc_scr, scale)
        o_ref[pl.ds(pl.multiple_of(h * v_dim, v_dim), v_dim), :] = jnp.dot(
            wuvt_ref[h], o_lat.astype(jnp.bfloat16), preferred_element_type=jnp.float32)
        return carry

    lax.fori_loop(0, q_ref.shape[0], head, 0)


def _dsa_attention_t(idx_qt, idx_k, idx_wt, q_catt, kv_lat, ckvt, w_uvt):
    bsz, n_ih, d_i, seq = idx_qt.shape
    n_h, d_qk = q_catt.shape[1:3]
    d_v, rank = w_uvt.shape[1:]
    topk = min(IDX_TOPK, seq // 4)
    kern = functools.partial(_dsa_kernel, topk=topk, idx_scale=(IDX_DIM ** -0.5) * (IDX_HEADS ** -0.5),
                             scale=(A_NOPE + A_ROPE) ** -0.5)
    return pl.pallas_call(
        kern,
        grid=(bsz, seq // TQ),
        in_specs=[
            pl.BlockSpec((None, n_ih, d_i, TQ), lambda b, q: (b, 0, 0, q)),
            pl.BlockSpec((None, seq, d_i), lambda b, q: (b, 0, 0)),
            pl.BlockSpec((None, n_ih, TQ), lambda b, q: (b, 0, q)),
            pl.BlockSpec((None, n_h, d_qk, TQ), lambda b, q: (b, 0, 0, q)),
            pl.BlockSpec((None, seq, d_qk), lambda b, q: (b, 0, 0)),
            pl.BlockSpec((None, seq // TK, rank, TK), lambda b, q: (b, 0, 0, 0)),
            pl.BlockSpec((n_h, d_v, rank), lambda b, q: (0, 0, 0)),
        ],
        out_specs=pl.BlockSpec((None, n_h * d_v, TQ), lambda b, q: (b, 0, q)),
        out_shape=jax.ShapeDtypeStruct((bsz, n_h * d_v, seq), jnp.float32),
        scratch_shapes=[pltpu.VMEM((seq, TQ), jnp.int32), pltpu.VMEM((seq, TQ), jnp.float32),
                        pltpu.VMEM((seq, TQ), jnp.float32), pltpu.VMEM((rank, TQ), jnp.float32)],
        compiler_params=pltpu.CompilerParams(dimension_semantics=("parallel", "arbitrary"),
                                             vmem_limit_bytes=VMEM_LIMIT_BYTES),
        name="dsa_attention",
    )(idx_qt, idx_k, idx_wt, q_catt, kv_lat, ckvt, w_uvt)


def _nsa_kernel(q_ref, kc_ref, vct_ref, ovt_ref, ks_ref, vst_ref, kw_ref, vwt_ref, g_ref, o_ref,
                blk_scr, bias_scr, s_scr, acc_scr, *, n_sel, scale):
    qb = pl.program_id(2)
    nk = ((qb + 1) * TQ + TK - 1) // TK
    tpos = qb * TQ + lax.broadcasted_iota(jnp.int32, (1, TQ), 1)
    row = lax.broadcasted_iota(jnp.int32, (TK, 1), 0)
    n_grp, hd = q_ref.shape[0], q_ref.shape[1]
    n_cmp_pad = kc_ref.shape[0]
    n_slc = ovt_ref.shape[0]
    gate = lambda g, j: jax.nn.sigmoid(g_ref[3 * g + j:3 * g + j + 1, :])

    c_last = lax.broadcasted_iota(jnp.int32, (n_cmp_pad, 1), 0) * CMP_STRIDE + (CMP_LEN - 1)
    cbias = jnp.where(c_last <= tpos, 0.0, NEG_INF)
    psum = jnp.zeros((n_cmp_pad, TQ), jnp.float32)
    for g in range(n_grp):
        s = jnp.dot(kc_ref[...], q_ref[g], preferred_element_type=jnp.float32) * scale + cbias
        m = jnp.max(s, axis=0, keepdims=True)
        m = jnp.where(m == NEG_INF, 0.0, m)
        e = jnp.exp(s - m)
        den = jnp.sum(e, axis=0, keepdims=True)
        p = e * (1.0 / jnp.where(den > 0, den, 1.0))
        psum = psum + p
        o_cmp = jnp.dot(vct_ref[...], p.astype(jnp.bfloat16), preferred_element_type=jnp.float32)
        o_ref[pl.ds(g * hd, hd), :] = gate(g, 0) * o_cmp

    hi = psum.astype(jnp.bfloat16)
    r1 = psum - hi.astype(jnp.float32)
    mid = r1.astype(jnp.bfloat16)
    lo = (r1 - mid.astype(jnp.float32)).astype(jnp.bfloat16)
    ovt = ovt_ref[...]
    imp = (jnp.dot(ovt, hi, preferred_element_type=jnp.float32)
           + jnp.dot(ovt, mid, preferred_element_type=jnp.float32)
           + jnp.dot(ovt, lo, preferred_element_type=jnp.float32))
    jidx = lax.broadcasted_iota(jnp.int32, (n_slc, 1), 0)
    t_blk = tpos // SLC_LEN
    forced = (jidx == 0) | (jidx == t_blk) | (jidx == t_blk - 1)
    imp = jnp.where(forced, jnp.inf, imp)
    imp = jnp.where(jidx * SLC_LEN <= tpos, imp, NEG_INF)
    rank = jnp.zeros((n_slc, TQ), jnp.int32)
    for jp in range(n_slc):
        r = imp[jp:jp + 1, :]
        rank = rank + jnp.where(r > imp, 1, jnp.where(r == imp, jnp.where(jp < jidx, 1, 0), 0))
    blk_scr[...] = jnp.where(rank < n_sel, 0.0, NEG_INF)

    blocks_per_chunk = TK // SLC_LEN

    def slc_bias_chunk(c, carry):
        k0 = pl.multiple_of(c * TK, TK)
        rows = [jnp.broadcast_to(blk_scr[pl.ds(c * blocks_per_chunk + i, 1), :], (SLC_LEN, TQ))
                for i in range(blocks_per_chunk)]
        bias_scr[pl.ds(k0, TK), :] = jnp.where(k0 + row <= tpos, jnp.concatenate(rows, axis=0), NEG_INF)
        return carry

    lax.fori_loop(0, nk, slc_bias_chunk, 0)

    def slc_bias(c):
        return bias_scr[pl.ds(pl.multiple_of(c * TK, TK), TK), :]

    def win_bias(c):
        spos = c * TK + row
        return jnp.where(spos <= tpos, jnp.where(spos > tpos - WIN_LEN, 0.0, NEG_INF), NEG_INF)

    w_lo = jnp.maximum(qb * TQ - WIN_LEN + 1, 0) // TK
    for g in range(n_grp):
        qt = q_ref[g]
        o_s = _attend(ks_ref, vst_ref, qt, slc_bias, 0, nk, s_scr, acc_scr, scale)
        o_ref[pl.ds(g * hd, hd), :] += gate(g, 1) * o_s
        o_w = _attend(kw_ref, vwt_ref, qt, win_bias, w_lo, nk, s_scr, acc_scr, scale)
        o_ref[pl.ds(g * hd, hd), :] += gate(g, 2) * o_w


def _nsa_attention_t(qt, kc, vct, ovt, ks, vst, kw, vwt, gates_t):
    bsz, kvh, grp, hd, seq = qt.shape
    ncp = kc.shape[2]
    n_slc = ovt.shape[0]
    kern = functools.partial(_nsa_kernel, n_sel=min(SLC_COUNT, n_slc), scale=hd ** -0.5)
    kv_spec = pl.BlockSpec((None, None, seq, hd), lambda b, h, q: (b, h, 0, 0))
    vt_spec = pl.BlockSpec((None, None, seq // TK, hd, TK), lambda b, h, q: (b, h, 0, 0, 0))
    return pl.pallas_call(
        kern,
        grid=(bsz, kvh, seq // TQ),
        in_specs=[
            pl.BlockSpec((None, None, grp, hd, TQ), lambda b, h, q: (b, h, 0, 0, q)),
            pl.BlockSpec((None, None, ncp, hd), lambda b, h, q: (b, h, 0, 0)),
            pl.BlockSpec((None, None, hd, ncp), lambda b, h, q: (b, h, 0, 0)),
            pl.BlockSpec((n_slc, ncp), lambda b, h, q: (0, 0)),
            kv_spec, vt_spec, kv_spec, vt_spec,
            pl.BlockSpec((None, None, 3 * grp, TQ), lambda b, h, q: (b, h, 0, q)),
        ],
        out_specs=pl.BlockSpec((None, grp * hd, TQ), lambda b, h, q: (b, h, q)),
        out_shape=jax.ShapeDtypeStruct((bsz, kvh * grp * hd, seq), jnp.float32),
        scratch_shapes=[pltpu.VMEM((n_slc, TQ), jnp.float32), pltpu.VMEM((seq, TQ), jnp.float32),
                        pltpu.VMEM((seq, TQ), jnp.float32), pltpu.VMEM((hd, TQ), jnp.float32)],
        compiler_params=pltpu.CompilerParams(dimension_semantics=("parallel", "parallel", "arbitrary"),
                                             vmem_limit_bytes=VMEM_LIMIT_BYTES),
        name="nsa_attention",
    )(qt, kc, vct, ovt, ks, vst, kw, vwt, gates_t)


HG_C = 64
HG_SB = 8
HG_CT = 1024


def _cumsum_rows(x):
    n = x.shape[0]
    row = lax.broadcasted_iota(jnp.int32, (n, 1), 0)
    s = 1
    while s < n:
        x = x + jnp.where(row >= s, pltpu.roll(x, s, axis=0), 0.0)
        s *= 2
    return x


def _bcast_rows(ref, first, period, n):
    return jnp.concatenate(
        [jnp.broadcast_to(ref[pl.ds(first + p * period, 1), :], (period, ref.shape[1])) for p in range(n // period)],
        axis=0)


def _hgrn_kernel(q_ref, f_ref, i_ref, g_ref, lb_ref, gn_ref, o_ref, st_scr, k_scr, b_scr, v_scr):
    @pl.when(pl.program_id(2) == 0)
    def _():
        st_scr[...] = jnp.zeros_like(st_scr)

    c_len, d = HG_C, q_ref.shape[1]
    bf = jnp.bfloat16
    lb = lb_ref[...]
    gain = gn_ref[...]
    row = lax.broadcasted_iota(jnp.int32, (c_len, 1), 0)
    col = lax.broadcasted_iota(jnp.int32, (1, c_len), 1)
    ones_b = jnp.ones((d, d), bf)
    nt = (((1,), (1,)), ((), ()))
    tn = (((0,), (0,)), ((), ()))

    def chunk(c, carry):
        sl = pl.ds(pl.multiple_of(c * c_len, c_len), c_len)
        qr, fr, v, gr = q_ref[sl, :], f_ref[sl, :], i_ref[sl, :], g_ref[sl, :]
        qv = qr * jax.nn.sigmoid(qr)
        fg = lb + (1.0 - lb) * jax.nn.sigmoid(fr)
        kk = 1.0 - fg
        b = _cumsum_rows(jnp.log(fg))
        k_scr[...] = kk
        b_scr[...] = b
        v_scr[...] = v
        v_b = v.astype(bf)

        att = jnp.zeros((c_len, c_len), jnp.float32)
        size = c_len // 2
        while size >= HG_SB:
            ref = _bcast_rows(b_scr, size - 1, 2 * size, c_len)
            is_q = ((row // size) % 2) == 1
            a_q = qv * jnp.exp(jnp.where(is_q, b - ref, NEG_INF))
            k_f = kk * jnp.exp(jnp.where(is_q, NEG_INF, ref - b))
            att_l = lax.dot_general(a_q.astype(bf), k_f.astype(bf), nt, preferred_element_type=jnp.float32)
            if 2 * size < c_len:
                att_l = jnp.where((row // (2 * size)) == (col // (2 * size)), att_l, 0.0)
            att = att + att_l
            size //= 2
        o = jnp.dot(att.astype(bf), v_b, preferred_element_type=jnp.float32)

        rmod = row % HG_SB
        for j in range(HG_SB):
            kb = _bcast_rows(k_scr, j, HG_SB, c_len)
            bb = _bcast_rows(b_scr, j, HG_SB, c_len)
            vb = _bcast_rows(v_scr, j, HG_SB, c_len)
            x = qv * kb * jnp.exp(jnp.where(rmod >= j, b - bb, NEG_INF))
            o = o + jnp.dot(x.astype(bf), ones_b, preferred_element_type=jnp.float32) * vb

        st = st_scr[...]
        o = o + lax.dot_general((qv * jnp.exp(b)).astype(bf), st.astype(bf), nt, preferred_element_type=jnp.float32)
        b_last = b[c_len - 1:c_len, :]
        k_l = kk * jnp.exp(b_last - b)
        st_scr[...] = st * jnp.exp(b_last) + lax.dot_general(v_b, k_l.astype(bf), tn,
                                                            preferred_element_type=jnp.float32)

        ms = jnp.mean(o * o, axis=-1, keepdims=True)
        o_ref[sl, :] = o * lax.rsqrt(ms + NORM_EPS) * gain * (gr * jax.nn.sigmoid(gr))
        return carry

    lax.fori_loop(0, q_ref.shape[0] // c_len, chunk, 0, unroll=4)


def _hgrn_recurrence(proj, lower_bound, out_norm):
    bsz, seq, w4 = proj.shape
    width = w4 // 4
    n_heads = width // C_HEAD_DIM
    ct = min(HG_CT, seq)
    assert seq % ct == 0 and ct % HG_C == 0
    part = lambda j: pl.BlockSpec((None, ct, C_HEAD_DIM), lambda b, h, t: (b, t, j * n_heads + h))
    vec = pl.BlockSpec((None, 1, C_HEAD_DIM), lambda b, h, t: (h, 0, 0))
    return pl.pallas_call(
        _hgrn_kernel,
        grid=(bsz, n_heads, seq // ct),
        in_specs=[part(0), part(1), part(2), part(3), vec, vec],
        out_specs=pl.BlockSpec((None, ct, C_HEAD_DIM), lambda b, h, t: (b, t, h)),
        out_shape=jax.ShapeDtypeStruct((bsz, seq, width), jnp.float32),
        scratch_shapes=[pltpu.VMEM((C_HEAD_DIM, C_HEAD_DIM), jnp.float32)]
        + [pltpu.VMEM((HG_C, C_HEAD_DIM), jnp.float32)] * 3,
        compiler_params=pltpu.CompilerParams(dimension_semantics=("parallel", "parallel", "arbitrary"),
                                             vmem_limit_bytes=VMEM_LIMIT_BYTES),
        name="hgrn2_recurrence",
    )(proj, proj, proj, proj, lower_bound.reshape(n_heads, 1, C_HEAD_DIM).astype(jnp.float32),
      out_norm.reshape(n_heads, 1, C_HEAD_DIM).astype(jnp.float32))


def rms_norm(x, g):
    xf = x.astype(jnp.float32)
    y = xf * lax.rsqrt(jnp.mean(xf * xf, axis=-1, keepdims=True) + NORM_EPS)
    return (y * g.astype(jnp.float32)).astype(x.dtype)


def rope(x, pos):
    d = x.shape[-1]
    inv = ROPE_THETA ** (-jnp.arange(0, d, 2, dtype=jnp.float32) / d)
    ang = pos.astype(jnp.float32)[:, None] * inv[None, :]
    cos = jnp.cos(ang)[:, None, :]
    sin = jnp.sin(ang)[:, None, :]
    xf = x.astype(jnp.float32)
    x1, x2 = xf[..., : d // 2], xf[..., d // 2:]
    return jnp.concatenate([x1 * cos - x2 * sin, x2 * cos + x1 * sin], axis=-1).astype(x.dtype)


def split_cols(a, widths):
    return jnp.split(a, np.cumsum(widths)[:-1].tolist(), axis=-1)


def dsa_nsa_mixer(u, w_in, kv_norm, w_uk, w_uv, cmp_pos, cmp_w1, cmp_w2, w_out):
    bsz, seq, _ = u.shape
    pos = jnp.arange(seq)
    (a_q_nope, a_q_rope, a_ckv, a_k_rope, i_q, i_k, i_w, b_q, b_kv, b_gate) = split_cols(_proj(u, w_in), AB_WIDTHS)
    q_nope = a_q_nope.reshape(bsz, seq, A_HEADS, A_NOPE)
    q_rope = rope(a_q_rope.reshape(bsz, seq, A_HEADS, A_ROPE), pos)
    q_abs = jnp.einsum('blhd,hdc->blhc', q_nope, w_uk)
    q_cat = jnp.concatenate([q_abs, q_rope], axis=-1)
    c_kv = rms_norm(a_ckv, kv_norm)
    k_rope = rope(a_k_rope[:, :, None, :], pos)[:, :, 0]
    kv_lat = jnp.concatenate([c_kv, k_rope], axis=-1)
    idx_q = rope(i_q.reshape(bsz, seq, IDX_HEADS, IDX_DIM), pos)
    idx_k = rope(i_k[:, :, None, :], pos)[:, :, 0]
    bf = jnp.bfloat16
    o_a_t = _dsa_attention_t(
        idx_q.transpose(0, 2, 3, 1).astype(bf), idx_k.astype(bf), i_w.transpose(0, 2, 1),
        q_cat.transpose(0, 2, 3, 1).astype(bf), kv_lat.astype(bf),
        c_kv.reshape(bsz, seq // TK, TK, A_KV_RANK).transpose(0, 1, 3, 2).astype(bf),
        w_uv.transpose(0, 2, 1).astype(bf))
    bq = rope(b_q.reshape(bsz, seq, B_HEADS, B_HEAD_DIM), pos)
    kv = b_kv.reshape(bsz, seq, 6, B_KV_HEADS, B_HEAD_DIM)
    o_b_t = nsa_attention_t(bq, rope(kv[:, :, 0], pos), kv[:, :, 1],
                            rope(kv[:, :, 2], pos), kv[:, :, 3],
                            rope(kv[:, :, 4], pos), kv[:, :, 5],
                            b_gate, cmp_pos, cmp_w1, cmp_w2)
    o_t = jnp.concatenate([o_a_t, o_b_t], axis=1)
    return _proj(o_t.transpose(0, 2, 1), w_out)


def nsa_attention_t(q, k_cmp, v_cmp, k_slc, v_slc, k_win, v_win, gates, cmp_pos, cmp_w1, cmp_w2):
    bsz, seq = q.shape[:2]
    kvh, grp, hd = B_KV_HEADS, B_GROUP, B_HEAD_DIM
    bf = jnp.bfloat16
    n_cmp = (seq - CMP_LEN) // CMP_STRIDE + 1
    n_cmp_pad = -(-n_cmp // 128) * 128
    tok = np.arange(n_cmp)[:, None] * CMP_STRIDE + np.arange(CMP_LEN)[None, :]

    def compress(a, j):
        blk = a[:, tok] + cmp_pos[j][:, None, :]
        blk = blk.transpose(0, 1, 3, 2, 4).reshape(bsz, n_cmp, kvh, CMP_LEN * hd)
        out = jax.nn.gelu(blk @ cmp_w1[j]) @ cmp_w2[j]
        return jnp.pad(out, ((0, 0), (0, n_cmp_pad - n_cmp), (0, 0), (0, 0)))

    kc = compress(k_cmp, 0).transpose(0, 2, 1, 3).astype(bf)
    vct = compress(v_cmp, 1).transpose(0, 2, 3, 1).astype(bf)
    n_slc = seq // SLC_LEN
    cmp_start = np.arange(n_cmp_pad) * CMP_STRIDE
    slc_start = np.arange(n_slc) * SLC_LEN
    ovt = ((cmp_start[None, :] < slc_start[:, None] + SLC_LEN)
           & (cmp_start[None, :] + CMP_LEN > slc_start[:, None]) & (np.arange(n_cmp_pad)[None, :] < n_cmp))
    ovt = jnp.asarray(ovt, bf)
    qt = q.reshape(bsz, seq, kvh, grp, hd).transpose(0, 2, 3, 4, 1).astype(bf)
    k_t = lambda a: a.transpose(0, 2, 1, 3).astype(bf)
    v_t = lambda a: a.reshape(bsz, seq // TK, TK, kvh, hd).transpose(0, 3, 1, 4, 2).astype(bf)
    gates_t = gates.reshape(bsz, seq, kvh, 3 * grp).transpose(0, 2, 3, 1)
    return _nsa_attention_t(qt, kc, vct, ovt, k_t(k_slc), v_t(v_slc), k_t(k_win), v_t(v_win), gates_t)


def hgrn2_mixer(u, w_in, lower_bound, out_norm, w_out):
    return _proj(_hgrn_recurrence(_proj(u, w_in), lower_bound, out_norm), w_out)


def routed_experts(xb_tok, e_id, e_w, w_gate_up, w_down, layer):
    n_as = e_id.shape[0]
    tok = jnp.arange(n_as, dtype=jnp.int32) // GROUP_TOPK
    onehot = (e_id[:, None] == jnp.arange(N_EXPERTS, dtype=e_id.dtype)[None, :]).astype(jnp.int32)
    csum = jnp.cumsum(onehot, axis=0)
    counts = csum[-1]
    rank = jnp.take_along_axis(csum, e_id[:, None], axis=1)[:, 0] - 1
    padded = (counts + MOE_BLOCK - 1) // MOE_BLOCK * MOE_BLOCK
    pad_end = jnp.cumsum(padded)
    pad_start = pad_end - padded
    dest = pad_start[e_id] + rank
    n_blk = -(-(n_as + N_EXPERTS * (MOE_BLOCK - 1)) // MOE_BLOCK)
    rows = n_blk * MOE_BLOCK
    row_tok = jnp.zeros((rows,), jnp.int32).at[dest].set(tok)
    row_w = jnp.zeros((rows,), jnp.float32).at[dest].set(e_w)
    blk_e = jnp.minimum(jnp.searchsorted(pad_end, jnp.arange(n_blk) * MOE_BLOCK, side='right'),
                        N_EXPERTS - 1).astype(jnp.int32)
    n_used = (pad_end[-1:] // MOE_BLOCK).astype(jnp.int32)
    yb = _moe_blocks(blk_e, n_used, xb_tok[row_tok], w_gate_up, w_down, row_w[:, None], layer)
    return [yb[dest[s::GROUP_TOPK]] for s in range(GROUP_TOPK)]


def hier_moe(xb_tok, logits, b_group, b_expert, w_gate_up, w_down, layer):
    n_tok = xb_tok.shape[0]
    g_logits = logits[:, :N_GROUPS] + b_group.astype(jnp.float32)
    g_prob = jax.nn.softmax(g_logits, axis=-1)
    grp = jnp.argmax(g_logits, axis=-1)
    g_w = jnp.take_along_axis(g_prob, grp[:, None], axis=-1)
    e_logits = logits[:, N_GROUPS:N_GROUPS + N_EXPERTS] + b_expert.astype(jnp.float32)
    e_logits = e_logits.reshape(n_tok, N_GROUPS, EXPERTS_PER_GROUP)
    e_logits = jnp.take_along_axis(e_logits, grp[:, None, None], axis=1)[:, 0]
    top_v, top_i = lax.top_k(e_logits, GROUP_TOPK)
    e_w = jax.nn.softmax(top_v, axis=-1) * g_w
    e_id = (grp[:, None] * EXPERTS_PER_GROUP + top_i).astype(jnp.int32)
    return routed_experts(xb_tok, e_id.reshape(-1), e_w.reshape(-1).astype(jnp.float32), w_gate_up, w_down, layer)


def kernel(x, mix_norm, ffn_norm, final_norm, ab_w_in, dsa_kv_norm, dsa_w_uk, dsa_w_uv, nsa_cmp_pos, nsa_cmp_w1, nsa_cmp_w2, ab_w_out, hgrn_w_in, hgrn_lb_logits, hgrn_out_norm, hgrn_w_out, moe_w_group, moe_b_group, moe_w_expert, moe_b_expert, moe_w_gate_up, moe_w_down):
    lb_p = jax.nn.softmax(hgrn_lb_logits.astype(jnp.float32), axis=0)
    lower_bounds = jnp.cumsum(lb_p, axis=0) - lb_p[0]
    bsz, seq, d = x.shape
    bf = jnp.bfloat16
    h = x.reshape(bsz * seq, d)
    pending = []
    for layer in range(DEPTH):
        j = layer // 2
        if pending:
            h, u = _add_norm(h, pending, mix_norm[layer], bf)
        else:
            (u,) = _add_norm(h, [], mix_norm[layer], bf, write_h=False)
        u = u.reshape(bsz, seq, d)
        if layer % 2 == 0:
            m = dsa_nsa_mixer(u, ab_w_in[j], dsa_kv_norm[j], dsa_w_uk[j], dsa_w_uv[j],
                              nsa_cmp_pos[j], nsa_cmp_w1[j], nsa_cmp_w2[j], ab_w_out[j])
        else:
            m = hgrn2_mixer(u, hgrn_w_in[j], lower_bounds[layer], hgrn_out_norm[j], hgrn_w_out[j])
        w_router = jnp.concatenate([moe_w_group[layer], moe_w_expert[layer]], axis=1)
        w_router = jnp.pad(w_router, ((0, 0), (0, ROUTER_PAD - w_router.shape[1])))
        h, u, logits = _add_norm(h, [m.reshape(bsz * seq, d)], ffn_norm[layer], bf, w_router=w_router)
        pending = hier_moe(u, logits, moe_b_group[layer], moe_b_expert[layer], moe_w_gate_up, moe_w_down, layer)
    (out,) = _add_norm(h, pending, final_norm, jnp.float32, write_h=False)
    return out.reshape(bsz, seq, d)
```

```python
import functools

import numpy as np
import jax
import jax.numpy as jnp
from jax import lax
from jax.experimental import pallas as pl
from jax.experimental.pallas import tpu as pltpu

D_MODEL = 4096
BATCH = 4
SEQ = 2048
DEPTH = 2

ROPE_THETA = 10000.0
NORM_EPS = 1e-6
Q_BLOCK = 64

A_HEADS = 16
A_NOPE = 128
A_ROPE = 64
A_KV_RANK = 512
A_V_DIM = 128
IDX_HEADS = 32
IDX_DIM = 64
IDX_TOPK = 256

B_HEADS = 16
B_KV_HEADS = 4
B_GROUP = B_HEADS // B_KV_HEADS
B_HEAD_DIM = 128
CMP_LEN = 32
CMP_STRIDE = 16
CMP_HIDDEN = 256
SLC_LEN = 64
SLC_COUNT = 16
WIN_LEN = 512

AB_WIDTHS = (
    A_HEADS * A_NOPE,
    A_HEADS * A_ROPE,
    A_KV_RANK,
    A_ROPE,
    IDX_HEADS * IDX_DIM,
    IDX_DIM,
    IDX_HEADS,
    B_HEADS * B_HEAD_DIM,
    6 * B_KV_HEADS * B_HEAD_DIM,
    3 * B_HEADS,
)
AB_IN = sum(AB_WIDTHS)
AB_OUT = A_HEADS * A_V_DIM + B_HEADS * B_HEAD_DIM

C_HEAD_DIM = 128
C_HEADS = D_MODEL // C_HEAD_DIM
C_WIDTH = C_HEADS * C_HEAD_DIM
C_CHUNK = 64

N_GROUPS = 8
EXPERTS_PER_GROUP = 8
N_EXPERTS = N_GROUPS * EXPERTS_PER_GROUP
GROUP_TOPK = 2
EXPERT_FF = 3 * D_MODEL // 32
MOE_BLOCK = 128

VMEM_LIMIT_BYTES = 48 * 1024 * 1024


MM_TM = 1024
MM_TN = 512


def _mm_kernel(x_ref, w_ref, o_ref):
    o_ref[...] = jnp.dot(x_ref[...], w_ref[...].astype(jnp.bfloat16), preferred_element_type=jnp.float32)


def _matmul(x, w):
    m, kd = x.shape
    n = w.shape[1]
    assert m % MM_TM == 0
    return pl.pallas_call(
        _mm_kernel,
        grid=(pl.cdiv(n, MM_TN), m // MM_TM),
        in_specs=[pl.BlockSpec((MM_TM, kd), lambda j, i: (i, 0)),
                  pl.BlockSpec((kd, MM_TN), lambda j, i: (0, j))],
        out_specs=pl.BlockSpec((MM_TM, MM_TN), lambda j, i: (i, j)),
        out_shape=jax.ShapeDtypeStruct((m, n), jnp.float32),
        compiler_params=pltpu.CompilerParams(
            dimension_semantics=("parallel", "parallel"),
            vmem_limit_bytes=VMEM_LIMIT_BYTES),
        name="matmul",
    )(x, w)


def _mm_heads_kernel(*refs):
    *x_refs, w_ref, o_ref = refs
    x = jnp.concatenate([x_ref[h] for x_ref in x_refs for h in range(x_ref.shape[0])], axis=1)
    o_ref[...] = jnp.dot(x, w_ref[...].astype(jnp.bfloat16), preferred_element_type=jnp.float32)


def _matmul_heads(xs, w):
    bsz, _, seq, hd = xs[0].shape
    kd, n = w.shape
    assert sum(x.shape[1] for x in xs) * hd == kd and seq % MM_TM == 0
    tiles = seq // MM_TM
    return pl.pallas_call(
        _mm_heads_kernel,
        grid=(pl.cdiv(n, MM_TN), bsz * tiles),
        in_specs=[pl.BlockSpec((None, x.shape[1], MM_TM, hd), lambda j, i: (i // tiles, 0, i % tiles, 0)) for x in xs]
        + [pl.BlockSpec((kd, MM_TN), lambda j, i: (0, j))],
        out_specs=pl.BlockSpec((MM_TM, MM_TN), lambda j, i: (i, j)),
        out_shape=jax.ShapeDtypeStruct((bsz * seq, n), jnp.float32),
        compiler_params=pltpu.CompilerParams(
            dimension_semantics=("parallel", "parallel"),
            vmem_limit_bytes=VMEM_LIMIT_BYTES),
        name="matmul_heads",
    )(*xs, w)


def _proj(u, w):
    lead = u.shape[:-1]
    return _matmul(u.reshape(-1, u.shape[-1]).astype(jnp.bfloat16), w).reshape(*lead, w.shape[1])


MOE_KC = 1024
MOE_VMEM_LIMIT_BYTES = 58 * 1024 * 1024
MOE_SPLITS = 4


def _moe_kernel(blk_e_ref, n_used_ref, x_ref, wgu_ref, wd_ref, rw_ref, *rest, first_blk):
    del blk_e_ref
    o_ref = rest[-1]
    i = pl.program_id(0) + first_blk
    d_model, ff2 = wgu_ref.shape
    ff = ff2 // 2

    @pl.when(i < n_used_ref[0])
    def _():
        gu = jnp.zeros((x_ref.shape[0], ff2), jnp.float32)
        for k0 in range(0, d_model, MOE_KC):
            gu = gu + jnp.dot(x_ref[:, k0:k0 + MOE_KC], wgu_ref[k0:k0 + MOE_KC, :].astype(jnp.bfloat16),
                              preferred_element_type=jnp.float32)
        hid = (jax.nn.silu(gu[:, :ff]) * gu[:, ff:]).astype(jnp.bfloat16)
        rw = rw_ref[...]
        for n0 in range(0, d_model, MOE_KC):
            o_ref[:, n0:n0 + MOE_KC] = rw * jnp.dot(hid, wd_ref[:, n0:n0 + MOE_KC].astype(jnp.bfloat16),
                                                    preferred_element_type=jnp.float32)

    @pl.when(i >= n_used_ref[0])
    def _():
        o_ref[...] = jnp.zeros_like(o_ref)


def _moe_blocks(blk_e, n_used, xb, w_gate_up, w_down, row_w, layer, first_blk, prev):
    rows, d = xb.shape
    n_blk = rows // MOE_BLOCK
    ff2 = w_gate_up.shape[-1]
    in_specs = [pl.BlockSpec((MOE_BLOCK, d), lambda i, be, nu: (i, 0)),
                pl.BlockSpec((None, None, d, ff2), lambda i, be, nu: (layer, be[i + first_blk], 0, 0)),
                pl.BlockSpec((None, None, ff2 // 2, d), lambda i, be, nu: (layer, be[i + first_blk], 0, 0)),
                pl.BlockSpec((MOE_BLOCK, 1), lambda i, be, nu: (i, 0))]
    operands = [blk_e, n_used, xb, w_gate_up, w_down, row_w]
    aliases = {}
    if prev is not None:
        in_specs.append(pl.BlockSpec(memory_space=pl.ANY))
        aliases = {len(operands): 0}
        operands.append(prev)
    grid_spec = pltpu.PrefetchScalarGridSpec(
        num_scalar_prefetch=2,
        grid=(n_blk,),
        in_specs=in_specs,
        out_specs=pl.BlockSpec((MOE_BLOCK, d), lambda i, be, nu: (i + first_blk, 0)),
    )
    return pl.pallas_call(
        functools.partial(_moe_kernel, first_blk=first_blk),
        grid_spec=grid_spec,
        out_shape=jax.ShapeDtypeStruct((blk_e.shape[0] * MOE_BLOCK, d), jnp.float32),
        input_output_aliases=aliases,
        compiler_params=pltpu.CompilerParams(dimension_semantics=("arbitrary",),
                                             vmem_limit_bytes=MOE_VMEM_LIMIT_BYTES),
        name="moe_experts",
    )(*operands)


NORM_TM = 256
ROUTER_PAD = 128


def _add_norm_kernel(*refs, n_add, has_router, write_h):
    it = iter(refs)
    h_ref = next(it)
    add_refs = [next(it) for _ in range(n_add)]
    g_ref = next(it)
    wr_ref = next(it) if has_router else None
    h_out = next(it) if write_h else None
    u_ref = next(it)
    lg_ref = next(it) if has_router else None
    hv = h_ref[...]
    for a_ref in add_refs:
        hv = hv + a_ref[...]
    if write_h:
        h_out[...] = hv
    y = hv * lax.rsqrt(jnp.mean(hv * hv, axis=-1, keepdims=True) + NORM_EPS) * g_ref[...]
    u_ref[...] = y.astype(u_ref.dtype)
    if has_router:
        lg_ref[...] = jnp.dot(y, wr_ref[...], precision=lax.Precision.HIGHEST, preferred_element_type=jnp.float32)


def _add_norm(h, adds, gain, out_dtype, w_router=None, write_h=True):
    t, d = h.shape
    row = pl.BlockSpec((NORM_TM, d), lambda i: (i, 0))
    operands = [h, *adds, gain.reshape(1, d).astype(jnp.float32)]
    in_specs = [row] * (1 + len(adds)) + [pl.BlockSpec((1, d), lambda i: (0, 0))]
    out_shape, out_specs = [], []
    if w_router is not None:
        operands.append(w_router)
        in_specs.append(pl.BlockSpec(w_router.shape, lambda i: (0, 0)))
    if write_h:
        out_shape.append(jax.ShapeDtypeStruct((t, d), jnp.float32))
        out_specs.append(row)
    out_shape.append(jax.ShapeDtypeStruct((t, d), out_dtype))
    out_specs.append(row)
    if w_router is not None:
        out_shape.append(jax.ShapeDtypeStruct((t, w_router.shape[1]), jnp.float32))
        out_specs.append(pl.BlockSpec((NORM_TM, w_router.shape[1]), lambda i: (i, 0)))
    return pl.pallas_call(
        functools.partial(_add_norm_kernel, n_add=len(adds), has_router=w_router is not None, write_h=write_h),
        grid=(t // NORM_TM,),
        in_specs=in_specs,
        out_specs=out_specs,
        out_shape=out_shape,
        compiler_params=pltpu.CompilerParams(dimension_semantics=("parallel",),
                                             vmem_limit_bytes=VMEM_LIMIT_BYTES),
        name="add_norm",
    )(*operands)


TQ = 256
TK = 512
IK = 256
SUBLANES = 8
NEG_INF = float("-inf")
INT_MIN = -2 ** 31


def _fold8(x, op):
    r, c = x.shape
    return op(x.reshape(r // SUBLANES, SUBLANES, c), axis=0)


def _attend_heads(k_ref, vt_ref, q_ref, n_heads, bias_fn, c_lo, c_hi, s_scr, acc_scr, scale, emit):
    tq = q_ref.shape[-1]

    def scores(c, qt, slot, m8):
        k0 = pl.multiple_of(c * TK, TK)
        s = jnp.dot(k_ref[pl.ds(k0, TK), :], qt, preferred_element_type=jnp.float32) * scale + bias_fn(c)
        s_scr[slot, pl.ds(k0, TK), :] = s
        return jnp.maximum(m8, _fold8(s, jnp.max))

    neg = jnp.full((SUBLANES, tq), NEG_INF, jnp.float32)
    q0 = q_ref[0]
    m8_first = lax.fori_loop(c_lo, c_hi, lambda c, m8: scores(c, q0, 0, m8), neg)

    def head(h, m8):
        slot = h % 2
        m = jnp.max(m8, axis=0, keepdims=True)
        m = jnp.where(m == NEG_INF, 0.0, m)
        acc_scr[...] = jnp.zeros_like(acc_scr)
        q_next = q_ref[jnp.minimum(h + 1, n_heads - 1)]

        def body(c, carry):
            d8, m8n = carry
            k0 = pl.multiple_of(c * TK, TK)
            p = jnp.exp(s_scr[slot, pl.ds(k0, TK), :] - m)
            acc_scr[...] += jnp.dot(vt_ref[c], p.astype(jnp.bfloat16), preferred_element_type=jnp.float32)
            return d8 + _fold8(p, jnp.sum), scores(c, q_next, 1 - slot, m8n)

        d8, m8_next = lax.fori_loop(c_lo, c_hi, body, (jnp.zeros((SUBLANES, tq), jnp.float32), neg))
        den = jnp.sum(d8, axis=0, keepdims=True)
        emit(h, acc_scr[...] * (1.0 / jnp.where(den > 0, den, 1.0)))
        return m8_next

    lax.fori_loop(0, n_heads, head, m8_first)


def _dsa_kernel(iq_ref, ik_ref, iw_ref, q_ref, kv_ref, ckvt_ref, wuvt_ref, o_ref,
                key_scr, bias_scr, s_scr, acc_scr, *, topk, idx_scale, scale):
    qb = pl.program_id(1)
    nk = ((qb + 1) * TQ + TK - 1) // TK
    n_ik = nk * (TK // IK)
    tpos = qb * TQ + lax.broadcasted_iota(jnp.int32, (1, TQ), 1)
    row = lax.broadcasted_iota(jnp.int32, (IK, 1), 0)
    w_rows = iw_ref[...] * idx_scale
    n_idx_heads = iq_ref.shape[0]

    def index_chunk(c, carry):
        k0 = pl.multiple_of(c * IK, IK)
        kblk = ik_ref[pl.ds(k0, IK), :]
        acc = jnp.zeros((IK, TQ), jnp.float32)
        for h in range(n_idx_heads):
            lg = jnp.dot(kblk, iq_ref[h], preferred_element_type=jnp.float32)
            acc = acc + w_rows[h:h + 1, :] * jnp.maximum(lg, 0.0)
        acc = acc + 0.0
        bits = lax.bitcast_convert_type(acc, jnp.int32)
        key = bits ^ ((bits >> 31) & 0x7FFFFFFF)
        key_scr[pl.ds(k0, IK), :] = jnp.where(k0 + row <= tpos, key, INT_MIN)
        return carry

    lax.fori_loop(0, n_ik, index_chunk, 0)

    def count(pred_fn):
        def body(c, cnt8):
            k0 = pl.multiple_of(c * IK, IK)
            hit = jnp.where(pred_fn(key_scr[pl.ds(k0, IK), :]), 1, 0).astype(jnp.int32)
            return cnt8 + _fold8(hit, jnp.sum)
        cnt8 = lax.fori_loop(0, n_ik, body, jnp.zeros((SUBLANES, TQ), jnp.int32))
        return jnp.sum(cnt8, axis=0, keepdims=True)

    thr = jnp.where(count(lambda kk: kk >= 0) >= topk, 0, INT_MIN).astype(jnp.int32) + jnp.zeros((1, TQ), jnp.int32)

    def bit_step(i, thr):
        cand = thr + jnp.left_shift(jnp.int32(1), 30 - i)
        return jnp.where(count(lambda kk: kk >= cand) >= topk, cand, thr)

    thr = lax.fori_loop(0, 31, bit_step, thr)
    thr = jnp.maximum(thr, INT_MIN + 1)
    need = (topk - count(lambda kk: kk > thr)).astype(jnp.float32)

    tri = jnp.where(lax.broadcasted_iota(jnp.int32, (IK, IK), 1) <= lax.broadcasted_iota(jnp.int32, (IK, IK), 0),
                    1.0, 0.0).astype(jnp.bfloat16)

    def bias_chunk(c, run):
        k0 = pl.multiple_of(c * IK, IK)
        kk = key_scr[pl.ds(k0, IK), :]
        eq = jnp.where(kk == thr, 1.0, 0.0)
        pref = jnp.dot(tri, eq.astype(jnp.bfloat16), preferred_element_type=jnp.float32) + run
        tie_ok = jnp.where(pref <= need, 0.0, NEG_INF)
        bias_scr[pl.ds(k0, IK), :] = jnp.where(kk > thr, 0.0, jnp.where(kk == thr, tie_ok, NEG_INF))
        return pref[IK - 1:IK, :]

    lax.fori_loop(0, n_ik, bias_chunk, jnp.zeros((1, TQ), jnp.float32))

    def emit(h, o_lat):
        d_v = wuvt_ref.shape[1]
        acc_scr[pl.ds(0, d_v), :] = jnp.dot(wuvt_ref[h], o_lat.astype(jnp.bfloat16),
                                            preferred_element_type=jnp.float32)
        o_ref[h] = acc_scr[pl.ds(0, d_v), :].T.astype(o_ref.dtype)

    _attend_heads(kv_ref, ckvt_ref, q_ref, q_ref.shape[0],
                  lambda c: bias_scr[pl.ds(pl.multiple_of(c * TK, TK), TK), :], 0, nk, s_scr, acc_scr, scale, emit)


def _dsa_attention_t(idx_qt, idx_k, idx_wt, q_catt, kv_lat, ckvt, w_uvt):
    bsz, n_ih, d_i, seq = idx_qt.shape
    n_h, d_qk = q_catt.shape[1:3]
    d_v, rank = w_uvt.shape[1:]
    topk = min(IDX_TOPK, seq // 4)
    kern = functools.partial(_dsa_kernel, topk=topk, idx_scale=(IDX_DIM ** -0.5) * (IDX_HEADS ** -0.5),
                             scale=(A_NOPE + A_ROPE) ** -0.5)
    return pl.pallas_call(
        kern,
        grid=(bsz, seq // TQ),
        in_specs=[
            pl.BlockSpec((None, n_ih, d_i, TQ), lambda b, q: (b, 0, 0, q)),
            pl.BlockSpec((None, seq, d_i), lambda b, q: (b, 0, 0)),
            pl.BlockSpec((None, n_ih, TQ), lambda b, q: (b, 0, q)),
            pl.BlockSpec((None, n_h, d_qk, TQ), lambda b, q: (b, 0, 0, q)),
            pl.BlockSpec((None, seq, d_qk), lambda b, q: (b, 0, 0)),
            pl.BlockSpec((None, seq // TK, rank, TK), lambda b, q: (b, 0, 0, 0)),
            pl.BlockSpec((n_h, d_v, rank), lambda b, q: (0, 0, 0)),
        ],
        out_specs=pl.BlockSpec((None, n_h, TQ, d_v), lambda b, q: (b, 0, q, 0)),
        out_shape=jax.ShapeDtypeStruct((bsz, n_h, seq, d_v), jnp.bfloat16),
        scratch_shapes=[pltpu.VMEM((seq, TQ), jnp.int32), pltpu.VMEM((seq, TQ), jnp.float32),
                        pltpu.VMEM((2, seq, TQ), jnp.float32), pltpu.VMEM((rank, TQ), jnp.float32)],
        compiler_params=pltpu.CompilerParams(dimension_semantics=("parallel", "arbitrary"),
                                             vmem_limit_bytes=VMEM_LIMIT_BYTES),
        name="dsa_attention",
    )(idx_qt, idx_k, idx_wt, q_catt, kv_lat, ckvt, w_uvt)


def _nsa_kernel(q_ref, kc_ref, vct_ref, ovt_ref, ks_ref, vst_ref, kw_ref, vwt_ref, g_ref, o_ref,
                blk_scr, bias_scr, s_scr, acc_scr, mix_scr, *, n_sel, scale):
    qb = pl.program_id(2)
    nk = ((qb + 1) * TQ + TK - 1) // TK
    tpos = qb * TQ + lax.broadcasted_iota(jnp.int32, (1, TQ), 1)
    row = lax.broadcasted_iota(jnp.int32, (TK, 1), 0)
    n_grp, hd = q_ref.shape[0], q_ref.shape[1]
    n_cmp_pad = kc_ref.shape[0]
    n_slc = ovt_ref.shape[0]
    gate = lambda g, j: jax.nn.sigmoid(g_ref[pl.ds(3 * g + j, 1), :])
    head_rows = lambda g: pl.ds(pl.multiple_of(g * hd, hd), hd)

    c_last = lax.broadcasted_iota(jnp.int32, (n_cmp_pad, 1), 0) * CMP_STRIDE + (CMP_LEN - 1)
    cbias = jnp.where(c_last <= tpos, 0.0, NEG_INF)
    psum = jnp.zeros((n_cmp_pad, TQ), jnp.float32)
    for g in range(n_grp):
        s = jnp.dot(kc_ref[...], q_ref[g], preferred_element_type=jnp.float32) * scale + cbias
        m = jnp.max(s, axis=0, keepdims=True)
        m = jnp.where(m == NEG_INF, 0.0, m)
        e = jnp.exp(s - m)
        den = jnp.sum(e, axis=0, keepdims=True)
        p = e * (1.0 / jnp.where(den > 0, den, 1.0))
        psum = psum + p
        o_cmp = jnp.dot(vct_ref[...], p.astype(jnp.bfloat16), preferred_element_type=jnp.float32)
        mix_scr[pl.ds(g * hd, hd), :] = gate(g, 0) * o_cmp

    hi = psum.astype(jnp.bfloat16)
    r1 = psum - hi.astype(jnp.float32)
    mid = r1.astype(jnp.bfloat16)
    lo = (r1 - mid.astype(jnp.float32)).astype(jnp.bfloat16)
    ovt = ovt_ref[...]
    imp = (jnp.dot(ovt, hi, preferred_element_type=jnp.float32)
           + jnp.dot(ovt, mid, preferred_element_type=jnp.float32)
           + jnp.dot(ovt, lo, preferred_element_type=jnp.float32))
    jidx = lax.broadcasted_iota(jnp.int32, (n_slc, 1), 0)
    t_blk = tpos // SLC_LEN
    forced = (jidx == 0) | (jidx == t_blk) | (jidx == t_blk - 1)
    imp = jnp.where(forced, jnp.inf, imp)
    imp = jnp.where(jidx * SLC_LEN <= tpos, imp, NEG_INF)
    rank = jnp.zeros((n_slc, TQ), jnp.int32)
    for jp in range(n_slc):
        r = imp[jp:jp + 1, :]
        rank = rank + jnp.where(r > imp, 1, jnp.where(r == imp, jnp.where(jp < jidx, 1, 0), 0))
    blk_scr[...] = jnp.where(rank < n_sel, 0.0, NEG_INF)

    blocks_per_chunk = TK // SLC_LEN

    def slc_bias_chunk(c, carry):
        k0 = pl.multiple_of(c * TK, TK)
        rows = [jnp.broadcast_to(blk_scr[pl.ds(c * blocks_per_chunk + i, 1), :], (SLC_LEN, TQ))
                for i in range(blocks_per_chunk)]
        bias_scr[pl.ds(k0, TK), :] = jnp.where(k0 + row <= tpos, jnp.concatenate(rows, axis=0), NEG_INF)
        return carry

    lax.fori_loop(0, nk, slc_bias_chunk, 0)

    def slc_bias(c):
        return bias_scr[pl.ds(pl.multiple_of(c * TK, TK), TK), :]

    def win_bias(c):
        spos = c * TK + row
        return jnp.where(spos <= tpos, jnp.where(spos > tpos - WIN_LEN, 0.0, NEG_INF), NEG_INF)

    w_lo = jnp.maximum(qb * TQ - WIN_LEN + 1, 0) // TK

    def add_branch(j):
        def emit(g, o):
            mix_scr[head_rows(g), :] += gate(g, j) * o
        return emit

    _attend_heads(ks_ref, vst_ref, q_ref, n_grp, slc_bias, 0, nk, s_scr, acc_scr, scale, add_branch(1))
    _attend_heads(kw_ref, vwt_ref, q_ref, n_grp, win_bias, w_lo, nk, s_scr, acc_scr, scale, add_branch(2))
    for g in range(n_grp):
        o_ref[g] = mix_scr[pl.ds(g * hd, hd), :].T.astype(o_ref.dtype)


def _nsa_attention_t(qt, kc, vct, ovt, ks, vst, kw, vwt, gates_t):
    bsz, kvh, grp, hd, seq = qt.shape
    ncp = kc.shape[2]
    n_slc = ovt.shape[0]
    kern = functools.partial(_nsa_kernel, n_sel=min(SLC_COUNT, n_slc), scale=hd ** -0.5)
    kv_spec = pl.BlockSpec((None, None, seq, hd), lambda b, h, q: (b, h, 0, 0))
    vt_spec = pl.BlockSpec((None, None, seq // TK, hd, TK), lambda b, h, q: (b, h, 0, 0, 0))
    return pl.pallas_call(
        kern,
        grid=(bsz, kvh, seq // TQ),
        in_specs=[
            pl.BlockSpec((None, None, grp, hd, TQ), lambda b, h, q: (b, h, 0, 0, q)),
            pl.BlockSpec((None, None, ncp, hd), lambda b, h, q: (b, h, 0, 0)),
            pl.BlockSpec((None, None, hd, ncp), lambda b, h, q: (b, h, 0, 0)),
            pl.BlockSpec((n_slc, ncp), lambda b, h, q: (0, 0)),
            kv_spec, vt_spec, kv_spec, vt_spec,
            pl.BlockSpec((None, None, 3 * grp, TQ), lambda b, h, q: (b, h, 0, q)),
        ],
        out_specs=pl.BlockSpec((None, grp, TQ, hd), lambda b, h, q: (b, h, q, 0)),
        out_shape=jax.ShapeDtypeStruct((bsz, kvh * grp, seq, hd), jnp.bfloat16),
        scratch_shapes=[pltpu.VMEM((n_slc, TQ), jnp.float32), pltpu.VMEM((seq, TQ), jnp.float32),
                        pltpu.VMEM((2, seq, TQ), jnp.float32), pltpu.VMEM((hd, TQ), jnp.float32),
                        pltpu.VMEM((grp * hd, TQ), jnp.float32)],
        compiler_params=pltpu.CompilerParams(dimension_semantics=("parallel", "parallel", "arbitrary"),
                                             vmem_limit_bytes=VMEM_LIMIT_BYTES),
        name="nsa_attention",
    )(qt, kc, vct, ovt, ks, vst, kw, vwt, gates_t)


HG_C = 64
HG_SB = 8
HG_CT = 1024


def _cumsum_rows(x):
    n = x.shape[0]
    row = lax.broadcasted_iota(jnp.int32, (n, 1), 0)
    s = 1
    while s < n:
        x = x + jnp.where(row >= s, pltpu.roll(x, s, axis=0), 0.0)
        s *= 2
    return x


def _bcast_rows(ref, first, period, n):
    return jnp.concatenate(
        [jnp.broadcast_to(ref[pl.ds(first + p * period, 1), :], (period, ref.shape[1])) for p in range(n // period)],
        axis=0)


def _hgrn_kernel(q_ref, f_ref, i_ref, g_ref, lb_ref, gn_ref, o_ref, st_scr, k_scr, b_scr, v_scr):
    @pl.when(pl.program_id(2) == 0)
    def _():
        st_scr[...] = jnp.zeros_like(st_scr)

    c_len, d = HG_C, q_ref.shape[1]
    bf = jnp.bfloat16
    lb = lb_ref[...]
    gain = gn_ref[...]
    row = lax.broadcasted_iota(jnp.int32, (c_len, 1), 0)
    col = lax.broadcasted_iota(jnp.int32, (1, c_len), 1)
    ones_b = jnp.ones((d, d), bf)
    nt = (((1,), (1,)), ((), ()))
    tn = (((0,), (0,)), ((), ()))

    def chunk(c, carry):
        sl = pl.ds(pl.multiple_of(c * c_len, c_len), c_len)
        qr, fr, v, gr = q_ref[sl, :], f_ref[sl, :], i_ref[sl, :], g_ref[sl, :]
        qv = qr * jax.nn.sigmoid(qr)
        fg = lb + (1.0 - lb) * jax.nn.sigmoid(fr)
        kk = 1.0 - fg
        b = _cumsum_rows(jnp.log(fg))
        k_scr[...] = kk
        b_scr[...] = b
        v_scr[...] = v
        v_b = v.astype(bf)

        att = jnp.zeros((c_len, c_len), jnp.float32)
        size = c_len // 2
        while size >= HG_SB:
            ref = _bcast_rows(b_scr, size - 1, 2 * size, c_len)
            is_q = ((row // size) % 2) == 1
            a_q = qv * jnp.exp(jnp.where(is_q, b - ref, NEG_INF))
            k_f = kk * jnp.exp(jnp.where(is_q, NEG_INF, ref - b))
            att_l = lax.dot_general(a_q.astype(bf), k_f.astype(bf), nt, preferred_element_type=jnp.float32)
            if 2 * size < c_len:
                att_l = jnp.where((row // (2 * size)) == (col // (2 * size)), att_l, 0.0)
            att = att + att_l
            size //= 2
        o = jnp.dot(att.astype(bf), v_b, preferred_element_type=jnp.float32)

        rmod = row % HG_SB
        for j in range(HG_SB):
            kb = _bcast_rows(k_scr, j, HG_SB, c_len)
            bb = _bcast_rows(b_scr, j, HG_SB, c_len)
            vb = _bcast_rows(v_scr, j, HG_SB, c_len)
            x = qv * kb * jnp.exp(jnp.where(rmod >= j, b - bb, NEG_INF))
            o = o + jnp.dot(x.astype(bf), ones_b, preferred_element_type=jnp.float32) * vb

        st = st_scr[...]
        o = o + lax.dot_general((qv * jnp.exp(b)).astype(bf), st.astype(bf), nt, preferred_element_type=jnp.float32)
        b_last = b[c_len - 1:c_len, :]
        k_l = kk * jnp.exp(b_last - b)
        st_scr[...] = st * jnp.exp(b_last) + lax.dot_general(v_b, k_l.astype(bf), tn,
                                                            preferred_element_type=jnp.float32)

        ms = jnp.mean(o * o, axis=-1, keepdims=True)
        o_ref[sl, :] = o * lax.rsqrt(ms + NORM_EPS) * gain * (gr * jax.nn.sigmoid(gr))
        return carry

    lax.fori_loop(0, q_ref.shape[0] // c_len, chunk, 0, unroll=4)


def _hgrn_recurrence(proj, lower_bound, out_norm):
    bsz, seq, w4 = proj.shape
    width = w4 // 4
    n_heads = width // C_HEAD_DIM
    ct = min(HG_CT, seq)
    assert seq % ct == 0 and ct % HG_C == 0
    part = lambda j: pl.BlockSpec((None, ct, C_HEAD_DIM), lambda b, h, t: (b, t, j * n_heads + h))
    vec = pl.BlockSpec((None, 1, C_HEAD_DIM), lambda b, h, t: (h, 0, 0))
    return pl.pallas_call(
        _hgrn_kernel,
        grid=(bsz, n_heads, seq // ct),
        in_specs=[part(0), part(1), part(2), part(3), vec, vec],
        out_specs=pl.BlockSpec((None, ct, C_HEAD_DIM), lambda b, h, t: (b, t, h)),
        out_shape=jax.ShapeDtypeStruct((bsz, seq, width), jnp.float32),
        scratch_shapes=[pltpu.VMEM((C_HEAD_DIM, C_HEAD_DIM), jnp.float32)]
        + [pltpu.VMEM((HG_C, C_HEAD_DIM), jnp.float32)] * 3,
        compiler_params=pltpu.CompilerParams(dimension_semantics=("parallel", "parallel", "arbitrary"),
                                             vmem_limit_bytes=VMEM_LIMIT_BYTES),
        name="hgrn2_recurrence",
    )(proj, proj, proj, proj, lower_bound.reshape(n_heads, 1, C_HEAD_DIM).astype(jnp.float32),
      out_norm.reshape(n_heads, 1, C_HEAD_DIM).astype(jnp.float32))


def rms_norm(x, g):
    xf = x.astype(jnp.float32)
    y = xf * lax.rsqrt(jnp.mean(xf * xf, axis=-1, keepdims=True) + NORM_EPS)
    return (y * g.astype(jnp.float32)).astype(x.dtype)


def rope(x, pos):
    d = x.shape[-1]
    inv = ROPE_THETA ** (-jnp.arange(0, d, 2, dtype=jnp.float32) / d)
    ang = pos.astype(jnp.float32)[:, None] * inv[None, :]
    cos = jnp.cos(ang)[:, None, :]
    sin = jnp.sin(ang)[:, None, :]
    xf = x.astype(jnp.float32)
    x1, x2 = xf[..., : d // 2], xf[..., d // 2:]
    return jnp.concatenate([x1 * cos - x2 * sin, x2 * cos + x1 * sin], axis=-1).astype(x.dtype)


def split_cols(a, widths):
    return jnp.split(a, np.cumsum(widths)[:-1].tolist(), axis=-1)


def dsa_nsa_mixer(u, w_in, kv_norm, w_uk, w_uv, cmp_pos, cmp_w1, cmp_w2, w_out):
    bsz, seq, _ = u.shape
    pos = jnp.arange(seq)
    (a_q_nope, a_q_rope, a_ckv, a_k_rope, i_q, i_k, i_w, b_q, b_kv, b_gate) = split_cols(_proj(u, w_in), AB_WIDTHS)
    q_nope = a_q_nope.reshape(bsz, seq, A_HEADS, A_NOPE)
    q_rope = rope(a_q_rope.reshape(bsz, seq, A_HEADS, A_ROPE), pos)
    q_abs = jnp.einsum('blhd,hdc->blhc', q_nope, w_uk)
    q_cat = jnp.concatenate([q_abs, q_rope], axis=-1)
    c_kv = rms_norm(a_ckv, kv_norm)
    k_rope = rope(a_k_rope[:, :, None, :], pos)[:, :, 0]
    kv_lat = jnp.concatenate([c_kv, k_rope], axis=-1)
    idx_q = rope(i_q.reshape(bsz, seq, IDX_HEADS, IDX_DIM), pos)
    idx_k = rope(i_k[:, :, None, :], pos)[:, :, 0]
    bf = jnp.bfloat16
    o_a_t = _dsa_attention_t(
        idx_q.transpose(0, 2, 3, 1).astype(bf), idx_k.astype(bf), i_w.transpose(0, 2, 1),
        q_cat.transpose(0, 2, 3, 1).astype(bf), kv_lat.astype(bf),
        c_kv.reshape(bsz, seq // TK, TK, A_KV_RANK).transpose(0, 1, 3, 2).astype(bf),
        w_uv.transpose(0, 2, 1).astype(bf))
    bq = rope(b_q.reshape(bsz, seq, B_HEADS, B_HEAD_DIM), pos)
    kv = b_kv.reshape(bsz, seq, 6, B_KV_HEADS, B_HEAD_DIM)
    o_b_t = nsa_attention_t(bq, rope(kv[:, :, 0], pos), kv[:, :, 1],
                            rope(kv[:, :, 2], pos), kv[:, :, 3],
                            rope(kv[:, :, 4], pos), kv[:, :, 5],
                            b_gate, cmp_pos, cmp_w1, cmp_w2)
    return _matmul_heads([o_a_t, o_b_t], w_out).reshape(bsz, seq, w_out.shape[1])


def nsa_attention_t(q, k_cmp, v_cmp, k_slc, v_slc, k_win, v_win, gates, cmp_pos, cmp_w1, cmp_w2):
    bsz, seq = q.shape[:2]
    kvh, grp, hd = B_KV_HEADS, B_GROUP, B_HEAD_DIM
    bf = jnp.bfloat16
    n_cmp = (seq - CMP_LEN) // CMP_STRIDE + 1
    n_cmp_pad = -(-n_cmp // 128) * 128
    tok = np.arange(n_cmp)[:, None] * CMP_STRIDE + np.arange(CMP_LEN)[None, :]

    def compress(a, j):
        blk = a[:, tok] + cmp_pos[j][:, None, :]
        blk = blk.transpose(0, 1, 3, 2, 4).reshape(bsz, n_cmp, kvh, CMP_LEN * hd)
        out = jax.nn.gelu(blk @ cmp_w1[j]) @ cmp_w2[j]
        return jnp.pad(out, ((0, 0), (0, n_cmp_pad - n_cmp), (0, 0), (0, 0)))

    kc = compress(k_cmp, 0).transpose(0, 2, 1, 3).astype(bf)
    vct = compress(v_cmp, 1).transpose(0, 2, 3, 1).astype(bf)
    n_slc = seq // SLC_LEN
    cmp_start = np.arange(n_cmp_pad) * CMP_STRIDE
    slc_start = np.arange(n_slc) * SLC_LEN
    ovt = ((cmp_start[None, :] < slc_start[:, None] + SLC_LEN)
           & (cmp_start[None, :] + CMP_LEN > slc_start[:, None]) & (np.arange(n_cmp_pad)[None, :] < n_cmp))
    ovt = jnp.asarray(ovt, bf)
    qt = q.reshape(bsz, seq, kvh, grp, hd).transpose(0, 2, 3, 4, 1).astype(bf)
    k_t = lambda a: a.transpose(0, 2, 1, 3).astype(bf)
    v_t = lambda a: a.reshape(bsz, seq // TK, TK, kvh, hd).transpose(0, 3, 1, 4, 2).astype(bf)
    gates_t = gates.reshape(bsz, seq, kvh, 3 * grp).transpose(0, 2, 3, 1)
    return _nsa_attention_t(qt, kc, vct, ovt, k_t(k_slc), v_t(v_slc), k_t(k_win), v_t(v_win), gates_t)


def hgrn2_mixer(u, w_in, lower_bound, out_norm, w_out):
    return _proj(_hgrn_recurrence(_proj(u, w_in), lower_bound, out_norm), w_out)


def routed_experts(xb_tok, e_id, e_w, w_gate_up, w_down, layer):
    n_as = e_id.shape[0]
    tok = jnp.arange(n_as, dtype=jnp.int32) // GROUP_TOPK
    onehot = (e_id[:, None] == jnp.arange(N_EXPERTS, dtype=e_id.dtype)[None, :]).astype(jnp.int32)
    csum = jnp.cumsum(onehot, axis=0)
    counts = csum[-1]
    rank = jnp.take_along_axis(csum, e_id[:, None], axis=1)[:, 0] - 1
    padded = (counts + MOE_BLOCK - 1) // MOE_BLOCK * MOE_BLOCK
    pad_end = jnp.cumsum(padded)
    pad_start = pad_end - padded
    dest = pad_start[e_id] + rank
    n_blk = -(-(n_as + N_EXPERTS * (MOE_BLOCK - 1)) // MOE_BLOCK)
    rows = n_blk * MOE_BLOCK
    row_tok = jnp.zeros((rows,), jnp.int32).at[dest].set(tok)
    row_w = jnp.zeros((rows,), jnp.float32).at[dest].set(e_w)
    blk_e = jnp.minimum(jnp.searchsorted(pad_end, jnp.arange(n_blk) * MOE_BLOCK, side='right'),
                        N_EXPERTS - 1).astype(jnp.int32)
    n_used = (pad_end[-1:] // MOE_BLOCK).astype(jnp.int32)
    assert n_blk % MOE_SPLITS == 0
    per = n_blk // MOE_SPLITS
    yb = None
    for s in range(MOE_SPLITS):
        r = slice(s * per * MOE_BLOCK, (s + 1) * per * MOE_BLOCK)
        yb = _moe_blocks(blk_e, n_used, xb_tok[row_tok[r]], w_gate_up, w_down, row_w[r, None], layer, s * per, yb)
    return [yb[dest[s::GROUP_TOPK]] for s in range(GROUP_TOPK)]


def hier_moe(xb_tok, logits, b_group, b_expert, w_gate_up, w_down, layer):
    n_tok = xb_tok.shape[0]
    g_logits = logits[:, :N_GROUPS] + b_group.astype(jnp.float32)
    g_prob = jax.nn.softmax(g_logits, axis=-1)
    grp = jnp.argmax(g_logits, axis=-1)
    g_w = jnp.take_along_axis(g_prob, grp[:, None], axis=-1)
    e_logits = logits[:, N_GROUPS:N_GROUPS + N_EXPERTS] + b_expert.astype(jnp.float32)
    e_logits = e_logits.reshape(n_tok, N_GROUPS, EXPERTS_PER_GROUP)
    e_logits = jnp.take_along_axis(e_logits, grp[:, None, None], axis=1)[:, 0]
    top_v, top_i = lax.top_k(e_logits, GROUP_TOPK)
    e_w = jax.nn.softmax(top_v, axis=-1) * g_w
    e_id = (grp[:, None] * EXPERTS_PER_GROUP + top_i).astype(jnp.int32)
    return routed_experts(xb_tok, e_id.reshape(-1), e_w.reshape(-1).astype(jnp.float32), w_gate_up, w_down, layer)


def kernel(x, mix_norm, ffn_norm, final_norm, ab_w_in, dsa_kv_norm, dsa_w_uk, dsa_w_uv, nsa_cmp_pos, nsa_cmp_w1, nsa_cmp_w2, ab_w_out, hgrn_w_in, hgrn_lb_logits, hgrn_out_norm, hgrn_w_out, moe_w_group, moe_b_group, moe_w_expert, moe_b_expert, moe_w_gate_up, moe_w_down):
    lb_p = jax.nn.softmax(hgrn_lb_logits.astype(jnp.float32), axis=0)
    lower_bounds = jnp.cumsum(lb_p, axis=0) - lb_p[0]
    bsz, seq, d = x.shape
    bf = jnp.bfloat16
    h = x.reshape(bsz * seq, d)
    pending = []
    for layer in range(DEPTH):
        j = layer // 2
        if pending:
            h, u = _add_norm(h, pending, mix_norm[layer], bf)
        else:
            (u,) = _add_norm(h, [], mix_norm[layer], bf, write_h=False)
        u = u.reshape(bsz, seq, d)
        if layer % 2 == 0:
            m = dsa_nsa_mixer(u, ab_w_in[j], dsa_kv_norm[j], dsa_w_uk[j], dsa_w_uv[j],
                              nsa_cmp_pos[j], nsa_cmp_w1[j], nsa_cmp_w2[j], ab_w_out[j])
        else:
            m = hgrn2_mixer(u, hgrn_w_in[j], lower_bounds[layer], hgrn_out_norm[j], hgrn_w_out[j])
        w_router = jnp.concatenate([moe_w_group[layer], moe_w_expert[layer]], axis=1)
        w_router = jnp.pad(w_router, ((0, 0), (0, ROUTER_PAD - w_router.shape[1])))
        h, u, logits = _add_norm(h, [m.reshape(bsz * seq, d)], ffn_norm[layer], bf, w_router=w_router)
        pending = hier_moe(u, logits, moe_b_group[layer], moe_b_expert[layer], moe_w_gate_up, moe_w_down, layer)
    (out,) = _add_norm(h, pending, final_norm, jnp.float32, write_h=False)
    return out.reshape(bsz, seq, d)
```

```python
import functools

import numpy as np
import jax
import jax.numpy as jnp
from jax import lax
from jax.experimental import pallas as pl
from jax.experimental.pallas import tpu as pltpu

D_MODEL = 4096
BATCH = 4
SEQ = 2048
DEPTH = 2

ROPE_THETA = 10000.0
NORM_EPS = 1e-6
Q_BLOCK = 64

A_HEADS = 16
A_NOPE = 128
A_ROPE = 64
A_KV_RANK = 512
A_V_DIM = 128
IDX_HEADS = 32
IDX_DIM = 64
IDX_TOPK = 256

B_HEADS = 16
B_KV_HEADS = 4
B_GROUP = B_HEADS // B_KV_HEADS
B_HEAD_DIM = 128
CMP_LEN = 32
CMP_STRIDE = 16
CMP_HIDDEN = 256
SLC_LEN = 64
SLC_COUNT = 16
WIN_LEN = 512

AB_WIDTHS = (
    A_HEADS * A_NOPE,
    A_HEADS * A_ROPE,
    A_KV_RANK,
    A_ROPE,
    IDX_HEADS * IDX_DIM,
    IDX_DIM,
    IDX_HEADS,
    B_HEADS * B_HEAD_DIM,
    6 * B_KV_HEADS * B_HEAD_DIM,
    3 * B_HEADS,
)
AB_IN = sum(AB_WIDTHS)
AB_OUT = A_HEADS * A_V_DIM + B_HEADS * B_HEAD_DIM

C_HEAD_DIM = 128
C_HEADS = D_MODEL // C_HEAD_DIM
C_WIDTH = C_HEADS * C_HEAD_DIM
C_CHUNK = 64

N_GROUPS = 8
EXPERTS_PER_GROUP = 8
N_EXPERTS = N_GROUPS * EXPERTS_PER_GROUP
GROUP_TOPK = 2
EXPERT_FF = 3 * D_MODEL // 32
MOE_BLOCK = 128

VMEM_LIMIT_BYTES = 48 * 1024 * 1024


MM_TM = 1024
MM_TN = 512


def _mm_kernel(x_ref, w_ref, o_ref):
    o_ref[...] = jnp.dot(x_ref[...], w_ref[...].astype(jnp.bfloat16), preferred_element_type=jnp.float32)


def _matmul(x, w):
    m, kd = x.shape
    n = w.shape[1]
    assert m % MM_TM == 0
    return pl.pallas_call(
        _mm_kernel,
        grid=(pl.cdiv(n, MM_TN), m // MM_TM),
        in_specs=[pl.BlockSpec((MM_TM, kd), lambda j, i: (i, 0)),
                  pl.BlockSpec((kd, MM_TN), lambda j, i: (0, j))],
        out_specs=pl.BlockSpec((MM_TM, MM_TN), lambda j, i: (i, j)),
        out_shape=jax.ShapeDtypeStruct((m, n), jnp.float32),
        compiler_params=pltpu.CompilerParams(
            dimension_semantics=("parallel", "parallel"),
            vmem_limit_bytes=VMEM_LIMIT_BYTES),
        name="matmul",
    )(x, w)


def _mm_heads_kernel(*refs):
    *x_refs, w_ref, o_ref = refs
    x = jnp.concatenate([x_ref[h] for x_ref in x_refs for h in range(x_ref.shape[0])], axis=1)
    o_ref[...] = jnp.dot(x, w_ref[...].astype(jnp.bfloat16), preferred_element_type=jnp.float32)


def _matmul_heads(xs, w):
    bsz, _, seq, hd = xs[0].shape
    kd, n = w.shape
    assert sum(x.shape[1] for x in xs) * hd == kd and seq % MM_TM == 0
    tiles = seq // MM_TM
    return pl.pallas_call(
        _mm_heads_kernel,
        grid=(pl.cdiv(n, MM_TN), bsz * tiles),
        in_specs=[pl.BlockSpec((None, x.shape[1], MM_TM, hd), lambda j, i: (i // tiles, 0, i % tiles, 0)) for x in xs]
        + [pl.BlockSpec((kd, MM_TN), lambda j, i: (0, j))],
        out_specs=pl.BlockSpec((MM_TM, MM_TN), lambda j, i: (i, j)),
        out_shape=jax.ShapeDtypeStruct((bsz * seq, n), jnp.float32),
        compiler_params=pltpu.CompilerParams(
            dimension_semantics=("parallel", "parallel"),
            vmem_limit_bytes=VMEM_LIMIT_BYTES),
        name="matmul_heads",
    )(*xs, w)


def _proj(u, w):
    lead = u.shape[:-1]
    return _matmul(u.reshape(-1, u.shape[-1]).astype(jnp.bfloat16), w).reshape(*lead, w.shape[1])


MOE_KC = 1024
MOE_VMEM_LIMIT_BYTES = 58 * 1024 * 1024
MOE_SPLITS = 4


def _moe_kernel(blk_e_ref, n_used_ref, x_ref, wgu_ref, wd_ref, rw_ref, *rest, first_blk):
    del blk_e_ref
    o_ref = rest[-1]
    i = pl.program_id(0) + first_blk
    d_model, ff2 = wgu_ref.shape
    ff = ff2 // 2

    @pl.when(i < n_used_ref[0])
    def _():
        gu = jnp.zeros((x_ref.shape[0], ff2), jnp.float32)
        for k0 in range(0, d_model, MOE_KC):
            gu = gu + jnp.dot(x_ref[:, k0:k0 + MOE_KC], wgu_ref[k0:k0 + MOE_KC, :].astype(jnp.bfloat16),
                              preferred_element_type=jnp.float32)
        hid = (jax.nn.silu(gu[:, :ff]) * gu[:, ff:]).astype(jnp.bfloat16)
        rw = rw_ref[...]
        for n0 in range(0, d_model, MOE_KC):
            o_ref[:, n0:n0 + MOE_KC] = rw * jnp.dot(hid, wd_ref[:, n0:n0 + MOE_KC].astype(jnp.bfloat16),
                                                    preferred_element_type=jnp.float32)

    @pl.when(i >= n_used_ref[0])
    def _():
        o_ref[...] = jnp.zeros_like(o_ref)


def _moe_blocks(blk_e, n_used, xb, w_gate_up, w_down, row_w, layer, first_blk, prev):
    rows, d = xb.shape
    n_blk = rows // MOE_BLOCK
    ff2 = w_gate_up.shape[-1]
    in_specs = [pl.BlockSpec((MOE_BLOCK, d), lambda i, be, nu: (i, 0)),
                pl.BlockSpec((None, None, d, ff2), lambda i, be, nu: (layer, be[i + first_blk], 0, 0)),
                pl.BlockSpec((None, None, ff2 // 2, d), lambda i, be, nu: (layer, be[i + first_blk], 0, 0)),
                pl.BlockSpec((MOE_BLOCK, 1), lambda i, be, nu: (i, 0))]
    operands = [blk_e, n_used, xb, w_gate_up, w_down, row_w]
    aliases = {}
    if prev is not None:
        in_specs.append(pl.BlockSpec(memory_space=pl.ANY))
        aliases = {len(operands): 0}
        operands.append(prev)
    grid_spec = pltpu.PrefetchScalarGridSpec(
        num_scalar_prefetch=2,
        grid=(n_blk,),
        in_specs=in_specs,
        out_specs=pl.BlockSpec((MOE_BLOCK, d), lambda i, be, nu: (i + first_blk, 0)),
    )
    return pl.pallas_call(
        functools.partial(_moe_kernel, first_blk=first_blk),
        grid_spec=grid_spec,
        out_shape=jax.ShapeDtypeStruct((blk_e.shape[0] * MOE_BLOCK, d), jnp.float32),
        input_output_aliases=aliases,
        compiler_params=pltpu.CompilerParams(dimension_semantics=("arbitrary",),
                                             vmem_limit_bytes=MOE_VMEM_LIMIT_BYTES),
        name="moe_experts",
    )(*operands)


NORM_TM = 256
ROUTER_PAD = 128


def _add_norm_kernel(*refs, n_add, has_router, write_h):
    it = iter(refs)
    h_ref = next(it)
    add_refs = [next(it) for _ in range(n_add)]
    g_ref = next(it)
    wr_ref = next(it) if has_router else None
    h_out = next(it) if write_h else None
    u_ref = next(it)
    lg_ref = next(it) if has_router else None
    hv = h_ref[...]
    for a_ref in add_refs:
        hv = hv + a_ref[...]
    if write_h:
        h_out[...] = hv
    y = hv * lax.rsqrt(jnp.mean(hv * hv, axis=-1, keepdims=True) + NORM_EPS) * g_ref[...]
    u_ref[...] = y.astype(u_ref.dtype)
    if has_router:
        lg_ref[...] = jnp.dot(y, wr_ref[...], precision=lax.Precision.HIGHEST, preferred_element_type=jnp.float32)


def _add_norm(h, adds, gain, out_dtype, w_router=None, write_h=True):
    t, d = h.shape
    row = pl.BlockSpec((NORM_TM, d), lambda i: (i, 0))
    operands = [h, *adds, gain.reshape(1, d).astype(jnp.float32)]
    in_specs = [row] * (1 + len(adds)) + [pl.BlockSpec((1, d), lambda i: (0, 0))]
    out_shape, out_specs = [], []
    if w_router is not None:
        operands.append(w_router)
        in_specs.append(pl.BlockSpec(w_router.shape, lambda i: (0, 0)))
    if write_h:
        out_shape.append(jax.ShapeDtypeStruct((t, d), jnp.float32))
        out_specs.append(row)
    out_shape.append(jax.ShapeDtypeStruct((t, d), out_dtype))
    out_specs.append(row)
    if w_router is not None:
        out_shape.append(jax.ShapeDtypeStruct((t, w_router.shape[1]), jnp.float32))
        out_specs.append(pl.BlockSpec((NORM_TM, w_router.shape[1]), lambda i: (i, 0)))
    return pl.pallas_call(
        functools.partial(_add_norm_kernel, n_add=len(adds), has_router=w_router is not None, write_h=write_h),
        grid=(t // NORM_TM,),
        in_specs=in_specs,
        out_specs=out_specs,
        out_shape=out_shape,
        compiler_params=pltpu.CompilerParams(dimension_semantics=("parallel",),
                                             vmem_limit_bytes=VMEM_LIMIT_BYTES),
        name="add_norm",
    )(*operands)


TQ = 256
TK = 512
IK = 256
SUBLANES = 8
NEG_INF = float("-inf")
INT_MIN = -2 ** 31


def _fold8(x, op):
    r, c = x.shape
    return op(x.reshape(r // SUBLANES, SUBLANES, c), axis=0)


def _attend_heads(k_ref, vt_ref, q_ref, n_heads, bias_fn, c_lo, c_hi, s_scr, acc_scr, scale, emit):
    tq = q_ref.shape[-1]

    def scores(c, qt, slot, m8):
        k0 = pl.multiple_of(c * TK, TK)
        s = jnp.dot(k_ref[pl.ds(k0, TK), :], qt, preferred_element_type=jnp.float32) * scale + bias_fn(c)
        s_scr[slot, pl.ds(k0, TK), :] = s
        return jnp.maximum(m8, _fold8(s, jnp.max))

    neg = jnp.full((SUBLANES, tq), NEG_INF, jnp.float32)
    q0 = q_ref[0]
    m8_first = lax.fori_loop(c_lo, c_hi, lambda c, m8: scores(c, q0, 0, m8), neg)

    def head(h, m8):
        slot = h % 2
        m = jnp.max(m8, axis=0, keepdims=True)
        m = jnp.where(m == NEG_INF, 0.0, m)
        acc_scr[...] = jnp.zeros_like(acc_scr)
        q_next = q_ref[jnp.minimum(h + 1, n_heads - 1)]

        def body(c, carry):
            d8, m8n = carry
            k0 = pl.multiple_of(c * TK, TK)
            p = jnp.exp(s_scr[slot, pl.ds(k0, TK), :] - m)
            acc_scr[...] += jnp.dot(vt_ref[c], p.astype(jnp.bfloat16), preferred_element_type=jnp.float32)
            return d8 + _fold8(p, jnp.sum), scores(c, q_next, 1 - slot, m8n)

        d8, m8_next = lax.fori_loop(c_lo, c_hi, body, (jnp.zeros((SUBLANES, tq), jnp.float32), neg))
        den = jnp.sum(d8, axis=0, keepdims=True)
        emit(h, acc_scr[...] * (1.0 / jnp.where(den > 0, den, 1.0)))
        return m8_next

    lax.fori_loop(0, n_heads, head, m8_first)


def _dsa_queries(qn_ref, qr_ref, cos_ref, sin_ref, wukt_ref, qcat_scr):
    bf = jnp.bfloat16
    n_heads, rank, d_nope = wukt_ref.shape
    nt = (((1,), (1,)), ((), ()))
    for h in range(n_heads):
        qn_h = qn_ref[:, h * d_nope:(h + 1) * d_nope].astype(bf)
        qcat_scr[h, 0:rank, :] = lax.dot_general(wukt_ref[h], qn_h, nt, preferred_element_type=jnp.float32).astype(bf)
    x = qr_ref[...]
    width = x.shape[1]
    d_rope = width // n_heads
    half = d_rope // 2
    reps = width // cos_ref.shape[1]
    cos = jnp.concatenate([cos_ref[...]] * reps, axis=1)
    sin = jnp.concatenate([sin_ref[...]] * reps, axis=1)
    lane = lax.broadcasted_iota(jnp.int32, (1, width), 1)
    partner = jnp.where((lane % d_rope) < half, pltpu.roll(x, width - half, axis=1), pltpu.roll(x, half, axis=1))
    r = x * cos + partner * sin
    per_tile = cos_ref.shape[1] // d_rope
    for p in range(n_heads // per_tile):
        rt = r[:, p * per_tile * d_rope:(p + 1) * per_tile * d_rope].T
        for i in range(per_tile):
            qcat_scr[p * per_tile + i, rank:rank + d_rope, :] = rt[i * d_rope:(i + 1) * d_rope, :].astype(bf)


def _dsa_kernel(iq_ref, ik_ref, iw_ref, qn_ref, qr_ref, cos_ref, sin_ref, wukt_ref, kv_ref, ckvt_ref, wuvt_ref, o_ref,
                key_scr, bias_scr, s_scr, acc_scr, q_ref, *, topk, idx_scale, scale):
    _dsa_queries(qn_ref, qr_ref, cos_ref, sin_ref, wukt_ref, q_ref)
    qb = pl.program_id(1)
    nk = ((qb + 1) * TQ + TK - 1) // TK
    n_ik = nk * (TK // IK)
    tpos = qb * TQ + lax.broadcasted_iota(jnp.int32, (1, TQ), 1)
    row = lax.broadcasted_iota(jnp.int32, (IK, 1), 0)
    w_rows = iw_ref[...] * idx_scale
    n_idx_heads = iq_ref.shape[0]

    def index_chunk(c, carry):
        k0 = pl.multiple_of(c * IK, IK)
        kblk = ik_ref[pl.ds(k0, IK), :]
        acc = jnp.zeros((IK, TQ), jnp.float32)
        for h in range(n_idx_heads):
            lg = jnp.dot(kblk, iq_ref[h], preferred_element_type=jnp.float32)
            acc = acc + w_rows[h:h + 1, :] * jnp.maximum(lg, 0.0)
        acc = acc + 0.0
        bits = lax.bitcast_convert_type(acc, jnp.int32)
        key = bits ^ ((bits >> 31) & 0x7FFFFFFF)
        key_scr[pl.ds(k0, IK), :] = jnp.where(k0 + row <= tpos, key, INT_MIN)
        return carry

    lax.fori_loop(0, n_ik, index_chunk, 0)

    def count(pred_fn):
        def body(c, cnt8):
            k0 = pl.multiple_of(c * IK, IK)
            hit = jnp.where(pred_fn(key_scr[pl.ds(k0, IK), :]), 1, 0).astype(jnp.int32)
            return cnt8 + _fold8(hit, jnp.sum)
        cnt8 = lax.fori_loop(0, n_ik, body, jnp.zeros((SUBLANES, TQ), jnp.int32))
        return jnp.sum(cnt8, axis=0, keepdims=True)

    thr = jnp.where(count(lambda kk: kk >= 0) >= topk, 0, INT_MIN).astype(jnp.int32) + jnp.zeros((1, TQ), jnp.int32)

    def bit_step(i, thr):
        cand = thr + jnp.left_shift(jnp.int32(1), 30 - i)
        return jnp.where(count(lambda kk: kk >= cand) >= topk, cand, thr)

    thr = lax.fori_loop(0, 31, bit_step, thr)
    thr = jnp.maximum(thr, INT_MIN + 1)
    need = (topk - count(lambda kk: kk > thr)).astype(jnp.float32)

    tri = jnp.where(lax.broadcasted_iota(jnp.int32, (IK, IK), 1) <= lax.broadcasted_iota(jnp.int32, (IK, IK), 0),
                    1.0, 0.0).astype(jnp.bfloat16)

    def bias_chunk(c, run):
        k0 = pl.multiple_of(c * IK, IK)
        kk = key_scr[pl.ds(k0, IK), :]
        eq = jnp.where(kk == thr, 1.0, 0.0)
        pref = jnp.dot(tri, eq.astype(jnp.bfloat16), preferred_element_type=jnp.float32) + run
        tie_ok = jnp.where(pref <= need, 0.0, NEG_INF)
        bias_scr[pl.ds(k0, IK), :] = jnp.where(kk > thr, 0.0, jnp.where(kk == thr, tie_ok, NEG_INF))
        return pref[IK - 1:IK, :]

    lax.fori_loop(0, n_ik, bias_chunk, jnp.zeros((1, TQ), jnp.float32))

    def emit(h, o_lat):
        d_v = wuvt_ref.shape[1]
        acc_scr[pl.ds(0, d_v), :] = jnp.dot(wuvt_ref[h], o_lat.astype(jnp.bfloat16),
                                            preferred_element_type=jnp.float32)
        o_ref[h] = acc_scr[pl.ds(0, d_v), :].T.astype(o_ref.dtype)

    _attend_heads(kv_ref, ckvt_ref, q_ref, q_ref.shape[0],
                  lambda c: bias_scr[pl.ds(pl.multiple_of(c * TK, TK), TK), :], 0, nk, s_scr, acc_scr, scale, emit)


def _dsa_attention_t(idx_qt, idx_k, idx_wt, proj, cos_t, sin_s, w_ukt, kv_lat, ckvt, w_uvt):
    bsz, n_ih, d_i, seq = idx_qt.shape
    n_h, _, d_nope = w_ukt.shape
    d_qk = kv_lat.shape[-1]
    d_v, rank = w_uvt.shape[1:]
    w_nope, w_rope = n_h * d_nope, n_h * (d_qk - rank)
    assert w_nope % w_rope == 0
    topk = min(IDX_TOPK, seq // 4)
    kern = functools.partial(_dsa_kernel, topk=topk, idx_scale=(IDX_DIM ** -0.5) * (IDX_HEADS ** -0.5),
                             scale=(A_NOPE + A_ROPE) ** -0.5)
    return pl.pallas_call(
        kern,
        grid=(bsz, seq // TQ),
        in_specs=[
            pl.BlockSpec((None, n_ih, d_i, TQ), lambda b, q: (b, 0, 0, q)),
            pl.BlockSpec((None, seq, d_i), lambda b, q: (b, 0, 0)),
            pl.BlockSpec((None, n_ih, TQ), lambda b, q: (b, 0, q)),
            pl.BlockSpec((None, TQ, w_nope), lambda b, q: (b, q, 0)),
            pl.BlockSpec((None, TQ, w_rope), lambda b, q: (b, q, w_nope // w_rope)),
            pl.BlockSpec((TQ, cos_t.shape[1]), lambda b, q: (q, 0)),
            pl.BlockSpec((TQ, sin_s.shape[1]), lambda b, q: (q, 0)),
            pl.BlockSpec((n_h, rank, d_nope), lambda b, q: (0, 0, 0)),
            pl.BlockSpec((None, seq, d_qk), lambda b, q: (b, 0, 0)),
            pl.BlockSpec((None, seq // TK, rank, TK), lambda b, q: (b, 0, 0, 0)),
            pl.BlockSpec((n_h, d_v, rank), lambda b, q: (0, 0, 0)),
        ],
        out_specs=pl.BlockSpec((None, n_h, TQ, d_v), lambda b, q: (b, 0, q, 0)),
        out_shape=jax.ShapeDtypeStruct((bsz, n_h, seq, d_v), jnp.bfloat16),
        scratch_shapes=[pltpu.VMEM((seq, TQ), jnp.int32), pltpu.VMEM((seq, TQ), jnp.float32),
                        pltpu.VMEM((2, seq, TQ), jnp.float32), pltpu.VMEM((rank, TQ), jnp.float32),
                        pltpu.VMEM((n_h, d_qk, TQ), jnp.bfloat16)],
        compiler_params=pltpu.CompilerParams(dimension_semantics=("parallel", "arbitrary"),
                                             vmem_limit_bytes=VMEM_LIMIT_BYTES),
        name="dsa_attention",
    )(idx_qt, idx_k, idx_wt, proj, proj, cos_t, sin_s, w_ukt, kv_lat, ckvt, w_uvt)


def _nsa_kernel(q_ref, kc_ref, vct_ref, ovt_ref, ks_ref, vst_ref, kw_ref, vwt_ref, g_ref, o_ref,
                blk_scr, bias_scr, s_scr, acc_scr, mix_scr, *, n_sel, scale):
    qb = pl.program_id(2)
    nk = ((qb + 1) * TQ + TK - 1) // TK
    tpos = qb * TQ + lax.broadcasted_iota(jnp.int32, (1, TQ), 1)
    row = lax.broadcasted_iota(jnp.int32, (TK, 1), 0)
    n_grp, hd = q_ref.shape[0], q_ref.shape[1]
    n_cmp_pad = kc_ref.shape[0]
    n_slc = ovt_ref.shape[0]
    gate = lambda g, j: jax.nn.sigmoid(g_ref[pl.ds(3 * g + j, 1), :])
    head_rows = lambda g: pl.ds(pl.multiple_of(g * hd, hd), hd)

    c_last = lax.broadcasted_iota(jnp.int32, (n_cmp_pad, 1), 0) * CMP_STRIDE + (CMP_LEN - 1)
    cbias = jnp.where(c_last <= tpos, 0.0, NEG_INF)
    psum = jnp.zeros((n_cmp_pad, TQ), jnp.float32)
    for g in range(n_grp):
        s = jnp.dot(kc_ref[...], q_ref[g], preferred_element_type=jnp.float32) * scale + cbias
        m = jnp.max(s, axis=0, keepdims=True)
        m = jnp.where(m == NEG_INF, 0.0, m)
        e = jnp.exp(s - m)
        den = jnp.sum(e, axis=0, keepdims=True)
        p = e * (1.0 / jnp.where(den > 0, den, 1.0))
        psum = psum + p
        o_cmp = jnp.dot(vct_ref[...], p.astype(jnp.bfloat16), preferred_element_type=jnp.float32)
        mix_scr[pl.ds(g * hd, hd), :] = gate(g, 0) * o_cmp

    hi = psum.astype(jnp.bfloat16)
    r1 = psum - hi.astype(jnp.float32)
    mid = r1.astype(jnp.bfloat16)
    lo = (r1 - mid.astype(jnp.float32)).astype(jnp.bfloat16)
    ovt = ovt_ref[...]
    imp = (jnp.dot(ovt, hi, preferred_element_type=jnp.float32)
           + jnp.dot(ovt, mid, preferred_element_type=jnp.float32)
           + jnp.dot(ovt, lo, preferred_element_type=jnp.float32))
    jidx = lax.broadcasted_iota(jnp.int32, (n_slc, 1), 0)
    t_blk = tpos // SLC_LEN
    forced = (jidx == 0) | (jidx == t_blk) | (jidx == t_blk - 1)
    imp = jnp.where(forced, jnp.inf, imp)
    imp = jnp.where(jidx * SLC_LEN <= tpos, imp, NEG_INF)
    rank = jnp.zeros((n_slc, TQ), jnp.int32)
    for jp in range(n_slc):
        r = imp[jp:jp + 1, :]
        rank = rank + jnp.where(r > imp, 1, jnp.where(r == imp, jnp.where(jp < jidx, 1, 0), 0))
    blk_scr[...] = jnp.where(rank < n_sel, 0.0, NEG_INF)

    blocks_per_chunk = TK // SLC_LEN

    def slc_bias_chunk(c, carry):
        k0 = pl.multiple_of(c * TK, TK)
        rows = [jnp.broadcast_to(blk_scr[pl.ds(c * blocks_per_chunk + i, 1), :], (SLC_LEN, TQ))
                for i in range(blocks_per_chunk)]
        bias_scr[pl.ds(k0, TK), :] = jnp.where(k0 + row <= tpos, jnp.concatenate(rows, axis=0), NEG_INF)
        return carry

    lax.fori_loop(0, nk, slc_bias_chunk, 0)

    def slc_bias(c):
        return bias_scr[pl.ds(pl.multiple_of(c * TK, TK), TK), :]

    def win_bias(c):
        spos = c * TK + row
        return jnp.where(spos <= tpos, jnp.where(spos > tpos - WIN_LEN, 0.0, NEG_INF), NEG_INF)

    w_lo = jnp.maximum(qb * TQ - WIN_LEN + 1, 0) // TK

    def add_branch(j):
        def emit(g, o):
            mix_scr[head_rows(g), :] += gate(g, j) * o
        return emit

    _attend_heads(ks_ref, vst_ref, q_ref, n_grp, slc_bias, 0, nk, s_scr, acc_scr, scale, add_branch(1))
    _attend_heads(kw_ref, vwt_ref, q_ref, n_grp, win_bias, w_lo, nk, s_scr, acc_scr, scale, add_branch(2))
    for g in range(n_grp):
        o_ref[g] = mix_scr[pl.ds(g * hd, hd), :].T.astype(o_ref.dtype)


def _nsa_attention_t(qt, kc, vct, ovt, ks, vst, kw, vwt, gates_t):
    bsz, kvh, grp, hd, seq = qt.shape
    ncp = kc.shape[2]
    n_slc = ovt.shape[0]
    kern = functools.partial(_nsa_kernel, n_sel=min(SLC_COUNT, n_slc), scale=hd ** -0.5)
    kv_spec = pl.BlockSpec((None, None, seq, hd), lambda b, h, q: (b, h, 0, 0))
    vt_spec = pl.BlockSpec((None, None, seq // TK, hd, TK), lambda b, h, q: (b, h, 0, 0, 0))
    return pl.pallas_call(
        kern,
        grid=(bsz, kvh, seq // TQ),
        in_specs=[
            pl.BlockSpec((None, None, grp, hd, TQ), lambda b, h, q: (b, h, 0, 0, q)),
            pl.BlockSpec((None, None, ncp, hd), lambda b, h, q: (b, h, 0, 0)),
            pl.BlockSpec((None, None, hd, ncp), lambda b, h, q: (b, h, 0, 0)),
            pl.BlockSpec((n_slc, ncp), lambda b, h, q: (0, 0)),
            kv_spec, vt_spec, kv_spec, vt_spec,
            pl.BlockSpec((None, None, 3 * grp, TQ), lambda b, h, q: (b, h, 0, q)),
        ],
        out_specs=pl.BlockSpec((None, grp, TQ, hd), lambda b, h, q: (b, h, q, 0)),
        out_shape=jax.ShapeDtypeStruct((bsz, kvh * grp, seq, hd), jnp.bfloat16),
        scratch_shapes=[pltpu.VMEM((n_slc, TQ), jnp.float32), pltpu.VMEM((seq, TQ), jnp.float32),
                        pltpu.VMEM((2, seq, TQ), jnp.float32), pltpu.VMEM((hd, TQ), jnp.float32),
                        pltpu.VMEM((grp * hd, TQ), jnp.float32)],
        compiler_params=pltpu.CompilerParams(dimension_semantics=("parallel", "parallel", "arbitrary"),
                                             vmem_limit_bytes=VMEM_LIMIT_BYTES),
        name="nsa_attention",
    )(qt, kc, vct, ovt, ks, vst, kw, vwt, gates_t)


HG_C = 64
HG_SB = 8
HG_CT = 1024


def _cumsum_rows(x):
    n = x.shape[0]
    row = lax.broadcasted_iota(jnp.int32, (n, 1), 0)
    s = 1
    while s < n:
        x = x + jnp.where(row >= s, pltpu.roll(x, s, axis=0), 0.0)
        s *= 2
    return x


def _bcast_rows(ref, first, period, n):
    return jnp.concatenate(
        [jnp.broadcast_to(ref[pl.ds(first + p * period, 1), :], (period, ref.shape[1])) for p in range(n // period)],
        axis=0)


def _hgrn_kernel(q_ref, f_ref, i_ref, g_ref, lb_ref, gn_ref, o_ref, st_scr, k_scr, b_scr, v_scr):
    @pl.when(pl.program_id(2) == 0)
    def _():
        st_scr[...] = jnp.zeros_like(st_scr)

    c_len, d = HG_C, q_ref.shape[1]
    bf = jnp.bfloat16
    lb = lb_ref[...]
    gain = gn_ref[...]
    row = lax.broadcasted_iota(jnp.int32, (c_len, 1), 0)
    col = lax.broadcasted_iota(jnp.int32, (1, c_len), 1)
    ones_b = jnp.ones((d, d), bf)
    nt = (((1,), (1,)), ((), ()))
    tn = (((0,), (0,)), ((), ()))

    def chunk(c, carry):
        sl = pl.ds(pl.multiple_of(c * c_len, c_len), c_len)
        qr, fr, v, gr = q_ref[sl, :], f_ref[sl, :], i_ref[sl, :], g_ref[sl, :]
        qv = qr * jax.nn.sigmoid(qr)
        fg = lb + (1.0 - lb) * jax.nn.sigmoid(fr)
        kk = 1.0 - fg
        b = _cumsum_rows(jnp.log(fg))
        k_scr[...] = kk
        b_scr[...] = b
        v_scr[...] = v
        v_b = v.astype(bf)

        att = jnp.zeros((c_len, c_len), jnp.float32)
        size = c_len // 2
        while size >= HG_SB:
            ref = _bcast_rows(b_scr, size - 1, 2 * size, c_len)
            is_q = ((row // size) % 2) == 1
            a_q = qv * jnp.exp(jnp.where(is_q, b - ref, NEG_INF))
            k_f = kk * jnp.exp(jnp.where(is_q, NEG_INF, ref - b))
            att_l = lax.dot_general(a_q.astype(bf), k_f.astype(bf), nt, preferred_element_type=jnp.float32)
            if 2 * size < c_len:
                att_l = jnp.where((row // (2 * size)) == (col // (2 * size)), att_l, 0.0)
            att = att + att_l
            size //= 2
        o = jnp.dot(att.astype(bf), v_b, preferred_element_type=jnp.float32)

        rmod = row % HG_SB
        for j in range(HG_SB):
            kb = _bcast_rows(k_scr, j, HG_SB, c_len)
            bb = _bcast_rows(b_scr, j, HG_SB, c_len)
            vb = _bcast_rows(v_scr, j, HG_SB, c_len)
            x = qv * kb * jnp.exp(jnp.where(rmod >= j, b - bb, NEG_INF))
            o = o + jnp.dot(x.astype(bf), ones_b, preferred_element_type=jnp.float32) * vb

        st = st_scr[...]
        o = o + lax.dot_general((qv * jnp.exp(b)).astype(bf), st.astype(bf), nt, preferred_element_type=jnp.float32)
        b_last = b[c_len - 1:c_len, :]
        k_l = kk * jnp.exp(b_last - b)
        st_scr[...] = st * jnp.exp(b_last) + lax.dot_general(v_b, k_l.astype(bf), tn,
                                                            preferred_element_type=jnp.float32)

        ms = jnp.mean(o * o, axis=-1, keepdims=True)
        o_ref[sl, :] = o * lax.rsqrt(ms + NORM_EPS) * gain * (gr * jax.nn.sigmoid(gr))
        return carry

    lax.fori_loop(0, q_ref.shape[0] // c_len, chunk, 0, unroll=4)


def _hgrn_recurrence(proj, lower_bound, out_norm):
    bsz, seq, w4 = proj.shape
    width = w4 // 4
    n_heads = width // C_HEAD_DIM
    ct = min(HG_CT, seq)
    assert seq % ct == 0 and ct % HG_C == 0
    part = lambda j: pl.BlockSpec((None, ct, C_HEAD_DIM), lambda b, h, t: (b, t, j * n_heads + h))
    vec = pl.BlockSpec((None, 1, C_HEAD_DIM), lambda b, h, t: (h, 0, 0))
    return pl.pallas_call(
        _hgrn_kernel,
        grid=(bsz, n_heads, seq // ct),
        in_specs=[part(0), part(1), part(2), part(3), vec, vec],
        out_specs=pl.BlockSpec((None, ct, C_HEAD_DIM), lambda b, h, t: (b, t, h)),
        out_shape=jax.ShapeDtypeStruct((bsz, seq, width), jnp.float32),
        scratch_shapes=[pltpu.VMEM((C_HEAD_DIM, C_HEAD_DIM), jnp.float32)]
        + [pltpu.VMEM((HG_C, C_HEAD_DIM), jnp.float32)] * 3,
        compiler_params=pltpu.CompilerParams(dimension_semantics=("parallel", "parallel", "arbitrary"),
                                             vmem_limit_bytes=VMEM_LIMIT_BYTES),
        name="hgrn2_recurrence",
    )(proj, proj, proj, proj, lower_bound.reshape(n_heads, 1, C_HEAD_DIM).astype(jnp.float32),
      out_norm.reshape(n_heads, 1, C_HEAD_DIM).astype(jnp.float32))


def rms_norm(x, g):
    xf = x.astype(jnp.float32)
    y = xf * lax.rsqrt(jnp.mean(xf * xf, axis=-1, keepdims=True) + NORM_EPS)
    return (y * g.astype(jnp.float32)).astype(x.dtype)


def rope(x, pos):
    d = x.shape[-1]
    inv = ROPE_THETA ** (-jnp.arange(0, d, 2, dtype=jnp.float32) / d)
    ang = pos.astype(jnp.float32)[:, None] * inv[None, :]
    cos = jnp.cos(ang)[:, None, :]
    sin = jnp.sin(ang)[:, None, :]
    xf = x.astype(jnp.float32)
    x1, x2 = xf[..., : d // 2], xf[..., d // 2:]
    return jnp.concatenate([x1 * cos - x2 * sin, x2 * cos + x1 * sin], axis=-1).astype(x.dtype)


def split_cols(a, widths):
    return jnp.split(a, np.cumsum(widths)[:-1].tolist(), axis=-1)


def dsa_nsa_mixer(u, w_in, kv_norm, w_uk, w_uv, cmp_pos, cmp_w1, cmp_w2, w_out):
    bsz, seq, _ = u.shape
    pos = jnp.arange(seq)
    proj = _proj(u, w_in)
    (_, _, a_ckv, a_k_rope, i_q, i_k, i_w, b_q, b_kv, b_gate) = split_cols(proj, AB_WIDTHS)
    c_kv = rms_norm(a_ckv, kv_norm)
    k_rope = rope(a_k_rope[:, :, None, :], pos)[:, :, 0]
    kv_lat = jnp.concatenate([c_kv, k_rope], axis=-1)
    idx_q = rope(i_q.reshape(bsz, seq, IDX_HEADS, IDX_DIM), pos)
    idx_k = rope(i_k[:, :, None, :], pos)[:, :, 0]
    bf = jnp.bfloat16
    inv = ROPE_THETA ** (-jnp.arange(0, A_ROPE, 2, dtype=jnp.float32) / A_ROPE)
    ang = pos.astype(jnp.float32)[:, None] * inv[None, :]
    lane_tile = 128
    cos_t = jnp.tile(jnp.cos(ang), (1, lane_tile // (A_ROPE // 2)))
    sin_s = jnp.tile(jnp.concatenate([-jnp.sin(ang), jnp.sin(ang)], axis=1), (1, lane_tile // A_ROPE))
    o_a_t = _dsa_attention_t(
        idx_q.transpose(0, 2, 3, 1).astype(bf), idx_k.astype(bf), i_w.transpose(0, 2, 1),
        proj, cos_t, sin_s, w_uk.transpose(0, 2, 1).astype(bf), kv_lat.astype(bf),
        c_kv.reshape(bsz, seq // TK, TK, A_KV_RANK).transpose(0, 1, 3, 2).astype(bf),
        w_uv.transpose(0, 2, 1).astype(bf))
    bq = rope(b_q.reshape(bsz, seq, B_HEADS, B_HEAD_DIM), pos)
    kv = b_kv.reshape(bsz, seq, 6, B_KV_HEADS, B_HEAD_DIM)
    o_b_t = nsa_attention_t(bq, rope(kv[:, :, 0], pos), kv[:, :, 1],
                            rope(kv[:, :, 2], pos), kv[:, :, 3],
                            rope(kv[:, :, 4], pos), kv[:, :, 5],
                            b_gate, cmp_pos, cmp_w1, cmp_w2)
    return _matmul_heads([o_a_t, o_b_t], w_out).reshape(bsz, seq, w_out.shape[1])


def nsa_attention_t(q, k_cmp, v_cmp, k_slc, v_slc, k_win, v_win, gates, cmp_pos, cmp_w1, cmp_w2):
    bsz, seq = q.shape[:2]
    kvh, grp, hd = B_KV_HEADS, B_GROUP, B_HEAD_DIM
    bf = jnp.bfloat16
    n_cmp = (seq - CMP_LEN) // CMP_STRIDE + 1
    n_cmp_pad = -(-n_cmp // 128) * 128
    tok = np.arange(n_cmp)[:, None] * CMP_STRIDE + np.arange(CMP_LEN)[None, :]

    def compress(a, j):
        blk = a[:, tok] + cmp_pos[j][:, None, :]
        blk = blk.transpose(0, 1, 3, 2, 4).reshape(bsz, n_cmp, kvh, CMP_LEN * hd)
        out = jax.nn.gelu(blk @ cmp_w1[j]) @ cmp_w2[j]
        return jnp.pad(out, ((0, 0), (0, n_cmp_pad - n_cmp), (0, 0), (0, 0)))

    kc = compress(k_cmp, 0).transpose(0, 2, 1, 3).astype(bf)
    vct = compress(v_cmp, 1).transpose(0, 2, 3, 1).astype(bf)
    n_slc = seq // SLC_LEN
    cmp_start = np.arange(n_cmp_pad) * CMP_STRIDE
    slc_start = np.arange(n_slc) * SLC_LEN
    ovt = ((cmp_start[None, :] < slc_start[:, None] + SLC_LEN)
           & (cmp_start[None, :] + CMP_LEN > slc_start[:, None]) & (np.arange(n_cmp_pad)[None, :] < n_cmp))
    ovt = jnp.asarray(ovt, bf)
    qt = q.reshape(bsz, seq, kvh, grp, hd).transpose(0, 2, 3, 4, 1).astype(bf)
    k_t = lambda a: a.transpose(0, 2, 1, 3).astype(bf)
    v_t = lambda a: a.reshape(bsz, seq // TK, TK, kvh, hd).transpose(0, 3, 1, 4, 2).astype(bf)
    gates_t = gates.reshape(bsz, seq, kvh, 3 * grp).transpose(0, 2, 3, 1)
    return _nsa_attention_t(qt, kc, vct, ovt, k_t(k_slc), v_t(v_slc), k_t(k_win), v_t(v_win), gates_t)


def hgrn2_mixer(u, w_in, lower_bound, out_norm, w_out):
    return _proj(_hgrn_recurrence(_proj(u, w_in), lower_bound, out_norm), w_out)


def routed_experts(xb_tok, e_id, e_w, w_gate_up, w_down, layer):
    n_as = e_id.shape[0]
    tok = jnp.arange(n_as, dtype=jnp.int32) // GROUP_TOPK
    onehot = (e_id[:, None] == jnp.arange(N_EXPERTS, dtype=e_id.dtype)[None, :]).astype(jnp.int32)
    csum = jnp.cumsum(onehot, axis=0)
    counts = csum[-1]
    rank = jnp.take_along_axis(csum, e_id[:, None], axis=1)[:, 0] - 1
    padded = (counts + MOE_BLOCK - 1) // MOE_BLOCK * MOE_BLOCK
    pad_end = jnp.cumsum(padded)
    pad_start = pad_end - padded
    dest = pad_start[e_id] + rank
    n_blk = -(-(n_as + N_EXPERTS * (MOE_BLOCK - 1)) // MOE_BLOCK)
    rows = n_blk * MOE_BLOCK
    fields = jnp.stack([tok, lax.bitcast_convert_type(e_w, jnp.int32)], axis=1)
    placed = jnp.zeros((rows, 2), jnp.int32).at[dest].set(fields)
    row_tok = placed[:, 0]
    row_w = lax.bitcast_convert_type(placed[:, 1], jnp.float32)
    blk_e = jnp.minimum(jnp.searchsorted(pad_end, jnp.arange(n_blk) * MOE_BLOCK, side='right'),
                        N_EXPERTS - 1).astype(jnp.int32)
    n_used = (pad_end[-1:] // MOE_BLOCK).astype(jnp.int32)
    assert n_blk % MOE_SPLITS == 0
    per = n_blk // MOE_SPLITS
    yb = None
    for s in range(MOE_SPLITS):
        r = slice(s * per * MOE_BLOCK, (s + 1) * per * MOE_BLOCK)
        yb = _moe_blocks(blk_e, n_used, xb_tok[row_tok[r]], w_gate_up, w_down, row_w[r, None], layer, s * per, yb)
    return [yb[dest[s::GROUP_TOPK]] for s in range(GROUP_TOPK)]


def hier_moe(xb_tok, logits, b_group, b_expert, w_gate_up, w_down, layer):
    n_tok = xb_tok.shape[0]
    g_logits = logits[:, :N_GROUPS] + b_group.astype(jnp.float32)
    g_prob = jax.nn.softmax(g_logits, axis=-1)
    grp = jnp.argmax(g_logits, axis=-1)
    g_w = jnp.take_along_axis(g_prob, grp[:, None], axis=-1)
    e_logits = logits[:, N_GROUPS:N_GROUPS + N_EXPERTS] + b_expert.astype(jnp.float32)
    e_logits = e_logits.reshape(n_tok, N_GROUPS, EXPERTS_PER_GROUP)
    e_logits = jnp.take_along_axis(e_logits, grp[:, None, None], axis=1)[:, 0]
    top_v, top_i = lax.top_k(e_logits, GROUP_TOPK)
    e_w = jax.nn.softmax(top_v, axis=-1) * g_w
    e_id = (grp[:, None] * EXPERTS_PER_GROUP + top_i).astype(jnp.int32)
    return routed_experts(xb_tok, e_id.reshape(-1), e_w.reshape(-1).astype(jnp.float32), w_gate_up, w_down, layer)


def kernel(x, mix_norm, ffn_norm, final_norm, ab_w_in, dsa_kv_norm, dsa_w_uk, dsa_w_uv, nsa_cmp_pos, nsa_cmp_w1, nsa_cmp_w2, ab_w_out, hgrn_w_in, hgrn_lb_logits, hgrn_out_norm, hgrn_w_out, moe_w_group, moe_b_group, moe_w_expert, moe_b_expert, moe_w_gate_up, moe_w_down):
    lb_p = jax.nn.softmax(hgrn_lb_logits.astype(jnp.float32), axis=0)
    lower_bounds = jnp.cumsum(lb_p, axis=0) - lb_p[0]
    bsz, seq, d = x.shape
    bf = jnp.bfloat16
    h = x.reshape(bsz * seq, d)
    pending = []
    for layer in range(DEPTH):
        j = layer // 2
        if pending:
            h, u = _add_norm(h, pending, mix_norm[layer], bf)
        else:
            (u,) = _add_norm(h, [], mix_norm[layer], bf, write_h=False)
        u = u.reshape(bsz, seq, d)
        if layer % 2 == 0:
            m = dsa_nsa_mixer(u, ab_w_in[j], dsa_kv_norm[j], dsa_w_uk[j], dsa_w_uv[j],
                              nsa_cmp_pos[j], nsa_cmp_w1[j], nsa_cmp_w2[j], ab_w_out[j])
        else:
            m = hgrn2_mixer(u, hgrn_w_in[j], lower_bounds[layer], hgrn_out_norm[j], hgrn_w_out[j])
        w_router = jnp.concatenate([moe_w_group[layer], moe_w_expert[layer]], axis=1)
        w_router = jnp.pad(w_router, ((0, 0), (0, ROUTER_PAD - w_router.shape[1])))
        h, u, logits = _add_norm(h, [m.reshape(bsz * seq, d)], ffn_norm[layer], bf, w_router=w_router)
        pending = hier_moe(u, logits, moe_b_group[layer], moe_b_expert[layer], moe_w_gate_up, moe_w_down, layer)
    (out,) = _add_norm(h, pending, final_norm, jnp.float32, write_h=False)
    return out.reshape(bsz, seq, d)
```

```python
import functools

import numpy as np
import jax
import jax.numpy as jnp
from jax import lax
from jax.experimental import pallas as pl
from jax.experimental.pallas import tpu as pltpu

D_MODEL = 4096
BATCH = 4
SEQ = 2048
DEPTH = 2

ROPE_THETA = 10000.0
NORM_EPS = 1e-6
Q_BLOCK = 64

A_HEADS = 16
A_NOPE = 128
A_ROPE = 64
A_KV_RANK = 512
A_V_DIM = 128
IDX_HEADS = 32
IDX_DIM = 64
IDX_TOPK = 256

B_HEADS = 16
B_KV_HEADS = 4
B_GROUP = B_HEADS // B_KV_HEADS
B_HEAD_DIM = 128
CMP_LEN = 32
CMP_STRIDE = 16
CMP_HIDDEN = 256
SLC_LEN = 64
SLC_COUNT = 16
WIN_LEN = 512

AB_WIDTHS = (
    A_HEADS * A_NOPE,
    A_HEADS * A_ROPE,
    A_KV_RANK,
    A_ROPE,
    IDX_HEADS * IDX_DIM,
    IDX_DIM,
    IDX_HEADS,
    B_HEADS * B_HEAD_DIM,
    6 * B_KV_HEADS * B_HEAD_DIM,
    3 * B_HEADS,
)
AB_IN = sum(AB_WIDTHS)
AB_OUT = A_HEADS * A_V_DIM + B_HEADS * B_HEAD_DIM

C_HEAD_DIM = 128
C_HEADS = D_MODEL // C_HEAD_DIM
C_WIDTH = C_HEADS * C_HEAD_DIM
C_CHUNK = 64

N_GROUPS = 8
EXPERTS_PER_GROUP = 8
N_EXPERTS = N_GROUPS * EXPERTS_PER_GROUP
GROUP_TOPK = 2
EXPERT_FF = 3 * D_MODEL // 32
MOE_BLOCK = 128

VMEM_LIMIT_BYTES = 48 * 1024 * 1024


MM_TM = 1024
MM_TN = 1024
MM_KC = 1024
MM_VMEM_LIMIT_BYTES = 56 * 1024 * 1024


def _cast_weight_block(w_ref, wb_scr):
    @pl.when(pl.program_id(1) == 0)
    def _():
        for k0 in range(0, w_ref.shape[0], MM_KC):
            wb_scr[k0:k0 + MM_KC, :] = w_ref[k0:k0 + MM_KC, :].astype(jnp.bfloat16)


def _mm_kernel(x_ref, w_ref, o_ref, wb_scr):
    _cast_weight_block(w_ref, wb_scr)
    o_ref[...] = jnp.dot(x_ref[...], wb_scr[...], preferred_element_type=jnp.float32)


def _weight_spec(kd):
    return pl.BlockSpec((kd, MM_TN), lambda j, i: (0, j), pipeline_mode=pl.Buffered(1))


def _matmul(x, w):
    m, kd = x.shape
    n = w.shape[1]
    assert m % MM_TM == 0 and kd % MM_KC == 0
    return pl.pallas_call(
        _mm_kernel,
        grid=(pl.cdiv(n, MM_TN), m // MM_TM),
        in_specs=[pl.BlockSpec((MM_TM, kd), lambda j, i: (i, 0)), _weight_spec(kd)],
        out_specs=pl.BlockSpec((MM_TM, MM_TN), lambda j, i: (i, j)),
        out_shape=jax.ShapeDtypeStruct((m, n), jnp.float32),
        scratch_shapes=[pltpu.VMEM((kd, MM_TN), jnp.bfloat16)],
        compiler_params=pltpu.CompilerParams(
            dimension_semantics=("parallel", "arbitrary"),
            vmem_limit_bytes=MM_VMEM_LIMIT_BYTES),
        name="matmul",
    )(x, w)


def _mm_heads_kernel(*refs):
    *x_refs, w_ref, o_ref, wb_scr = refs
    _cast_weight_block(w_ref, wb_scr)
    x = jnp.concatenate([x_ref[h] for x_ref in x_refs for h in range(x_ref.shape[0])], axis=1)
    o_ref[...] = jnp.dot(x, wb_scr[...], preferred_element_type=jnp.float32)


def _matmul_heads(xs, w):
    bsz, _, seq, hd = xs[0].shape
    kd, n = w.shape
    assert sum(x.shape[1] for x in xs) * hd == kd and seq % MM_TM == 0
    tiles = seq // MM_TM
    return pl.pallas_call(
        _mm_heads_kernel,
        grid=(pl.cdiv(n, MM_TN), bsz * tiles),
        in_specs=[pl.BlockSpec((None, x.shape[1], MM_TM, hd), lambda j, i: (i // tiles, 0, i % tiles, 0)) for x in xs]
        + [_weight_spec(kd)],
        out_specs=pl.BlockSpec((MM_TM, MM_TN), lambda j, i: (i, j)),
        out_shape=jax.ShapeDtypeStruct((bsz * seq, n), jnp.float32),
        scratch_shapes=[pltpu.VMEM((kd, MM_TN), jnp.bfloat16)],
        compiler_params=pltpu.CompilerParams(
            dimension_semantics=("parallel", "arbitrary"),
            vmem_limit_bytes=MM_VMEM_LIMIT_BYTES),
        name="matmul_heads",
    )(*xs, w)


def _proj(u, w):
    lead = u.shape[:-1]
    return _matmul(u.reshape(-1, u.shape[-1]).astype(jnp.bfloat16), w).reshape(*lead, w.shape[1])


MOE_KC = 1024
MOE_VMEM_LIMIT_BYTES = 58 * 1024 * 1024
MOE_SPLITS = 2


def _moe_kernel(blk_e_ref, n_used_ref, x_ref, wgu_ref, wd_ref, rw_ref, *rest, first_blk):
    del blk_e_ref
    o_ref = rest[-1]
    i = pl.program_id(0) + first_blk
    d_model, ff2 = wgu_ref.shape
    ff = ff2 // 2

    @pl.when(i < n_used_ref[0])
    def _():
        gu = jnp.zeros((x_ref.shape[0], ff2), jnp.float32)
        for k0 in range(0, d_model, MOE_KC):
            gu = gu + jnp.dot(x_ref[:, k0:k0 + MOE_KC], wgu_ref[k0:k0 + MOE_KC, :].astype(jnp.bfloat16),
                              preferred_element_type=jnp.float32)
        hid = (jax.nn.silu(gu[:, :ff]) * gu[:, ff:]).astype(jnp.bfloat16)
        rw = rw_ref[...]
        for n0 in range(0, d_model, MOE_KC):
            o_ref[:, n0:n0 + MOE_KC] = rw * jnp.dot(hid, wd_ref[:, n0:n0 + MOE_KC].astype(jnp.bfloat16),
                                                    preferred_element_type=jnp.float32)

    @pl.when(i >= n_used_ref[0])
    def _():
        o_ref[...] = jnp.zeros_like(o_ref)


def _moe_blocks(blk_e, n_used, xb, w_gate_up, w_down, row_w, layer, first_blk, prev):
    rows, d = xb.shape
    n_blk = rows // MOE_BLOCK
    ff2 = w_gate_up.shape[-1]
    in_specs = [pl.BlockSpec((MOE_BLOCK, d), lambda i, be, nu: (i, 0)),
                pl.BlockSpec((None, None, d, ff2), lambda i, be, nu: (layer, be[i + first_blk], 0, 0)),
                pl.BlockSpec((None, None, ff2 // 2, d), lambda i, be, nu: (layer, be[i + first_blk], 0, 0)),
                pl.BlockSpec((MOE_BLOCK, 1), lambda i, be, nu: (i, 0))]
    operands = [blk_e, n_used, xb, w_gate_up, w_down, row_w]
    aliases = {}
    if prev is not None:
        in_specs.append(pl.BlockSpec(memory_space=pl.ANY))
        aliases = {len(operands): 0}
        operands.append(prev)
    grid_spec = pltpu.PrefetchScalarGridSpec(
        num_scalar_prefetch=2,
        grid=(n_blk,),
        in_specs=in_specs,
        out_specs=pl.BlockSpec((MOE_BLOCK, d), lambda i, be, nu: (i + first_blk, 0)),
    )
    return pl.pallas_call(
        functools.partial(_moe_kernel, first_blk=first_blk),
        grid_spec=grid_spec,
        out_shape=jax.ShapeDtypeStruct((blk_e.shape[0] * MOE_BLOCK, d), jnp.float32),
        input_output_aliases=aliases,
        compiler_params=pltpu.CompilerParams(dimension_semantics=("arbitrary",),
                                             vmem_limit_bytes=MOE_VMEM_LIMIT_BYTES),
        name="moe_experts",
    )(*operands)


NORM_TM = 256
ROUTER_PAD = 128


def _add_norm_kernel(*refs, n_add, has_router, write_h):
    it = iter(refs)
    h_ref = next(it)
    add_refs = [next(it) for _ in range(n_add)]
    g_ref = next(it)
    wr_ref = next(it) if has_router else None
    h_out = next(it) if write_h else None
    u_ref = next(it)
    lg_ref = next(it) if has_router else None
    hv = h_ref[...]
    for a_ref in add_refs:
        hv = hv + a_ref[...]
    if write_h:
        h_out[...] = hv
    y = hv * lax.rsqrt(jnp.mean(hv * hv, axis=-1, keepdims=True) + NORM_EPS) * g_ref[...]
    u_ref[...] = y.astype(u_ref.dtype)
    if has_router:
        lg_ref[...] = jnp.dot(y, wr_ref[...], precision=lax.Precision.HIGHEST, preferred_element_type=jnp.float32)


def _add_norm(h, adds, gain, out_dtype, w_router=None, write_h=True):
    t, d = h.shape
    row = pl.BlockSpec((NORM_TM, d), lambda i: (i, 0))
    operands = [h, *adds, gain.reshape(1, d).astype(jnp.float32)]
    in_specs = [row] * (1 + len(adds)) + [pl.BlockSpec((1, d), lambda i: (0, 0))]
    out_shape, out_specs = [], []
    if w_router is not None:
        operands.append(w_router)
        in_specs.append(pl.BlockSpec(w_router.shape, lambda i: (0, 0)))
    if write_h:
        out_shape.append(jax.ShapeDtypeStruct((t, d), jnp.float32))
        out_specs.append(row)
    out_shape.append(jax.ShapeDtypeStruct((t, d), out_dtype))
    out_specs.append(row)
    if w_router is not None:
        out_shape.append(jax.ShapeDtypeStruct((t, w_router.shape[1]), jnp.float32))
        out_specs.append(pl.BlockSpec((NORM_TM, w_router.shape[1]), lambda i: (i, 0)))
    return pl.pallas_call(
        functools.partial(_add_norm_kernel, n_add=len(adds), has_router=w_router is not None, write_h=write_h),
        grid=(t // NORM_TM,),
        in_specs=in_specs,
        out_specs=out_specs,
        out_shape=out_shape,
        compiler_params=pltpu.CompilerParams(dimension_semantics=("parallel",),
                                             vmem_limit_bytes=VMEM_LIMIT_BYTES),
        name="add_norm",
    )(*operands)


TQ = 256
TK = 512
IK = 256
SUBLANES = 8
NEG_INF = float("-inf")
INT_MIN = -2 ** 31


def _fold8(x, op):
    r, c = x.shape
    return op(x.reshape(r // SUBLANES, SUBLANES, c), axis=0)


def _attend_heads(k_ref, vt_ref, q_ref, n_heads, bias_fn, c_lo, c_hi, s_scr, acc_scr, scale, emit):
    tq = q_ref.shape[-1]

    def scores(c, qt, slot, m8):
        k0 = pl.multiple_of(c * TK, TK)
        s = jnp.dot(k_ref[pl.ds(k0, TK), :], qt, preferred_element_type=jnp.float32) * scale + bias_fn(c)
        s_scr[slot, pl.ds(k0, TK), :] = s
        return jnp.maximum(m8, _fold8(s, jnp.max))

    neg = jnp.full((SUBLANES, tq), NEG_INF, jnp.float32)
    q0 = q_ref[0]
    m8_first = lax.fori_loop(c_lo, c_hi, lambda c, m8: scores(c, q0, 0, m8), neg)

    def head(h, m8):
        slot = h % 2
        m = jnp.max(m8, axis=0, keepdims=True)
        m = jnp.where(m == NEG_INF, 0.0, m)
        acc_scr[...] = jnp.zeros_like(acc_scr)
        q_next = q_ref[jnp.minimum(h + 1, n_heads - 1)]

        def body(c, carry):
            d8, m8n = carry
            k0 = pl.multiple_of(c * TK, TK)
            p = jnp.exp(s_scr[slot, pl.ds(k0, TK), :] - m)
            acc_scr[...] += jnp.dot(vt_ref[c], p.astype(jnp.bfloat16), preferred_element_type=jnp.float32)
            return d8 + _fold8(p, jnp.sum), scores(c, q_next, 1 - slot, m8n)

        d8, m8_next = lax.fori_loop(c_lo, c_hi, body, (jnp.zeros((SUBLANES, tq), jnp.float32), neg))
        den = jnp.sum(d8, axis=0, keepdims=True)
        emit(h, acc_scr[...] * (1.0 / jnp.where(den > 0, den, 1.0)))
        return m8_next

    lax.fori_loop(0, n_heads, head, m8_first)


def _dsa_queries(qn_ref, qr_ref, cos_ref, sin_ref, wukt_ref, qcat_scr):
    bf = jnp.bfloat16
    n_heads, rank, d_nope = wukt_ref.shape
    nt = (((1,), (1,)), ((), ()))
    for h in range(n_heads):
        qn_h = qn_ref[:, h * d_nope:(h + 1) * d_nope].astype(bf)
        qcat_scr[h, 0:rank, :] = lax.dot_general(wukt_ref[h], qn_h, nt, preferred_element_type=jnp.float32).astype(bf)
    x = qr_ref[...]
    width = x.shape[1]
    d_rope = width // n_heads
    half = d_rope // 2
    reps = width // cos_ref.shape[1]
    cos = jnp.concatenate([cos_ref[...]] * reps, axis=1)
    sin = jnp.concatenate([sin_ref[...]] * reps, axis=1)
    lane = lax.broadcasted_iota(jnp.int32, (1, width), 1)
    partner = jnp.where((lane % d_rope) < half, pltpu.roll(x, width - half, axis=1), pltpu.roll(x, half, axis=1))
    r = x * cos + partner * sin
    per_tile = cos_ref.shape[1] // d_rope
    for p in range(n_heads // per_tile):
        rt = r[:, p * per_tile * d_rope:(p + 1) * per_tile * d_rope].T
        for i in range(per_tile):
            qcat_scr[p * per_tile + i, rank:rank + d_rope, :] = rt[i * d_rope:(i + 1) * d_rope, :].astype(bf)


def _dsa_kernel(iq_ref, ik_ref, iw_ref, qn_ref, qr_ref, cos_ref, sin_ref, wukt_ref, kv_ref, ckvt_ref, wuvt_ref, o_ref,
                key_scr, bias_scr, s_scr, acc_scr, q_ref, *, topk, idx_scale, scale):
    _dsa_queries(qn_ref, qr_ref, cos_ref, sin_ref, wukt_ref, q_ref)
    qb = pl.program_id(1)
    nk = ((qb + 1) * TQ + TK - 1) // TK
    n_ik = nk * (TK // IK)
    tpos = qb * TQ + lax.broadcasted_iota(jnp.int32, (1, TQ), 1)
    row = lax.broadcasted_iota(jnp.int32, (IK, 1), 0)
    w_rows = iw_ref[...] * idx_scale
    n_idx_heads = iq_ref.shape[0]

    def index_chunk(c, carry):
        k0 = pl.multiple_of(c * IK, IK)
        kblk = ik_ref[pl.ds(k0, IK), :]
        acc = jnp.zeros((IK, TQ), jnp.float32)
        for h in range(n_idx_heads):
            lg = jnp.dot(kblk, iq_ref[h], preferred_element_type=jnp.float32)
            acc = acc + w_rows[h:h + 1, :] * jnp.maximum(lg, 0.0)
        acc = acc + 0.0
        bits = lax.bitcast_convert_type(acc, jnp.int32)
        key = bits ^ ((bits >> 31) & 0x7FFFFFFF)
        key_scr[pl.ds(k0, IK), :] = jnp.where(k0 + row <= tpos, key, INT_MIN)
        return carry

    lax.fori_loop(0, n_ik, index_chunk, 0)

    def count(pred_fn):
        def body(c, cnt8):
            k0 = pl.multiple_of(c * IK, IK)
            hit = jnp.where(pred_fn(key_scr[pl.ds(k0, IK), :]), 1, 0).astype(jnp.int32)
            return cnt8 + _fold8(hit, jnp.sum)
        cnt8 = lax.fori_loop(0, n_ik, body, jnp.zeros((SUBLANES, TQ), jnp.int32))
        return jnp.sum(cnt8, axis=0, keepdims=True)

    thr = jnp.where(count(lambda kk: kk >= 0) >= topk, 0, INT_MIN).astype(jnp.int32) + jnp.zeros((1, TQ), jnp.int32)

    def bit_step(i, thr):
        cand = thr + jnp.left_shift(jnp.int32(1), 30 - i)
        return jnp.where(count(lambda kk: kk >= cand) >= topk, cand, thr)

    thr = lax.fori_loop(0, 31, bit_step, thr)
    thr = jnp.maximum(thr, INT_MIN + 1)
    need = (topk - count(lambda kk: kk > thr)).astype(jnp.float32)

    tri = jnp.where(lax.broadcasted_iota(jnp.int32, (IK, IK), 1) <= lax.broadcasted_iota(jnp.int32, (IK, IK), 0),
                    1.0, 0.0).astype(jnp.bfloat16)

    def bias_chunk(c, run):
        k0 = pl.multiple_of(c * IK, IK)
        kk = key_scr[pl.ds(k0, IK), :]
        eq = jnp.where(kk == thr, 1.0, 0.0)
        pref = jnp.dot(tri, eq.astype(jnp.bfloat16), preferred_element_type=jnp.float32) + run
        tie_ok = jnp.where(pref <= need, 0.0, NEG_INF)
        bias_scr[pl.ds(k0, IK), :] = jnp.where(kk > thr, 0.0, jnp.where(kk == thr, tie_ok, NEG_INF))
        return pref[IK - 1:IK, :]

    lax.fori_loop(0, n_ik, bias_chunk, jnp.zeros((1, TQ), jnp.float32))

    def emit(h, o_lat):
        d_v = wuvt_ref.shape[1]
        acc_scr[pl.ds(0, d_v), :] = jnp.dot(wuvt_ref[h], o_lat.astype(jnp.bfloat16),
                                            preferred_element_type=jnp.float32)
        o_ref[h] = acc_scr[pl.ds(0, d_v), :].T.astype(o_ref.dtype)

    _attend_heads(kv_ref, ckvt_ref, q_ref, q_ref.shape[0],
                  lambda c: bias_scr[pl.ds(pl.multiple_of(c * TK, TK), TK), :], 0, nk, s_scr, acc_scr, scale, emit)


def _dsa_attention_t(idx_qt, idx_k, idx_wt, proj, cos_t, sin_s, w_ukt, kv_lat, ckvt, w_uvt):
    bsz, n_ih, d_i, seq = idx_qt.shape
    n_h, _, d_nope = w_ukt.shape
    d_qk = kv_lat.shape[-1]
    d_v, rank = w_uvt.shape[1:]
    w_nope, w_rope = n_h * d_nope, n_h * (d_qk - rank)
    assert w_nope % w_rope == 0
    topk = min(IDX_TOPK, seq // 4)
    kern = functools.partial(_dsa_kernel, topk=topk, idx_scale=(IDX_DIM ** -0.5) * (IDX_HEADS ** -0.5),
                             scale=(A_NOPE + A_ROPE) ** -0.5)
    return pl.pallas_call(
        kern,
        grid=(bsz, seq // TQ),
        in_specs=[
            pl.BlockSpec((None, n_ih, d_i, TQ), lambda b, q: (b, 0, 0, q)),
            pl.BlockSpec((None, seq, d_i), lambda b, q: (b, 0, 0)),
            pl.BlockSpec((None, n_ih, TQ), lambda b, q: (b, 0, q)),
            pl.BlockSpec((None, TQ, w_nope), lambda b, q: (b, q, 0)),
            pl.BlockSpec((None, TQ, w_rope), lambda b, q: (b, q, w_nope // w_rope)),
            pl.BlockSpec((TQ, cos_t.shape[1]), lambda b, q: (q, 0)),
            pl.BlockSpec((TQ, sin_s.shape[1]), lambda b, q: (q, 0)),
            pl.BlockSpec((n_h, rank, d_nope), lambda b, q: (0, 0, 0)),
            pl.BlockSpec((None, seq, d_qk), lambda b, q: (b, 0, 0)),
            pl.BlockSpec((None, seq // TK, rank, TK), lambda b, q: (b, 0, 0, 0)),
            pl.BlockSpec((n_h, d_v, rank), lambda b, q: (0, 0, 0)),
        ],
        out_specs=pl.BlockSpec((None, n_h, TQ, d_v), lambda b, q: (b, 0, q, 0)),
        out_shape=jax.ShapeDtypeStruct((bsz, n_h, seq, d_v), jnp.bfloat16),
        scratch_shapes=[pltpu.VMEM((seq, TQ), jnp.int32), pltpu.VMEM((seq, TQ), jnp.float32),
                        pltpu.VMEM((2, seq, TQ), jnp.float32), pltpu.VMEM((rank, TQ), jnp.float32),
                        pltpu.VMEM((n_h, d_qk, TQ), jnp.bfloat16)],
        compiler_params=pltpu.CompilerParams(dimension_semantics=("parallel", "arbitrary"),
                                             vmem_limit_bytes=VMEM_LIMIT_BYTES),
        name="dsa_attention",
    )(idx_qt, idx_k, idx_wt, proj, proj, cos_t, sin_s, w_ukt, kv_lat, ckvt, w_uvt)


def _nsa_kernel(q_ref, kc_ref, vct_ref, ovt_ref, ks_ref, vst_ref, kw_ref, vwt_ref, g_ref, o_ref,
                blk_scr, bias_scr, s_scr, acc_scr, mix_scr, *, n_sel, scale):
    qb = pl.program_id(2)
    nk = ((qb + 1) * TQ + TK - 1) // TK
    tpos = qb * TQ + lax.broadcasted_iota(jnp.int32, (1, TQ), 1)
    row = lax.broadcasted_iota(jnp.int32, (TK, 1), 0)
    n_grp, hd = q_ref.shape[0], q_ref.shape[1]
    n_cmp_pad = kc_ref.shape[0]
    n_slc = ovt_ref.shape[0]
    gate = lambda g, j: jax.nn.sigmoid(g_ref[pl.ds(3 * g + j, 1), :])
    head_rows = lambda g: pl.ds(pl.multiple_of(g * hd, hd), hd)

    c_last = lax.broadcasted_iota(jnp.int32, (n_cmp_pad, 1), 0) * CMP_STRIDE + (CMP_LEN - 1)
    cbias = jnp.where(c_last <= tpos, 0.0, NEG_INF)
    psum = jnp.zeros((n_cmp_pad, TQ), jnp.float32)
    for g in range(n_grp):
        s = jnp.dot(kc_ref[...], q_ref[g], preferred_element_type=jnp.float32) * scale + cbias
        m = jnp.max(s, axis=0, keepdims=True)
        m = jnp.where(m == NEG_INF, 0.0, m)
        e = jnp.exp(s - m)
        den = jnp.sum(e, axis=0, keepdims=True)
        p = e * (1.0 / jnp.where(den > 0, den, 1.0))
        psum = psum + p
        o_cmp = jnp.dot(vct_ref[...], p.astype(jnp.bfloat16), preferred_element_type=jnp.float32)
        mix_scr[pl.ds(g * hd, hd), :] = gate(g, 0) * o_cmp

    hi = psum.astype(jnp.bfloat16)
    r1 = psum - hi.astype(jnp.float32)
    mid = r1.astype(jnp.bfloat16)
    lo = (r1 - mid.astype(jnp.float32)).astype(jnp.bfloat16)
    ovt = ovt_ref[...]
    imp = (jnp.dot(ovt, hi, preferred_element_type=jnp.float32)
           + jnp.dot(ovt, mid, preferred_element_type=jnp.float32)
           + jnp.dot(ovt, lo, preferred_element_type=jnp.float32))
    jidx = lax.broadcasted_iota(jnp.int32, (n_slc, 1), 0)
    t_blk = tpos // SLC_LEN
    forced = (jidx == 0) | (jidx == t_blk) | (jidx == t_blk - 1)
    imp = jnp.where(forced, jnp.inf, imp)
    imp = jnp.where(jidx * SLC_LEN <= tpos, imp, NEG_INF)
    rank = jnp.zeros((n_slc, TQ), jnp.int32)
    for jp in range(n_slc):
        r = imp[jp:jp + 1, :]
        rank = rank + jnp.where(r > imp, 1, jnp.where(r == imp, jnp.where(jp < jidx, 1, 0), 0))
    blk_scr[...] = jnp.where(rank < n_sel, 0.0, NEG_INF)

    blocks_per_chunk = TK // SLC_LEN

    def slc_bias_chunk(c, carry):
        k0 = pl.multiple_of(c * TK, TK)
        rows = [jnp.broadcast_to(blk_scr[pl.ds(c * blocks_per_chunk + i, 1), :], (SLC_LEN, TQ))
                for i in range(blocks_per_chunk)]
        bias_scr[pl.ds(k0, TK), :] = jnp.where(k0 + row <= tpos, jnp.concatenate(rows, axis=0), NEG_INF)
        return carry

    lax.fori_loop(0, nk, slc_bias_chunk, 0)

    def slc_bias(c):
        return bias_scr[pl.ds(pl.multiple_of(c * TK, TK), TK), :]

    def win_bias(c):
        spos = c * TK + row
        return jnp.where(spos <= tpos, jnp.where(spos > tpos - WIN_LEN, 0.0, NEG_INF), NEG_INF)

    w_lo = jnp.maximum(qb * TQ - WIN_LEN + 1, 0) // TK

    def add_branch(j):
        def emit(g, o):
            mix_scr[head_rows(g), :] += gate(g, j) * o
        return emit

    _attend_heads(ks_ref, vst_ref, q_ref, n_grp, slc_bias, 0, nk, s_scr, acc_scr, scale, add_branch(1))
    _attend_heads(kw_ref, vwt_ref, q_ref, n_grp, win_bias, w_lo, nk, s_scr, acc_scr, scale, add_branch(2))
    for g in range(n_grp):
        o_ref[g] = mix_scr[pl.ds(g * hd, hd), :].T.astype(o_ref.dtype)


def _nsa_attention_t(qt, kc, vct, ovt, ks, vst, kw, vwt, gates_t):
    bsz, kvh, grp, hd, seq = qt.shape
    ncp = kc.shape[2]
    n_slc = ovt.shape[0]
    kern = functools.partial(_nsa_kernel, n_sel=min(SLC_COUNT, n_slc), scale=hd ** -0.5)
    kv_spec = pl.BlockSpec((None, None, seq, hd), lambda b, h, q: (b, h, 0, 0))
    vt_spec = pl.BlockSpec((None, None, seq // TK, hd, TK), lambda b, h, q: (b, h, 0, 0, 0))
    return pl.pallas_call(
        kern,
        grid=(bsz, kvh, seq // TQ),
        in_specs=[
            pl.BlockSpec((None, None, grp, hd, TQ), lambda b, h, q: (b, h, 0, 0, q)),
            pl.BlockSpec((None, None, ncp, hd), lambda b, h, q: (b, h, 0, 0)),
            pl.BlockSpec((None, None, hd, ncp), lambda b, h, q: (b, h, 0, 0)),
            pl.BlockSpec((n_slc, ncp), lambda b, h, q: (0, 0)),
            kv_spec, vt_spec, kv_spec, vt_spec,
            pl.BlockSpec((None, None, 3 * grp, TQ), lambda b, h, q: (b, h, 0, q)),
        ],
        out_specs=pl.BlockSpec((None, grp, TQ, hd), lambda b, h, q: (b, h, q, 0)),
        out_shape=jax.ShapeDtypeStruct((bsz, kvh * grp, seq, hd), jnp.bfloat16),
        scratch_shapes=[pltpu.VMEM((n_slc, TQ), jnp.float32), pltpu.VMEM((seq, TQ), jnp.float32),
                        pltpu.VMEM((2, seq, TQ), jnp.float32), pltpu.VMEM((hd, TQ), jnp.float32),
                        pltpu.VMEM((grp * hd, TQ), jnp.float32)],
        compiler_params=pltpu.CompilerParams(dimension_semantics=("parallel", "parallel", "arbitrary"),
                                             vmem_limit_bytes=VMEM_LIMIT_BYTES),
        name="nsa_attention",
    )(qt, kc, vct, ovt, ks, vst, kw, vwt, gates_t)


HG_C = 64
HG_SB = 8
HG_CT = 1024


def _cumsum_rows(x):
    n = x.shape[0]
    row = lax.broadcasted_iota(jnp.int32, (n, 1), 0)
    s = 1
    while s < n:
        x = x + jnp.where(row >= s, pltpu.roll(x, s, axis=0), 0.0)
        s *= 2
    return x


def _bcast_rows(ref, first, period, n):
    return jnp.concatenate(
        [jnp.broadcast_to(ref[pl.ds(first + p * period, 1), :], (period, ref.shape[1])) for p in range(n // period)],
        axis=0)


def _hgrn_kernel(q_ref, f_ref, i_ref, g_ref, lb_ref, gn_ref, o_ref, st_scr, k_scr, b_scr, v_scr):
    @pl.when(pl.program_id(2) == 0)
    def _():
        st_scr[...] = jnp.zeros_like(st_scr)

    c_len, d = HG_C, q_ref.shape[1]
    bf = jnp.bfloat16
    lb = lb_ref[...]
    gain = gn_ref[...]
    row = lax.broadcasted_iota(jnp.int32, (c_len, 1), 0)
    col = lax.broadcasted_iota(jnp.int32, (1, c_len), 1)
    ones_b = jnp.ones((d, d), bf)
    nt = (((1,), (1,)), ((), ()))
    tn = (((0,), (0,)), ((), ()))

    def chunk(c, carry):
        sl = pl.ds(pl.multiple_of(c * c_len, c_len), c_len)
        qr, fr, v, gr = q_ref[sl, :], f_ref[sl, :], i_ref[sl, :], g_ref[sl, :]
        qv = qr * jax.nn.sigmoid(qr)
        fg = lb + (1.0 - lb) * jax.nn.sigmoid(fr)
        kk = 1.0 - fg
        b = _cumsum_rows(jnp.log(fg))
        k_scr[...] = kk
        b_scr[...] = b
        v_scr[...] = v
        v_b = v.astype(bf)

        att = jnp.zeros((c_len, c_len), jnp.float32)
        size = c_len // 2
        while size >= HG_SB:
            ref = _bcast_rows(b_scr, size - 1, 2 * size, c_len)
            is_q = ((row // size) % 2) == 1
            a_q = qv * jnp.exp(jnp.where(is_q, b - ref, NEG_INF))
            k_f = kk * jnp.exp(jnp.where(is_q, NEG_INF, ref - b))
            att_l = lax.dot_general(a_q.astype(bf), k_f.astype(bf), nt, preferred_element_type=jnp.float32)
            if 2 * size < c_len:
                att_l = jnp.where((row // (2 * size)) == (col // (2 * size)), att_l, 0.0)
            att = att + att_l
            size //= 2
        o = jnp.dot(att.astype(bf), v_b, preferred_element_type=jnp.float32)

        rmod = row % HG_SB
        for j in range(HG_SB):
            kb = _bcast_rows(k_scr, j, HG_SB, c_len)
            bb = _bcast_rows(b_scr, j, HG_SB, c_len)
            vb = _bcast_rows(v_scr, j, HG_SB, c_len)
            x = qv * kb * jnp.exp(jnp.where(rmod >= j, b - bb, NEG_INF))
            o = o + jnp.dot(x.astype(bf), ones_b, preferred_element_type=jnp.float32) * vb

        st = st_scr[...]
        o = o + lax.dot_general((qv * jnp.exp(b)).astype(bf), st.astype(bf), nt, preferred_element_type=jnp.float32)
        b_last = b[c_len - 1:c_len, :]
        k_l = kk * jnp.exp(b_last - b)
        st_scr[...] = st * jnp.exp(b_last) + lax.dot_general(v_b, k_l.astype(bf), tn,
                                                            preferred_element_type=jnp.float32)

        ms = jnp.mean(o * o, axis=-1, keepdims=True)
        o_ref[sl, :] = o * lax.rsqrt(ms + NORM_EPS) * gain * (gr * jax.nn.sigmoid(gr))
        return carry

    lax.fori_loop(0, q_ref.shape[0] // c_len, chunk, 0, unroll=4)


def _hgrn_recurrence(proj, lower_bound, out_norm):
    bsz, seq, w4 = proj.shape
    width = w4 // 4
    n_heads = width // C_HEAD_DIM
    ct = min(HG_CT, seq)
    assert seq % ct == 0 and ct % HG_C == 0
    part = lambda j: pl.BlockSpec((None, ct, C_HEAD_DIM), lambda b, h, t: (b, t, j * n_heads + h))
    vec = pl.BlockSpec((None, 1, C_HEAD_DIM), lambda b, h, t: (h, 0, 0))
    return pl.pallas_call(
        _hgrn_kernel,
        grid=(bsz, n_heads, seq // ct),
        in_specs=[part(0), part(1), part(2), part(3), vec, vec],
        out_specs=pl.BlockSpec((None, ct, C_HEAD_DIM), lambda b, h, t: (b, t, h)),
        out_shape=jax.ShapeDtypeStruct((bsz, seq, width), jnp.float32),
        scratch_shapes=[pltpu.VMEM((C_HEAD_DIM, C_HEAD_DIM), jnp.float32)]
        + [pltpu.VMEM((HG_C, C_HEAD_DIM), jnp.float32)] * 3,
        compiler_params=pltpu.CompilerParams(dimension_semantics=("parallel", "parallel", "arbitrary"),
                                             vmem_limit_bytes=VMEM_LIMIT_BYTES),
        name="hgrn2_recurrence",
    )(proj, proj, proj, proj, lower_bound.reshape(n_heads, 1, C_HEAD_DIM).astype(jnp.float32),
      out_norm.reshape(n_heads, 1, C_HEAD_DIM).astype(jnp.float32))


def rms_norm(x, g):
    xf = x.astype(jnp.float32)
    y = xf * lax.rsqrt(jnp.mean(xf * xf, axis=-1, keepdims=True) + NORM_EPS)
    return (y * g.astype(jnp.float32)).astype(x.dtype)


def rope(x, pos):
    d = x.shape[-1]
    inv = ROPE_THETA ** (-jnp.arange(0, d, 2, dtype=jnp.float32) / d)
    ang = pos.astype(jnp.float32)[:, None] * inv[None, :]
    cos = jnp.cos(ang)[:, None, :]
    sin = jnp.sin(ang)[:, None, :]
    xf = x.astype(jnp.float32)
    x1, x2 = xf[..., : d // 2], xf[..., d // 2:]
    return jnp.concatenate([x1 * cos - x2 * sin, x2 * cos + x1 * sin], axis=-1).astype(x.dtype)


def split_cols(a, widths):
    return jnp.split(a, np.cumsum(widths)[:-1].tolist(), axis=-1)


def dsa_nsa_mixer(u, w_in, kv_norm, w_uk, w_uv, cmp_pos, cmp_w1, cmp_w2, w_out):
    bsz, seq, _ = u.shape
    pos = jnp.arange(seq)
    proj = _proj(u, w_in)
    (_, _, a_ckv, a_k_rope, i_q, i_k, i_w, b_q, b_kv, b_gate) = split_cols(proj, AB_WIDTHS)
    c_kv = rms_norm(a_ckv, kv_norm)
    k_rope = rope(a_k_rope[:, :, None, :], pos)[:, :, 0]
    kv_lat = jnp.concatenate([c_kv, k_rope], axis=-1)
    idx_q = rope(i_q.reshape(bsz, seq, IDX_HEADS, IDX_DIM), pos)
    idx_k = rope(i_k[:, :, None, :], pos)[:, :, 0]
    bf = jnp.bfloat16
    inv = ROPE_THETA ** (-jnp.arange(0, A_ROPE, 2, dtype=jnp.float32) / A_ROPE)
    ang = pos.astype(jnp.float32)[:, None] * inv[None, :]
    lane_tile = 128
    cos_t = jnp.tile(jnp.cos(ang), (1, lane_tile // (A_ROPE // 2)))
    sin_s = jnp.tile(jnp.concatenate([-jnp.sin(ang), jnp.sin(ang)], axis=1), (1, lane_tile // A_ROPE))
    o_a_t = _dsa_attention_t(
        idx_q.transpose(0, 2, 3, 1).astype(bf), idx_k.astype(bf), i_w.transpose(0, 2, 1),
        proj, cos_t, sin_s, w_uk.transpose(0, 2, 1).astype(bf), kv_lat.astype(bf),
        c_kv.reshape(bsz, seq // TK, TK, A_KV_RANK).transpose(0, 1, 3, 2).astype(bf),
        w_uv.transpose(0, 2, 1).astype(bf))
    bq = rope(b_q.reshape(bsz, seq, B_HEADS, B_HEAD_DIM), pos)
    kv = b_kv.reshape(bsz, seq, 6, B_KV_HEADS, B_HEAD_DIM)
    o_b_t = nsa_attention_t(bq, rope(kv[:, :, 0], pos), kv[:, :, 1],
                            rope(kv[:, :, 2], pos), kv[:, :, 3],
                            rope(kv[:, :, 4], pos), kv[:, :, 5],
                            b_gate, cmp_pos, cmp_w1, cmp_w2)
    return _matmul_heads([o_a_t, o_b_t], w_out).reshape(bsz, seq, w_out.shape[1])


def nsa_attention_t(q, k_cmp, v_cmp, k_slc, v_slc, k_win, v_win, gates, cmp_pos, cmp_w1, cmp_w2):
    bsz, seq = q.shape[:2]
    kvh, grp, hd = B_KV_HEADS, B_GROUP, B_HEAD_DIM
    bf = jnp.bfloat16
    n_cmp = (seq - CMP_LEN) // CMP_STRIDE + 1
    n_cmp_pad = -(-n_cmp // 128) * 128
    tok = np.arange(n_cmp)[:, None] * CMP_STRIDE + np.arange(CMP_LEN)[None, :]

    def compress(a, j):
        blk = a[:, tok] + cmp_pos[j][:, None, :]
        blk = blk.transpose(0, 1, 3, 2, 4).reshape(bsz, n_cmp, kvh, CMP_LEN * hd)
        out = jax.nn.gelu(blk @ cmp_w1[j]) @ cmp_w2[j]
        return jnp.pad(out, ((0, 0), (0, n_cmp_pad - n_cmp), (0, 0), (0, 0)))

    kc = compress(k_cmp, 0).transpose(0, 2, 1, 3).astype(bf)
    vct = compress(v_cmp, 1).transpose(0, 2, 3, 1).astype(bf)
    n_slc = seq // SLC_LEN
    cmp_start = np.arange(n_cmp_pad) * CMP_STRIDE
    slc_start = np.arange(n_slc) * SLC_LEN
    ovt = ((cmp_start[None, :] < slc_start[:, None] + SLC_LEN)
           & (cmp_start[None, :] + CMP_LEN > slc_start[:, None]) & (np.arange(n_cmp_pad)[None, :] < n_cmp))
    ovt = jnp.asarray(ovt, bf)
    qt = q.reshape(bsz, seq, kvh, grp, hd).transpose(0, 2, 3, 4, 1).astype(bf)
    k_t = lambda a: a.transpose(0, 2, 1, 3).astype(bf)
    v_t = lambda a: a.reshape(bsz, seq // TK, TK, kvh, hd).transpose(0, 3, 1, 4, 2).astype(bf)
    gates_t = gates.reshape(bsz, seq, kvh, 3 * grp).transpose(0, 2, 3, 1)
    return _nsa_attention_t(qt, kc, vct, ovt, k_t(k_slc), v_t(v_slc), k_t(k_win), v_t(v_win), gates_t)


def hgrn2_mixer(u, w_in, lower_bound, out_norm, w_out):
    return _proj(_hgrn_recurrence(_proj(u, w_in), lower_bound, out_norm), w_out)


def routed_experts(xb_tok, e_id, e_w, w_gate_up, w_down, layer):
    n_as = e_id.shape[0]
    tok = jnp.arange(n_as, dtype=jnp.int32) // GROUP_TOPK
    onehot = (e_id[:, None] == jnp.arange(N_EXPERTS, dtype=e_id.dtype)[None, :]).astype(jnp.int32)
    csum = jnp.cumsum(onehot, axis=0)
    counts = csum[-1]
    rank = jnp.take_along_axis(csum, e_id[:, None], axis=1)[:, 0] - 1
    padded = (counts + MOE_BLOCK - 1) // MOE_BLOCK * MOE_BLOCK
    pad_end = jnp.cumsum(padded)
    pad_start = pad_end - padded
    dest = pad_start[e_id] + rank
    n_blk = -(-(n_as + N_EXPERTS * (MOE_BLOCK - 1)) // MOE_BLOCK)
    rows = n_blk * MOE_BLOCK
    fields = jnp.stack([tok, lax.bitcast_convert_type(e_w, jnp.int32)], axis=1)
    placed = jnp.zeros((rows, 2), jnp.int32).at[dest].set(fields)
    row_tok = placed[:, 0]
    row_w = lax.bitcast_convert_type(placed[:, 1], jnp.float32)
    blk_e = jnp.minimum(jnp.searchsorted(pad_end, jnp.arange(n_blk) * MOE_BLOCK, side='right'),
                        N_EXPERTS - 1).astype(jnp.int32)
    n_used = (pad_end[-1:] // MOE_BLOCK).astype(jnp.int32)
    assert n_blk % MOE_SPLITS == 0
    per = n_blk // MOE_SPLITS
    yb = None
    for s in range(MOE_SPLITS):
        r = slice(s * per * MOE_BLOCK, (s + 1) * per * MOE_BLOCK)
        yb = _moe_blocks(blk_e, n_used, xb_tok[row_tok[r]], w_gate_up, w_down, row_w[r, None], layer, s * per, yb)
    return [yb[dest[s::GROUP_TOPK]] for s in range(GROUP_TOPK)]


def hier_moe(xb_tok, logits, b_group, b_expert, w_gate_up, w_down, layer):
    n_tok = xb_tok.shape[0]
    g_logits = logits[:, :N_GROUPS] + b_group.astype(jnp.float32)
    g_prob = jax.nn.softmax(g_logits, axis=-1)
    grp = jnp.argmax(g_logits, axis=-1)
    g_w = jnp.take_along_axis(g_prob, grp[:, None], axis=-1)
    e_logits = logits[:, N_GROUPS:N_GROUPS + N_EXPERTS] + b_expert.astype(jnp.float32)
    e_logits = e_logits.reshape(n_tok, N_GROUPS, EXPERTS_PER_GROUP)
    e_logits = jnp.take_along_axis(e_logits, grp[:, None, None], axis=1)[:, 0]
    top_v, top_i = lax.top_k(e_logits, GROUP_TOPK)
    e_w = jax.nn.softmax(top_v, axis=-1) * g_w
    e_id = (grp[:, None] * EXPERTS_PER_GROUP + top_i).astype(jnp.int32)
    return routed_experts(xb_tok, e_id.reshape(-1), e_w.reshape(-1).astype(jnp.float32), w_gate_up, w_down, layer)


def kernel(x, mix_norm, ffn_norm, final_norm, ab_w_in, dsa_kv_norm, dsa_w_uk, dsa_w_uv, nsa_cmp_pos, nsa_cmp_w1, nsa_cmp_w2, ab_w_out, hgrn_w_in, hgrn_lb_logits, hgrn_out_norm, hgrn_w_out, moe_w_group, moe_b_group, moe_w_expert, moe_b_expert, moe_w_gate_up, moe_w_down):
    lb_p = jax.nn.softmax(hgrn_lb_logits.astype(jnp.float32), axis=0)
    lower_bounds = jnp.cumsum(lb_p, axis=0) - lb_p[0]
    bsz, seq, d = x.shape
    bf = jnp.bfloat16
    h = x.reshape(bsz * seq, d)
    pending = []
    for layer in range(DEPTH):
        j = layer // 2
        if pending:
            h, u = _add_norm(h, pending, mix_norm[layer], bf)
        else:
            (u,) = _add_norm(h, [], mix_norm[layer], bf, write_h=False)
        u = u.reshape(bsz, seq, d)
        if layer % 2 == 0:
            m = dsa_nsa_mixer(u, ab_w_in[j], dsa_kv_norm[j], dsa_w_uk[j], dsa_w_uv[j],
                              nsa_cmp_pos[j], nsa_cmp_w1[j], nsa_cmp_w2[j], ab_w_out[j])
        else:
            m = hgrn2_mixer(u, hgrn_w_in[j], lower_bounds[layer], hgrn_out_norm[j], hgrn_w_out[j])
        w_router = jnp.concatenate([moe_w_group[layer], moe_w_expert[layer]], axis=1)
        w_router = jnp.pad(w_router, ((0, 0), (0, ROUTER_PAD - w_router.shape[1])))
        h, u, logits = _add_norm(h, [m.reshape(bsz * seq, d)], ffn_norm[layer], bf, w_router=w_router)
        pending = hier_moe(u, logits, moe_b_group[layer], moe_b_expert[layer], moe_w_gate_up, moe_w_down, layer)
    (out,) = _add_norm(h, pending, final_norm, jnp.float32, write_h=False)
    return out.reshape(bsz, seq, d)
```

```python
import functools

import numpy as np
import jax
import jax.numpy as jnp
from jax import lax
from jax.experimental import pallas as pl
from jax.experimental.pallas import tpu as pltpu

D_MODEL = 4096
BATCH = 4
SEQ = 2048
DEPTH = 2

ROPE_THETA = 10000.0
NORM_EPS = 1e-6
Q_BLOCK = 64

A_HEADS = 16
A_NOPE = 128
A_ROPE = 64
A_KV_RANK = 512
A_V_DIM = 128
IDX_HEADS = 32
IDX_DIM = 64
IDX_TOPK = 256

B_HEADS = 16
B_KV_HEADS = 4
B_GROUP = B_HEADS // B_KV_HEADS
B_HEAD_DIM = 128
CMP_LEN = 32
CMP_STRIDE = 16
CMP_HIDDEN = 256
SLC_LEN = 64
SLC_COUNT = 16
WIN_LEN = 512

AB_WIDTHS = (
    A_HEADS * A_NOPE,
    A_HEADS * A_ROPE,
    A_KV_RANK,
    A_ROPE,
    IDX_HEADS * IDX_DIM,
    IDX_DIM,
    IDX_HEADS,
    B_HEADS * B_HEAD_DIM,
    6 * B_KV_HEADS * B_HEAD_DIM,
    3 * B_HEADS,
)
AB_IN = sum(AB_WIDTHS)
AB_OUT = A_HEADS * A_V_DIM + B_HEADS * B_HEAD_DIM

C_HEAD_DIM = 128
C_HEADS = D_MODEL // C_HEAD_DIM
C_WIDTH = C_HEADS * C_HEAD_DIM
C_CHUNK = 64

N_GROUPS = 8
EXPERTS_PER_GROUP = 8
N_EXPERTS = N_GROUPS * EXPERTS_PER_GROUP
GROUP_TOPK = 2
EXPERT_FF = 3 * D_MODEL // 32
MOE_BLOCK = 128

VMEM_LIMIT_BYTES = 48 * 1024 * 1024


MM_TM = 1024
MM_TN = 512


def _mm_kernel(x_ref, w_ref, o_ref):
    o_ref[...] = jnp.dot(x_ref[...], w_ref[...].astype(jnp.bfloat16), preferred_element_type=jnp.float32)


def _matmul(x, w):
    m, kd = x.shape
    n = w.shape[1]
    assert m % MM_TM == 0
    return pl.pallas_call(
        _mm_kernel,
        grid=(pl.cdiv(n, MM_TN), m // MM_TM),
        in_specs=[pl.BlockSpec((MM_TM, kd), lambda j, i: (i, 0)),
                  pl.BlockSpec((kd, MM_TN), lambda j, i: (0, j))],
        out_specs=pl.BlockSpec((MM_TM, MM_TN), lambda j, i: (i, j)),
        out_shape=jax.ShapeDtypeStruct((m, n), jnp.float32),
        compiler_params=pltpu.CompilerParams(
            dimension_semantics=("parallel", "parallel"),
            vmem_limit_bytes=VMEM_LIMIT_BYTES),
        name="matmul",
    )(x, w)


def _mm_heads_kernel(*refs):
    *x_refs, w_ref, o_ref = refs
    x = jnp.concatenate([x_ref[h] for x_ref in x_refs for h in range(x_ref.shape[0])], axis=1)
    o_ref[...] = jnp.dot(x, w_ref[...].astype(jnp.bfloat16), preferred_element_type=jnp.float32)


def _matmul_heads(xs, w):
    bsz, _, seq, hd = xs[0].shape
    kd, n = w.shape
    assert sum(x.shape[1] for x in xs) * hd == kd and seq % MM_TM == 0
    tiles = seq // MM_TM
    return pl.pallas_call(
        _mm_heads_kernel,
        grid=(pl.cdiv(n, MM_TN), bsz * tiles),
        in_specs=[pl.BlockSpec((None, x.shape[1], MM_TM, hd), lambda j, i: (i // tiles, 0, i % tiles, 0)) for x in xs]
        + [pl.BlockSpec((kd, MM_TN), lambda j, i: (0, j))],
        out_specs=pl.BlockSpec((MM_TM, MM_TN), lambda j, i: (i, j)),
        out_shape=jax.ShapeDtypeStruct((bsz * seq, n), jnp.float32),
        compiler_params=pltpu.CompilerParams(
            dimension_semantics=("parallel", "parallel"),
            vmem_limit_bytes=VMEM_LIMIT_BYTES),
        name="matmul_heads",
    )(*xs, w)


def _proj(u, w):
    lead = u.shape[:-1]
    return _matmul(u.reshape(-1, u.shape[-1]).astype(jnp.bfloat16), w).reshape(*lead, w.shape[1])


MOE_KC = 1024
MOE_VMEM_LIMIT_BYTES = 58 * 1024 * 1024
MOE_SPLITS = 4


def _moe_kernel(blk_e_ref, n_used_ref, run_ref, nxt_ref, slot_ref, x_ref, wgu_hbm, wd_hbm, rw_ref, *rest,
                first_blk, layer):
    o_ref, wgu_buf, wd_buf, sem = rest[-4:]
    step = pl.program_id(0)
    i = step + first_blk
    _, d_model, ff2 = wgu_buf.shape
    ff = ff2 // 2
    used = i < n_used_ref[0]
    slot = slot_ref[i]

    def weight_copies(e, s):
        return (pltpu.make_async_copy(wgu_hbm.at[layer, e], wgu_buf.at[s], sem.at[0, s]),
                pltpu.make_async_copy(wd_hbm.at[layer, e], wd_buf.at[s], sem.at[1, s]))

    @pl.when(used & (step == 0))
    def _():
        for cp in weight_copies(blk_e_ref[i], slot):
            cp.start()

    @pl.when(used & (run_ref[i] == 1))
    def _():
        nxt = nxt_ref[i]

        @pl.when(nxt >= 0)
        def _():
            for cp in weight_copies(nxt, 1 - slot):
                cp.start()

        for cp in weight_copies(blk_e_ref[i], slot):
            cp.wait()

    @pl.when(used)
    def _():
        gu = jnp.zeros((x_ref.shape[0], ff2), jnp.float32)
        for k0 in range(0, d_model, MOE_KC):
            gu = gu + jnp.dot(x_ref[:, k0:k0 + MOE_KC], wgu_buf[slot, k0:k0 + MOE_KC, :].astype(jnp.bfloat16),
                              preferred_element_type=jnp.float32)
        hid = (jax.nn.silu(gu[:, :ff]) * gu[:, ff:]).astype(jnp.bfloat16)
        rw = rw_ref[...]
        for n0 in range(0, d_model, MOE_KC):
            o_ref[:, n0:n0 + MOE_KC] = rw * jnp.dot(hid, wd_buf[slot, :, n0:n0 + MOE_KC].astype(jnp.bfloat16),
                                                    preferred_element_type=jnp.float32)

    @pl.when(jnp.logical_not(used))
    def _():
        o_ref[...] = jnp.zeros_like(o_ref)


def _moe_runs(blk_e, n_used, per):
    n = blk_e.shape[0]
    idx = jnp.arange(n, dtype=jnp.int32)
    used = idx < n_used[0]
    prev_e = jnp.concatenate([blk_e[:1], blk_e[:-1]])
    run = used & ((idx % per == 0) | (blk_e != prev_e))
    run_i = run.astype(jnp.int32)
    runs_in_call = jnp.cumsum(run_i.reshape(n // per, per), axis=1).reshape(n)
    slot = (runs_in_call + 1) % 2
    start_e = jnp.where(run, blk_e, -1).reshape(n // per, per)

    def nxt_scan(carry, col):
        return jnp.where(col >= 0, col, carry), carry

    _, nxt_cols = lax.scan(nxt_scan, jnp.full((n // per,), -1, jnp.int32), start_e.T[::-1])
    nxt = nxt_cols[::-1].T.reshape(n)
    return run_i, nxt.astype(jnp.int32), slot.astype(jnp.int32)


def _moe_blocks(blk_e, n_used, runs, xb, w_gate_up, w_down, row_w, layer, first_blk, prev):
    rows, d = xb.shape
    n_blk = rows // MOE_BLOCK
    ff2 = w_gate_up.shape[-1]
    in_specs = [pl.BlockSpec((MOE_BLOCK, d), lambda i, *_: (i, 0)),
                pl.BlockSpec(memory_space=pl.ANY),
                pl.BlockSpec(memory_space=pl.ANY),
                pl.BlockSpec((MOE_BLOCK, 1), lambda i, *_: (i, 0))]
    operands = [blk_e, n_used, *runs, xb, w_gate_up, w_down, row_w]
    aliases = {}
    if prev is not None:
        in_specs.append(pl.BlockSpec(memory_space=pl.ANY))
        aliases = {len(operands): 0}
        operands.append(prev)
    grid_spec = pltpu.PrefetchScalarGridSpec(
        num_scalar_prefetch=2 + len(runs),
        grid=(n_blk,),
        in_specs=in_specs,
        out_specs=pl.BlockSpec((MOE_BLOCK, d), lambda i, *_: (i + first_blk, 0)),
        scratch_shapes=[pltpu.VMEM((2, d, ff2), jnp.float32), pltpu.VMEM((2, ff2 // 2, d), jnp.float32),
                        pltpu.SemaphoreType.DMA((2, 2))],
    )
    return pl.pallas_call(
        functools.partial(_moe_kernel, first_blk=first_blk, layer=layer),
        grid_spec=grid_spec,
        out_shape=jax.ShapeDtypeStruct((blk_e.shape[0] * MOE_BLOCK, d), jnp.float32),
        input_output_aliases=aliases,
        compiler_params=pltpu.CompilerParams(dimension_semantics=("arbitrary",),
                                             vmem_limit_bytes=MOE_VMEM_LIMIT_BYTES),
        name="moe_experts",
    )(*operands)


NORM_TM = 256
ROUTER_PAD = 128


def _add_norm_kernel(*refs, n_add, has_router, write_h):
    it = iter(refs)
    h_ref = next(it)
    add_refs = [next(it) for _ in range(n_add)]
    g_ref = next(it)
    wr_ref = next(it) if has_router else None
    h_out = next(it) if write_h else None
    u_ref = next(it)
    lg_ref = next(it) if has_router else None
    hv = h_ref[...]
    for a_ref in add_refs:
        hv = hv + a_ref[...]
    if write_h:
        h_out[...] = hv
    y = hv * lax.rsqrt(jnp.mean(hv * hv, axis=-1, keepdims=True) + NORM_EPS) * g_ref[...]
    u_ref[...] = y.astype(u_ref.dtype)
    if has_router:
        lg_ref[...] = jnp.dot(y, wr_ref[...], precision=lax.Precision.HIGHEST, preferred_element_type=jnp.float32)


def _add_norm(h, adds, gain, out_dtype, w_router=None, write_h=True):
    t, d = h.shape
    row = pl.BlockSpec((NORM_TM, d), lambda i: (i, 0))
    operands = [h, *adds, gain.reshape(1, d).astype(jnp.float32)]
    in_specs = [row] * (1 + len(adds)) + [pl.BlockSpec((1, d), lambda i: (0, 0))]
    out_shape, out_specs = [], []
    if w_router is not None:
        operands.append(w_router)
        in_specs.append(pl.BlockSpec(w_router.shape, lambda i: (0, 0)))
    if write_h:
        out_shape.append(jax.ShapeDtypeStruct((t, d), jnp.float32))
        out_specs.append(row)
    out_shape.append(jax.ShapeDtypeStruct((t, d), out_dtype))
    out_specs.append(row)
    if w_router is not None:
        out_shape.append(jax.ShapeDtypeStruct((t, w_router.shape[1]), jnp.float32))
        out_specs.append(pl.BlockSpec((NORM_TM, w_router.shape[1]), lambda i: (i, 0)))
    return pl.pallas_call(
        functools.partial(_add_norm_kernel, n_add=len(adds), has_router=w_router is not None, write_h=write_h),
        grid=(t // NORM_TM,),
        in_specs=in_specs,
        out_specs=out_specs,
        out_shape=out_shape,
        compiler_params=pltpu.CompilerParams(dimension_semantics=("parallel",),
                                             vmem_limit_bytes=VMEM_LIMIT_BYTES),
        name="add_norm",
    )(*operands)


TQ = 256
TK = 512
IK = 256
SUBLANES = 8
NEG_INF = float("-inf")
INT_MIN = -2 ** 31


def _fold8(x, op):
    r, c = x.shape
    return op(x.reshape(r // SUBLANES, SUBLANES, c), axis=0)


def _attend_heads(k_ref, vt_ref, q_ref, n_heads, bias_fn, c_lo, c_hi, s_scr, acc_scr, scale, emit):
    tq = q_ref.shape[-1]

    def scores(c, qt, slot, m8):
        k0 = pl.multiple_of(c * TK, TK)
        s = jnp.dot(k_ref[pl.ds(k0, TK), :], qt, preferred_element_type=jnp.float32) * scale + bias_fn(c)
        s_scr[slot, pl.ds(k0, TK), :] = s
        return jnp.maximum(m8, _fold8(s, jnp.max))

    neg = jnp.full((SUBLANES, tq), NEG_INF, jnp.float32)
    q0 = q_ref[0]
    m8_first = lax.fori_loop(c_lo, c_hi, lambda c, m8: scores(c, q0, 0, m8), neg)

    def head(h, m8):
        slot = h % 2
        m = jnp.max(m8, axis=0, keepdims=True)
        m = jnp.where(m == NEG_INF, 0.0, m)
        acc_scr[...] = jnp.zeros_like(acc_scr)
        q_next = q_ref[jnp.minimum(h + 1, n_heads - 1)]

        def body(c, carry):
            d8, m8n = carry
            k0 = pl.multiple_of(c * TK, TK)
            p = jnp.exp(s_scr[slot, pl.ds(k0, TK), :] - m)
            acc_scr[...] += jnp.dot(vt_ref[c], p.astype(jnp.bfloat16), preferred_element_type=jnp.float32)
            return d8 + _fold8(p, jnp.sum), scores(c, q_next, 1 - slot, m8n)

        d8, m8_next = lax.fori_loop(c_lo, c_hi, body, (jnp.zeros((SUBLANES, tq), jnp.float32), neg))
        den = jnp.sum(d8, axis=0, keepdims=True)
        emit(h, acc_scr[...] * (1.0 / jnp.where(den > 0, den, 1.0)))
        return m8_next

    lax.fori_loop(0, n_heads, head, m8_first)


def _dsa_queries(qn_ref, qr_ref, cos_ref, sin_ref, wukt_ref, qcat_scr):
    bf = jnp.bfloat16
    n_heads, rank, d_nope = wukt_ref.shape
    nt = (((1,), (1,)), ((), ()))
    for h in range(n_heads):
        qn_h = qn_ref[:, h * d_nope:(h + 1) * d_nope].astype(bf)
        qcat_scr[h, 0:rank, :] = lax.dot_general(wukt_ref[h], qn_h, nt, preferred_element_type=jnp.float32).astype(bf)
    x = qr_ref[...]
    width = x.shape[1]
    d_rope = width // n_heads
    half = d_rope // 2
    reps = width // cos_ref.shape[1]
    cos = jnp.concatenate([cos_ref[...]] * reps, axis=1)
    sin = jnp.concatenate([sin_ref[...]] * reps, axis=1)
    lane = lax.broadcasted_iota(jnp.int32, (1, width), 1)
    partner = jnp.where((lane % d_rope) < half, pltpu.roll(x, width - half, axis=1), pltpu.roll(x, half, axis=1))
    r = x * cos + partner * sin
    per_tile = cos_ref.shape[1] // d_rope
    for p in range(n_heads // per_tile):
        rt = r[:, p * per_tile * d_rope:(p + 1) * per_tile * d_rope].T
        for i in range(per_tile):
            qcat_scr[p * per_tile + i, rank:rank + d_rope, :] = rt[i * d_rope:(i + 1) * d_rope, :].astype(bf)


def _dsa_kernel(iq_ref, ik_ref, iw_ref, qn_ref, qr_ref, cos_ref, sin_ref, wukt_ref, kv_ref, ckvt_ref, wuvt_ref, o_ref,
                key_scr, bias_scr, s_scr, acc_scr, q_ref, *, topk, idx_scale, scale):
    _dsa_queries(qn_ref, qr_ref, cos_ref, sin_ref, wukt_ref, q_ref)
    qb = pl.program_id(1)
    nk = ((qb + 1) * TQ + TK - 1) // TK
    n_ik = nk * (TK // IK)
    tpos = qb * TQ + lax.broadcasted_iota(jnp.int32, (1, TQ), 1)
    row = lax.broadcasted_iota(jnp.int32, (IK, 1), 0)
    w_rows = iw_ref[...] * idx_scale
    n_idx_heads = iq_ref.shape[0]

    def index_chunk(c, carry):
        k0 = pl.multiple_of(c * IK, IK)
        kblk = ik_ref[pl.ds(k0, IK), :]
        acc = jnp.zeros((IK, TQ), jnp.float32)
        for h in range(n_idx_heads):
            lg = jnp.dot(kblk, iq_ref[h], preferred_element_type=jnp.float32)
            acc = acc + w_rows[h:h + 1, :] * jnp.maximum(lg, 0.0)
        acc = acc + 0.0
        bits = lax.bitcast_convert_type(acc, jnp.int32)
        key = bits ^ ((bits >> 31) & 0x7FFFFFFF)
        key_scr[pl.ds(k0, IK), :] = jnp.where(k0 + row <= tpos, key, INT_MIN)
        return carry

    lax.fori_loop(0, n_ik, index_chunk, 0)

    def count(pred_fn):
        def body(c, cnt8):
            k0 = pl.multiple_of(c * IK, IK)
            hit = jnp.where(pred_fn(key_scr[pl.ds(k0, IK), :]), 1, 0).astype(jnp.int32)
            return cnt8 + _fold8(hit, jnp.sum)
        cnt8 = lax.fori_loop(0, n_ik, body, jnp.zeros((SUBLANES, TQ), jnp.int32))
        return jnp.sum(cnt8, axis=0, keepdims=True)

    thr = jnp.where(count(lambda kk: kk >= 0) >= topk, 0, INT_MIN).astype(jnp.int32) + jnp.zeros((1, TQ), jnp.int32)

    def bit_step(i, thr):
        cand = thr + jnp.left_shift(jnp.int32(1), 30 - i)
        return jnp.where(count(lambda kk: kk >= cand) >= topk, cand, thr)

    thr = lax.fori_loop(0, 31, bit_step, thr)
    thr = jnp.maximum(thr, INT_MIN + 1)
    need = (topk - count(lambda kk: kk > thr)).astype(jnp.float32)

    tri = jnp.where(lax.broadcasted_iota(jnp.int32, (IK, IK), 1) <= lax.broadcasted_iota(jnp.int32, (IK, IK), 0),
                    1.0, 0.0).astype(jnp.bfloat16)

    def bias_chunk(c, run):
        k0 = pl.multiple_of(c * IK, IK)
        kk = key_scr[pl.ds(k0, IK), :]
        eq = jnp.where(kk == thr, 1.0, 0.0)
        pref = jnp.dot(tri, eq.astype(jnp.bfloat16), preferred_element_type=jnp.float32) + run
        tie_ok = jnp.where(pref <= need, 0.0, NEG_INF)
        bias_scr[pl.ds(k0, IK), :] = jnp.where(kk > thr, 0.0, jnp.where(kk == thr, tie_ok, NEG_INF))
        return pref[IK - 1:IK, :]

    lax.fori_loop(0, n_ik, bias_chunk, jnp.zeros((1, TQ), jnp.float32))

    def emit(h, o_lat):
        d_v = wuvt_ref.shape[1]
        acc_scr[pl.ds(0, d_v), :] = jnp.dot(wuvt_ref[h], o_lat.astype(jnp.bfloat16),
                                            preferred_element_type=jnp.float32)
        o_ref[h] = acc_scr[pl.ds(0, d_v), :].T.astype(o_ref.dtype)

    _attend_heads(kv_ref, ckvt_ref, q_ref, q_ref.shape[0],
                  lambda c: bias_scr[pl.ds(pl.multiple_of(c * TK, TK), TK), :], 0, nk, s_scr, acc_scr, scale, emit)


def _dsa_attention_t(idx_qt, idx_k, idx_wt, proj, cos_t, sin_s, w_ukt, kv_lat, ckvt, w_uvt):
    bsz, n_ih, d_i, seq = idx_qt.shape
    n_h, _, d_nope = w_ukt.shape
    d_qk = kv_lat.shape[-1]
    d_v, rank = w_uvt.shape[1:]
    w_nope, w_rope = n_h * d_nope, n_h * (d_qk - rank)
    assert w_nope % w_rope == 0
    topk = min(IDX_TOPK, seq // 4)
    kern = functools.partial(_dsa_kernel, topk=topk, idx_scale=(IDX_DIM ** -0.5) * (IDX_HEADS ** -0.5),
                             scale=(A_NOPE + A_ROPE) ** -0.5)
    return pl.pallas_call(
        kern,
        grid=(bsz, seq // TQ),
        in_specs=[
            pl.BlockSpec((None, n_ih, d_i, TQ), lambda b, q: (b, 0, 0, q)),
            pl.BlockSpec((None, seq, d_i), lambda b, q: (b, 0, 0)),
            pl.BlockSpec((None, n_ih, TQ), lambda b, q: (b, 0, q)),
            pl.BlockSpec((None, TQ, w_nope), lambda b, q: (b, q, 0)),
            pl.BlockSpec((None, TQ, w_rope), lambda b, q: (b, q, w_nope // w_rope)),
            pl.BlockSpec((TQ, cos_t.shape[1]), lambda b, q: (q, 0)),
            pl.BlockSpec((TQ, sin_s.shape[1]), lambda b, q: (q, 0)),
            pl.BlockSpec((n_h, rank, d_nope), lambda b, q: (0, 0, 0)),
            pl.BlockSpec((None, seq, d_qk), lambda b, q: (b, 0, 0)),
            pl.BlockSpec((None, seq // TK, rank, TK), lambda b, q: (b, 0, 0, 0)),
            pl.BlockSpec((n_h, d_v, rank), lambda b, q: (0, 0, 0)),
        ],
        out_specs=pl.BlockSpec((None, n_h, TQ, d_v), lambda b, q: (b, 0, q, 0)),
        out_shape=jax.ShapeDtypeStruct((bsz, n_h, seq, d_v), jnp.bfloat16),
        scratch_shapes=[pltpu.VMEM((seq, TQ), jnp.int32), pltpu.VMEM((seq, TQ), jnp.float32),
                        pltpu.VMEM((2, seq, TQ), jnp.float32), pltpu.VMEM((rank, TQ), jnp.float32),
                        pltpu.VMEM((n_h, d_qk, TQ), jnp.bfloat16)],
        compiler_params=pltpu.CompilerParams(dimension_semantics=("parallel", "arbitrary"),
                                             vmem_limit_bytes=VMEM_LIMIT_BYTES),
        name="dsa_attention",
    )(idx_qt, idx_k, idx_wt, proj, proj, cos_t, sin_s, w_ukt, kv_lat, ckvt, w_uvt)


def _nsa_kernel(q_ref, kc_ref, vct_ref, ovt_ref, ks_ref, vst_ref, kw_ref, vwt_ref, g_ref, o_ref,
                blk_scr, bias_scr, s_scr, acc_scr, mix_scr, *, n_sel, scale):
    qb = pl.program_id(2)
    nk = ((qb + 1) * TQ + TK - 1) // TK
    tpos = qb * TQ + lax.broadcasted_iota(jnp.int32, (1, TQ), 1)
    row = lax.broadcasted_iota(jnp.int32, (TK, 1), 0)
    n_grp, hd = q_ref.shape[0], q_ref.shape[1]
    n_cmp_pad = kc_ref.shape[0]
    n_slc = ovt_ref.shape[0]
    gate = lambda g, j: jax.nn.sigmoid(g_ref[pl.ds(3 * g + j, 1), :])
    head_rows = lambda g: pl.ds(pl.multiple_of(g * hd, hd), hd)

    c_last = lax.broadcasted_iota(jnp.int32, (n_cmp_pad, 1), 0) * CMP_STRIDE + (CMP_LEN - 1)
    cbias = jnp.where(c_last <= tpos, 0.0, NEG_INF)
    psum = jnp.zeros((n_cmp_pad, TQ), jnp.float32)
    for g in range(n_grp):
        s = jnp.dot(kc_ref[...], q_ref[g], preferred_element_type=jnp.float32) * scale + cbias
        m = jnp.max(s, axis=0, keepdims=True)
        m = jnp.where(m == NEG_INF, 0.0, m)
        e = jnp.exp(s - m)
        den = jnp.sum(e, axis=0, keepdims=True)
        p = e * (1.0 / jnp.where(den > 0, den, 1.0))
        psum = psum + p
        o_cmp = jnp.dot(vct_ref[...], p.astype(jnp.bfloat16), preferred_element_type=jnp.float32)
        mix_scr[pl.ds(g * hd, hd), :] = gate(g, 0) * o_cmp

    hi = psum.astype(jnp.bfloat16)
    r1 = psum - hi.astype(jnp.float32)
    mid = r1.astype(jnp.bfloat16)
    lo = (r1 - mid.astype(jnp.float32)).astype(jnp.bfloat16)
    ovt = ovt_ref[...]
    imp = (jnp.dot(ovt, hi, preferred_element_type=jnp.float32)
           + jnp.dot(ovt, mid, preferred_element_type=jnp.float32)
           + jnp.dot(ovt, lo, preferred_element_type=jnp.float32))
    jidx = lax.broadcasted_iota(jnp.int32, (n_slc, 1), 0)
    t_blk = tpos // SLC_LEN
    forced = (jidx == 0) | (jidx == t_blk) | (jidx == t_blk - 1)
    imp = jnp.where(forced, jnp.inf, imp)
    imp = jnp.where(jidx * SLC_LEN <= tpos, imp, NEG_INF)
    rank = jnp.zeros((n_slc, TQ), jnp.int32)
    for jp in range(n_slc):
        r = imp[jp:jp + 1, :]
        rank = rank + jnp.where(r > imp, 1, jnp.where(r == imp, jnp.where(jp < jidx, 1, 0), 0))
    blk_scr[...] = jnp.where(rank < n_sel, 0.0, NEG_INF)

    blocks_per_chunk = TK // SLC_LEN

    def slc_bias_chunk(c, carry):
        k0 = pl.multiple_of(c * TK, TK)
        rows = [jnp.broadcast_to(blk_scr[pl.ds(c * blocks_per_chunk + i, 1), :], (SLC_LEN, TQ))
                for i in range(blocks_per_chunk)]
        bias_scr[pl.ds(k0, TK), :] = jnp.where(k0 + row <= tpos, jnp.concatenate(rows, axis=0), NEG_INF)
        return carry

    lax.fori_loop(0, nk, slc_bias_chunk, 0)

    def slc_bias(c):
        return bias_scr[pl.ds(pl.multiple_of(c * TK, TK), TK), :]

    def win_bias(c):
        spos = c * TK + row
        return jnp.where(spos <= tpos, jnp.where(spos > tpos - WIN_LEN, 0.0, NEG_INF), NEG_INF)

    w_lo = jnp.maximum(qb * TQ - WIN_LEN + 1, 0) // TK

    def add_branch(j):
        def emit(g, o):
            mix_scr[head_rows(g), :] += gate(g, j) * o
        return emit

    _attend_heads(ks_ref, vst_ref, q_ref, n_grp, slc_bias, 0, nk, s_scr, acc_scr, scale, add_branch(1))
    _attend_heads(kw_ref, vwt_ref, q_ref, n_grp, win_bias, w_lo, nk, s_scr, acc_scr, scale, add_branch(2))
    for g in range(n_grp):
        o_ref[g] = mix_scr[pl.ds(g * hd, hd), :].T.astype(o_ref.dtype)


def _nsa_attention_t(qt, kc, vct, ovt, ks, vst, kw, vwt, gates_t):
    bsz, kvh, grp, hd, seq = qt.shape
    ncp = kc.shape[2]
    n_slc = ovt.shape[0]
    kern = functools.partial(_nsa_kernel, n_sel=min(SLC_COUNT, n_slc), scale=hd ** -0.5)
    kv_spec = pl.BlockSpec((None, None, seq, hd), lambda b, h, q: (b, h, 0, 0))
    vt_spec = pl.BlockSpec((None, None, seq // TK, hd, TK), lambda b, h, q: (b, h, 0, 0, 0))
    return pl.pallas_call(
        kern,
        grid=(bsz, kvh, seq // TQ),
        in_specs=[
            pl.BlockSpec((None, None, grp, hd, TQ), lambda b, h, q: (b, h, 0, 0, q)),
            pl.BlockSpec((None, None, ncp, hd), lambda b, h, q: (b, h, 0, 0)),
            pl.BlockSpec((None, None, hd, ncp), lambda b, h, q: (b, h, 0, 0)),
            pl.BlockSpec((n_slc, ncp), lambda b, h, q: (0, 0)),
            kv_spec, vt_spec, kv_spec, vt_spec,
            pl.BlockSpec((None, None, 3 * grp, TQ), lambda b, h, q: (b, h, 0, q)),
        ],
        out_specs=pl.BlockSpec((None, grp, TQ, hd), lambda b, h, q: (b, h, q, 0)),
        out_shape=jax.ShapeDtypeStruct((bsz, kvh * grp, seq, hd), jnp.bfloat16),
        scratch_shapes=[pltpu.VMEM((n_slc, TQ), jnp.float32), pltpu.VMEM((seq, TQ), jnp.float32),
                        pltpu.VMEM((2, seq, TQ), jnp.float32), pltpu.VMEM((hd, TQ), jnp.float32),
                        pltpu.VMEM((grp * hd, TQ), jnp.float32)],
        compiler_params=pltpu.CompilerParams(dimension_semantics=("parallel", "parallel", "arbitrary"),
                                             vmem_limit_bytes=VMEM_LIMIT_BYTES),
        name="nsa_attention",
    )(qt, kc, vct, ovt, ks, vst, kw, vwt, gates_t)


HG_C = 64
HG_SB = 8
HG_CT = 1024


def _cumsum_rows(x):
    n = x.shape[0]
    row = lax.broadcasted_iota(jnp.int32, (n, 1), 0)
    s = 1
    while s < n:
        x = x + jnp.where(row >= s, pltpu.roll(x, s, axis=0), 0.0)
        s *= 2
    return x


def _bcast_rows(ref, first, period, n):
    return jnp.concatenate(
        [jnp.broadcast_to(ref[pl.ds(first + p * period, 1), :], (period, ref.shape[1])) for p in range(n // period)],
        axis=0)


def _hgrn_kernel(q_ref, f_ref, i_ref, g_ref, lb_ref, gn_ref, o_ref, st_scr, k_scr, b_scr, v_scr):
    @pl.when(pl.program_id(2) == 0)
    def _():
        st_scr[...] = jnp.zeros_like(st_scr)

    c_len, d = HG_C, q_ref.shape[1]
    bf = jnp.bfloat16
    lb = lb_ref[...]
    gain = gn_ref[...]
    row = lax.broadcasted_iota(jnp.int32, (c_len, 1), 0)
    col = lax.broadcasted_iota(jnp.int32, (1, c_len), 1)
    ones_b = jnp.ones((d, d), bf)
    nt = (((1,), (1,)), ((), ()))
    tn = (((0,), (0,)), ((), ()))

    def chunk(c, carry):
        sl = pl.ds(pl.multiple_of(c * c_len, c_len), c_len)
        qr, fr, v, gr = q_ref[sl, :], f_ref[sl, :], i_ref[sl, :], g_ref[sl, :]
        qv = qr * jax.nn.sigmoid(qr)
        fg = lb + (1.0 - lb) * jax.nn.sigmoid(fr)
        kk = 1.0 - fg
        b = _cumsum_rows(jnp.log(fg))
        k_scr[...] = kk
        b_scr[...] = b
        v_scr[...] = v
        v_b = v.astype(bf)

        att = jnp.zeros((c_len, c_len), jnp.float32)
        size = c_len // 2
        while size >= HG_SB:
            ref = _bcast_rows(b_scr, size - 1, 2 * size, c_len)
            is_q = ((row // size) % 2) == 1
            a_q = qv * jnp.exp(jnp.where(is_q, b - ref, NEG_INF))
            k_f = kk * jnp.exp(jnp.where(is_q, NEG_INF, ref - b))
            att_l = lax.dot_general(a_q.astype(bf), k_f.astype(bf), nt, preferred_element_type=jnp.float32)
            if 2 * size < c_len:
                att_l = jnp.where((row // (2 * size)) == (col // (2 * size)), att_l, 0.0)
            att = att + att_l
            size //= 2
        o = jnp.dot(att.astype(bf), v_b, preferred_element_type=jnp.float32)

        rmod = row % HG_SB
        for j in range(HG_SB):
            kb = _bcast_rows(k_scr, j, HG_SB, c_len)
            bb = _bcast_rows(b_scr, j, HG_SB, c_len)
            vb = _bcast_rows(v_scr, j, HG_SB, c_len)
            x = qv * kb * jnp.exp(jnp.where(rmod >= j, b - bb, NEG_INF))
            o = o + jnp.dot(x.astype(bf), ones_b, preferred_element_type=jnp.float32) * vb

        st = st_scr[...]
        o = o + lax.dot_general((qv * jnp.exp(b)).astype(bf), st.astype(bf), nt, preferred_element_type=jnp.float32)
        b_last = b[c_len - 1:c_len, :]
        k_l = kk * jnp.exp(b_last - b)
        st_scr[...] = st * jnp.exp(b_last) + lax.dot_general(v_b, k_l.astype(bf), tn,
                                                            preferred_element_type=jnp.float32)

        ms = jnp.mean(o * o, axis=-1, keepdims=True)
        o_ref[sl, :] = o * lax.rsqrt(ms + NORM_EPS) * gain * (gr * jax.nn.sigmoid(gr))
        return carry

    lax.fori_loop(0, q_ref.shape[0] // c_len, chunk, 0, unroll=4)


def _hgrn_recurrence(proj, lower_bound, out_norm):
    bsz, seq, w4 = proj.shape
    width = w4 // 4
    n_heads = width // C_HEAD_DIM
    ct = min(HG_CT, seq)
    assert seq % ct == 0 and ct % HG_C == 0
    part = lambda j: pl.BlockSpec((None, ct, C_HEAD_DIM), lambda b, h, t: (b, t, j * n_heads + h))
    vec = pl.BlockSpec((None, 1, C_HEAD_DIM), lambda b, h, t: (h, 0, 0))
    return pl.pallas_call(
        _hgrn_kernel,
        grid=(bsz, n_heads, seq // ct),
        in_specs=[part(0), part(1), part(2), part(3), vec, vec],
        out_specs=pl.BlockSpec((None, ct, C_HEAD_DIM), lambda b, h, t: (b, t, h)),
        out_shape=jax.ShapeDtypeStruct((bsz, seq, width), jnp.float32),
        scratch_shapes=[pltpu.VMEM((C_HEAD_DIM, C_HEAD_DIM), jnp.float32)]
        + [pltpu.VMEM((HG_C, C_HEAD_DIM), jnp.float32)] * 3,
        compiler_params=pltpu.CompilerParams(dimension_semantics=("parallel", "parallel", "arbitrary"),
                                             vmem_limit_bytes=VMEM_LIMIT_BYTES),
        name="hgrn2_recurrence",
    )(proj, proj, proj, proj, lower_bound.reshape(n_heads, 1, C_HEAD_DIM).astype(jnp.float32),
      out_norm.reshape(n_heads, 1, C_HEAD_DIM).astype(jnp.float32))


def rms_norm(x, g):
    xf = x.astype(jnp.float32)
    y = xf * lax.rsqrt(jnp.mean(xf * xf, axis=-1, keepdims=True) + NORM_EPS)
    return (y * g.astype(jnp.float32)).astype(x.dtype)


def rope(x, pos):
    d = x.shape[-1]
    inv = ROPE_THETA ** (-jnp.arange(0, d, 2, dtype=jnp.float32) / d)
    ang = pos.astype(jnp.float32)[:, None] * inv[None, :]
    cos = jnp.cos(ang)[:, None, :]
    sin = jnp.sin(ang)[:, None, :]
    xf = x.astype(jnp.float32)
    x1, x2 = xf[..., : d // 2], xf[..., d // 2:]
    return jnp.concatenate([x1 * cos - x2 * sin, x2 * cos + x1 * sin], axis=-1).astype(x.dtype)


def split_cols(a, widths):
    return jnp.split(a, np.cumsum(widths)[:-1].tolist(), axis=-1)


def dsa_nsa_mixer(u, w_in, kv_norm, w_uk, w_uv, cmp_pos, cmp_w1, cmp_w2, w_out):
    bsz, seq, _ = u.shape
    pos = jnp.arange(seq)
    proj = _proj(u, w_in)
    (_, _, a_ckv, a_k_rope, i_q, i_k, i_w, b_q, b_kv, b_gate) = split_cols(proj, AB_WIDTHS)
    c_kv = rms_norm(a_ckv, kv_norm)
    k_rope = rope(a_k_rope[:, :, None, :], pos)[:, :, 0]
    kv_lat = jnp.concatenate([c_kv, k_rope], axis=-1)
    idx_q = rope(i_q.reshape(bsz, seq, IDX_HEADS, IDX_DIM), pos)
    idx_k = rope(i_k[:, :, None, :], pos)[:, :, 0]
    bf = jnp.bfloat16
    inv = ROPE_THETA ** (-jnp.arange(0, A_ROPE, 2, dtype=jnp.float32) / A_ROPE)
    ang = pos.astype(jnp.float32)[:, None] * inv[None, :]
    lane_tile = 128
    cos_t = jnp.tile(jnp.cos(ang), (1, lane_tile // (A_ROPE // 2)))
    sin_s = jnp.tile(jnp.concatenate([-jnp.sin(ang), jnp.sin(ang)], axis=1), (1, lane_tile // A_ROPE))
    o_a_t = _dsa_attention_t(
        idx_q.transpose(0, 2, 3, 1).astype(bf), idx_k.astype(bf), i_w.transpose(0, 2, 1),
        proj, cos_t, sin_s, w_uk.transpose(0, 2, 1).astype(bf), kv_lat.astype(bf),
        c_kv.reshape(bsz, seq // TK, TK, A_KV_RANK).transpose(0, 1, 3, 2).astype(bf),
        w_uv.transpose(0, 2, 1).astype(bf))
    bq = rope(b_q.reshape(bsz, seq, B_HEADS, B_HEAD_DIM), pos)
    kv = b_kv.reshape(bsz, seq, 6, B_KV_HEADS, B_HEAD_DIM)
    o_b_t = nsa_attention_t(bq, rope(kv[:, :, 0], pos), kv[:, :, 1],
                            rope(kv[:, :, 2], pos), kv[:, :, 3],
                            rope(kv[:, :, 4], pos), kv[:, :, 5],
                            b_gate, cmp_pos, cmp_w1, cmp_w2)
    return _matmul_heads([o_a_t, o_b_t], w_out).reshape(bsz, seq, w_out.shape[1])


def nsa_attention_t(q, k_cmp, v_cmp, k_slc, v_slc, k_win, v_win, gates, cmp_pos, cmp_w1, cmp_w2):
    bsz, seq = q.shape[:2]
    kvh, grp, hd = B_KV_HEADS, B_GROUP, B_HEAD_DIM
    bf = jnp.bfloat16
    n_cmp = (seq - CMP_LEN) // CMP_STRIDE + 1
    n_cmp_pad = -(-n_cmp // 128) * 128
    tok = np.arange(n_cmp)[:, None] * CMP_STRIDE + np.arange(CMP_LEN)[None, :]

    def compress(a, j):
        blk = a[:, tok] + cmp_pos[j][:, None, :]
        blk = blk.transpose(0, 1, 3, 2, 4).reshape(bsz, n_cmp, kvh, CMP_LEN * hd)
        out = jax.nn.gelu(blk @ cmp_w1[j]) @ cmp_w2[j]
        return jnp.pad(out, ((0, 0), (0, n_cmp_pad - n_cmp), (0, 0), (0, 0)))

    kc = compress(k_cmp, 0).transpose(0, 2, 1, 3).astype(bf)
    vct = compress(v_cmp, 1).transpose(0, 2, 3, 1).astype(bf)
    n_slc = seq // SLC_LEN
    cmp_start = np.arange(n_cmp_pad) * CMP_STRIDE
    slc_start = np.arange(n_slc) * SLC_LEN
    ovt = ((cmp_start[None, :] < slc_start[:, None] + SLC_LEN)
           & (cmp_start[None, :] + CMP_LEN > slc_start[:, None]) & (np.arange(n_cmp_pad)[None, :] < n_cmp))
    ovt = jnp.asarray(ovt, bf)
    qt = q.reshape(bsz, seq, kvh, grp, hd).transpose(0, 2, 3, 4, 1).astype(bf)
    k_t = lambda a: a.transpose(0, 2, 1, 3).astype(bf)
    v_t = lambda a: a.reshape(bsz, seq // TK, TK, kvh, hd).transpose(0, 3, 1, 4, 2).astype(bf)
    gates_t = gates.reshape(bsz, seq, kvh, 3 * grp).transpose(0, 2, 3, 1)
    return _nsa_attention_t(qt, kc, vct, ovt, k_t(k_slc), v_t(v_slc), k_t(k_win), v_t(v_win), gates_t)


def hgrn2_mixer(u, w_in, lower_bound, out_norm, w_out):
    return _proj(_hgrn_recurrence(_proj(u, w_in), lower_bound, out_norm), w_out)


def routed_experts(xb_tok, e_id, e_w, w_gate_up, w_down, layer):
    n_as = e_id.shape[0]
    tok = jnp.arange(n_as, dtype=jnp.int32) // GROUP_TOPK
    onehot = (e_id[:, None] == jnp.arange(N_EXPERTS, dtype=e_id.dtype)[None, :]).astype(jnp.int32)
    csum = jnp.cumsum(onehot, axis=0)
    counts = csum[-1]
    rank = jnp.take_along_axis(csum, e_id[:, None], axis=1)[:, 0] - 1
    padded = (counts + MOE_BLOCK - 1) // MOE_BLOCK * MOE_BLOCK
    pad_end = jnp.cumsum(padded)
    pad_start = pad_end - padded
    dest = pad_start[e_id] + rank
    n_blk = -(-(n_as + N_EXPERTS * (MOE_BLOCK - 1)) // MOE_BLOCK)
    rows = n_blk * MOE_BLOCK
    fields = jnp.stack([tok, lax.bitcast_convert_type(e_w, jnp.int32)], axis=1)
    placed = jnp.zeros((rows, 2), jnp.int32).at[dest].set(fields)
    row_tok = placed[:, 0]
    row_w = lax.bitcast_convert_type(placed[:, 1], jnp.float32)
    blk_e = jnp.minimum(jnp.searchsorted(pad_end, jnp.arange(n_blk) * MOE_BLOCK, side='right'),
                        N_EXPERTS - 1).astype(jnp.int32)
    n_used = (pad_end[-1:] // MOE_BLOCK).astype(jnp.int32)
    assert n_blk % MOE_SPLITS == 0
    per = n_blk // MOE_SPLITS
    runs = _moe_runs(blk_e, n_used, per)
    yb = None
    for s in range(MOE_SPLITS):
        r = slice(s * per * MOE_BLOCK, (s + 1) * per * MOE_BLOCK)
        yb = _moe_blocks(blk_e, n_used, runs, xb_tok[row_tok[r]], w_gate_up, w_down, row_w[r, None], layer,
                         s * per, yb)
    return [yb[dest[s::GROUP_TOPK]] for s in range(GROUP_TOPK)]


def hier_moe(xb_tok, logits, b_group, b_expert, w_gate_up, w_down, layer):
    n_tok = xb_tok.shape[0]
    g_logits = logits[:, :N_GROUPS] + b_group.astype(jnp.float32)
    g_prob = jax.nn.softmax(g_logits, axis=-1)
    grp = jnp.argmax(g_logits, axis=-1)
    g_w = jnp.take_along_axis(g_prob, grp[:, None], axis=-1)
    e_logits = logits[:, N_GROUPS:N_GROUPS + N_EXPERTS] + b_expert.astype(jnp.float32)
    e_logits = e_logits.reshape(n_tok, N_GROUPS, EXPERTS_PER_GROUP)
    e_logits = jnp.take_along_axis(e_logits, grp[:, None, None], axis=1)[:, 0]
    top_v, top_i = lax.top_k(e_logits, GROUP_TOPK)
    e_w = jax.nn.softmax(top_v, axis=-1) * g_w
    e_id = (grp[:, None] * EXPERTS_PER_GROUP + top_i).astype(jnp.int32)
    return routed_experts(xb_tok, e_id.reshape(-1), e_w.reshape(-1).astype(jnp.float32), w_gate_up, w_down, layer)


def kernel(x, mix_norm, ffn_norm, final_norm, ab_w_in, dsa_kv_norm, dsa_w_uk, dsa_w_uv, nsa_cmp_pos, nsa_cmp_w1, nsa_cmp_w2, ab_w_out, hgrn_w_in, hgrn_lb_logits, hgrn_out_norm, hgrn_w_out, moe_w_group, moe_b_group, moe_w_expert, moe_b_expert, moe_w_gate_up, moe_w_down):
    lb_p = jax.nn.softmax(hgrn_lb_logits.astype(jnp.float32), axis=0)
    lower_bounds = jnp.cumsum(lb_p, axis=0) - lb_p[0]
    bsz, seq, d = x.shape
    bf = jnp.bfloat16
    h = x.reshape(bsz * seq, d)
    pending = []
    for layer in range(DEPTH):
        j = layer // 2
        if pending:
            h, u = _add_norm(h, pending, mix_norm[layer], bf)
        else:
            (u,) = _add_norm(h, [], mix_norm[layer], bf, write_h=False)
        u = u.reshape(bsz, seq, d)
        if layer % 2 == 0:
            m = dsa_nsa_mixer(u, ab_w_in[j], dsa_kv_norm[j], dsa_w_uk[j], dsa_w_uv[j],
                              nsa_cmp_pos[j], nsa_cmp_w1[j], nsa_cmp_w2[j], ab_w_out[j])
        else:
            m = hgrn2_mixer(u, hgrn_w_in[j], lower_bounds[layer], hgrn_out_norm[j], hgrn_w_out[j])
        w_router = jnp.concatenate([moe_w_group[layer], moe_w_expert[layer]], axis=1)
        w_router = jnp.pad(w_router, ((0, 0), (0, ROUTER_PAD - w_router.shape[1])))
        h, u, logits = _add_norm(h, [m.reshape(bsz * seq, d)], ffn_norm[layer], bf, w_router=w_router)
        pending = hier_moe(u, logits, moe_b_group[layer], moe_b_expert[layer], moe_w_gate_up, moe_w_down, layer)
    (out,) = _add_norm(h, pending, final_norm, jnp.float32, write_h=False)
    return out.reshape(bsz, seq, d)
```

```python
import functools

import numpy as np
import jax
import jax.numpy as jnp
from jax import lax
from jax.experimental import pallas as pl
from jax.experimental.pallas import tpu as pltpu

D_MODEL = 4096
DEPTH = 2

ROPE_THETA = 10000.0
NORM_EPS = 1e-6

A_HEADS = 16
A_NOPE = 128
A_ROPE = 64
A_KV_RANK = 512
A_V_DIM = 128
IDX_HEADS = 32
IDX_DIM = 64
IDX_TOPK = 256

B_HEADS = 16
B_KV_HEADS = 4
B_GROUP = B_HEADS // B_KV_HEADS
B_HEAD_DIM = 128
CMP_LEN = 32
CMP_STRIDE = 16
CMP_HIDDEN = 256
SLC_LEN = 64
SLC_COUNT = 16
WIN_LEN = 512

AB_WIDTHS = (
    A_HEADS * A_NOPE,
    A_HEADS * A_ROPE,
    A_KV_RANK,
    A_ROPE,
    IDX_HEADS * IDX_DIM,
    IDX_DIM,
    IDX_HEADS,
    B_HEADS * B_HEAD_DIM,
    6 * B_KV_HEADS * B_HEAD_DIM,
    3 * B_HEADS,
)

C_HEAD_DIM = 128

N_GROUPS = 8
EXPERTS_PER_GROUP = 8
N_EXPERTS = N_GROUPS * EXPERTS_PER_GROUP
GROUP_TOPK = 2
EXPERT_FF = 3 * D_MODEL // 32
MOE_BLOCK = 128

VMEM_LIMIT_BYTES = 48 * 1024 * 1024


MM_TM = 1024
MM_TN = 512


def _mm_kernel(x_ref, w_ref, o_ref):
    o_ref[...] = jnp.dot(x_ref[...], w_ref[...].astype(jnp.bfloat16), preferred_element_type=jnp.float32)


def _matmul(x, w):
    m, kd = x.shape
    n = w.shape[1]
    assert m % MM_TM == 0
    return pl.pallas_call(
        _mm_kernel,
        grid=(pl.cdiv(n, MM_TN), m // MM_TM),
        in_specs=[pl.BlockSpec((MM_TM, kd), lambda j, i: (i, 0)),
                  pl.BlockSpec((kd, MM_TN), lambda j, i: (0, j))],
        out_specs=pl.BlockSpec((MM_TM, MM_TN), lambda j, i: (i, j)),
        out_shape=jax.ShapeDtypeStruct((m, n), jnp.float32),
        compiler_params=pltpu.CompilerParams(
            dimension_semantics=("parallel", "parallel"),
            vmem_limit_bytes=VMEM_LIMIT_BYTES),
        name="matmul",
    )(x, w)


def _mm_heads_kernel(*refs):
    *x_refs, w_ref, o_ref = refs
    x = jnp.concatenate([x_ref[h] for x_ref in x_refs for h in range(x_ref.shape[0])], axis=1)
    o_ref[...] = jnp.dot(x, w_ref[...].astype(jnp.bfloat16), preferred_element_type=jnp.float32)


def _matmul_heads(xs, w):
    bsz, _, seq, hd = xs[0].shape
    kd, n = w.shape
    assert sum(x.shape[1] for x in xs) * hd == kd and seq % MM_TM == 0
    tiles = seq // MM_TM
    return pl.pallas_call(
        _mm_heads_kernel,
        grid=(pl.cdiv(n, MM_TN), bsz * tiles),
        in_specs=[pl.BlockSpec((None, x.shape[1], MM_TM, hd), lambda j, i: (i // tiles, 0, i % tiles, 0)) for x in xs]
        + [pl.BlockSpec((kd, MM_TN), lambda j, i: (0, j))],
        out_specs=pl.BlockSpec((MM_TM, MM_TN), lambda j, i: (i, j)),
        out_shape=jax.ShapeDtypeStruct((bsz * seq, n), jnp.float32),
        compiler_params=pltpu.CompilerParams(
            dimension_semantics=("parallel", "parallel"),
            vmem_limit_bytes=VMEM_LIMIT_BYTES),
        name="matmul_heads",
    )(*xs, w)


def _proj(u, w):
    lead = u.shape[:-1]
    return _matmul(u.reshape(-1, u.shape[-1]).astype(jnp.bfloat16), w).reshape(*lead, w.shape[1])


MOE_KC = 1024
MOE_VMEM_LIMIT_BYTES = 58 * 1024 * 1024
MOE_SPLITS = 4


def _moe_kernel(blk_e_ref, n_used_ref, run_ref, nxt_ref, slot_ref, x_ref, wgu_hbm, wd_hbm, rw_ref, *rest,
                first_blk, layer):
    o_ref, wgu_buf, wd_buf, sem = rest[-4:]
    step = pl.program_id(0)
    i = step + first_blk
    _, d_model, ff2 = wgu_buf.shape
    ff = ff2 // 2
    used = i < n_used_ref[0]
    slot = slot_ref[i]

    def weight_copies(e, s):
        return (pltpu.make_async_copy(wgu_hbm.at[layer, e], wgu_buf.at[s], sem.at[0, s]),
                pltpu.make_async_copy(wd_hbm.at[layer, e], wd_buf.at[s], sem.at[1, s]))

    @pl.when(used & (step == 0))
    def _():
        for cp in weight_copies(blk_e_ref[i], slot):
            cp.start()

    @pl.when(used & (run_ref[i] == 1))
    def _():
        nxt = nxt_ref[i]

        @pl.when(nxt >= 0)
        def _():
            for cp in weight_copies(nxt, 1 - slot):
                cp.start()

        for cp in weight_copies(blk_e_ref[i], slot):
            cp.wait()

    @pl.when(used)
    def _():
        gu = jnp.zeros((x_ref.shape[0], ff2), jnp.float32)
        for k0 in range(0, d_model, MOE_KC):
            gu = gu + jnp.dot(x_ref[:, k0:k0 + MOE_KC], wgu_buf[slot, k0:k0 + MOE_KC, :].astype(jnp.bfloat16),
                              preferred_element_type=jnp.float32)
        hid = (jax.nn.silu(gu[:, :ff]) * gu[:, ff:]).astype(jnp.bfloat16)
        rw = rw_ref[...]
        for n0 in range(0, d_model, MOE_KC):
            y = rw * jnp.dot(hid, wd_buf[slot, :, n0:n0 + MOE_KC].astype(jnp.bfloat16),
                             preferred_element_type=jnp.float32)
            o_ref[:, n0:n0 + MOE_KC] = y.astype(o_ref.dtype)

    @pl.when(jnp.logical_not(used))
    def _():
        o_ref[...] = jnp.zeros_like(o_ref)


def _moe_runs(blk_e, n_used, per):
    n = blk_e.shape[0]
    idx = jnp.arange(n, dtype=jnp.int32)
    used = idx < n_used[0]
    prev_e = jnp.concatenate([blk_e[:1], blk_e[:-1]])
    run = used & ((idx % per == 0) | (blk_e != prev_e))
    run_i = run.astype(jnp.int32)
    runs_in_call = jnp.cumsum(run_i.reshape(n // per, per), axis=1).reshape(n)
    slot = (runs_in_call + 1) % 2
    start_e = jnp.where(run, blk_e, -1).reshape(n // per, per)

    def nxt_scan(carry, col):
        return jnp.where(col >= 0, col, carry), carry

    _, nxt_cols = lax.scan(nxt_scan, jnp.full((n // per,), -1, jnp.int32), start_e.T[::-1])
    nxt = nxt_cols[::-1].T.reshape(n)
    return run_i, nxt.astype(jnp.int32), slot.astype(jnp.int32)


def _moe_blocks(blk_e, n_used, runs, xb, w_gate_up, w_down, row_w, layer, first_blk, prev):
    rows, d = xb.shape
    n_blk = rows // MOE_BLOCK
    ff2 = w_gate_up.shape[-1]
    in_specs = [pl.BlockSpec((MOE_BLOCK, d), lambda i, *_: (i, 0)),
                pl.BlockSpec(memory_space=pl.ANY),
                pl.BlockSpec(memory_space=pl.ANY),
                pl.BlockSpec((MOE_BLOCK, 1), lambda i, *_: (i, 0))]
    operands = [blk_e, n_used, *runs, xb, w_gate_up, w_down, row_w]
    aliases = {}
    if prev is not None:
        in_specs.append(pl.BlockSpec(memory_space=pl.ANY))
        aliases = {len(operands): 0}
        operands.append(prev)
    grid_spec = pltpu.PrefetchScalarGridSpec(
        num_scalar_prefetch=2 + len(runs),
        grid=(n_blk,),
        in_specs=in_specs,
        out_specs=pl.BlockSpec((MOE_BLOCK, d), lambda i, *_: (i + first_blk, 0)),
        scratch_shapes=[pltpu.VMEM((2, d, ff2), jnp.float32), pltpu.VMEM((2, ff2 // 2, d), jnp.float32),
                        pltpu.SemaphoreType.DMA((2, 2))],
    )
    return pl.pallas_call(
        functools.partial(_moe_kernel, first_blk=first_blk, layer=layer),
        grid_spec=grid_spec,
        out_shape=jax.ShapeDtypeStruct((blk_e.shape[0] * MOE_BLOCK, d), jnp.bfloat16),
        input_output_aliases=aliases,
        compiler_params=pltpu.CompilerParams(dimension_semantics=("arbitrary",),
                                             vmem_limit_bytes=MOE_VMEM_LIMIT_BYTES),
        name="moe_experts",
    )(*operands)


NORM_TM = 256
ROUTER_PAD = 128


def _add_norm_kernel(*refs, n_add, has_router, write_h):
    it = iter(refs)
    h_ref = next(it)
    add_refs = [next(it) for _ in range(n_add)]
    g_ref = next(it)
    wr_ref = next(it) if has_router else None
    h_out = next(it) if write_h else None
    u_ref = next(it)
    lg_ref = next(it) if has_router else None
    hv = h_ref[...]
    for a_ref in add_refs:
        hv = hv + a_ref[...].astype(jnp.float32)
    if write_h:
        h_out[...] = hv
    y = hv * lax.rsqrt(jnp.mean(hv * hv, axis=-1, keepdims=True) + NORM_EPS) * g_ref[...]
    u_ref[...] = y.astype(u_ref.dtype)
    if has_router:
        lg_ref[...] = jnp.dot(y, wr_ref[...], precision=lax.Precision.HIGHEST, preferred_element_type=jnp.float32)


def _add_norm(h, adds, gain, out_dtype, w_router=None, write_h=True):
    t, d = h.shape
    row = pl.BlockSpec((NORM_TM, d), lambda i: (i, 0))
    operands = [h, *adds, gain.reshape(1, d).astype(jnp.float32)]
    in_specs = [row] * (1 + len(adds)) + [pl.BlockSpec((1, d), lambda i: (0, 0))]
    out_shape, out_specs = [], []
    if w_router is not None:
        operands.append(w_router)
        in_specs.append(pl.BlockSpec(w_router.shape, lambda i: (0, 0)))
    if write_h:
        out_shape.append(jax.ShapeDtypeStruct((t, d), jnp.float32))
        out_specs.append(row)
    out_shape.append(jax.ShapeDtypeStruct((t, d), out_dtype))
    out_specs.append(row)
    if w_router is not None:
        out_shape.append(jax.ShapeDtypeStruct((t, w_router.shape[1]), jnp.float32))
        out_specs.append(pl.BlockSpec((NORM_TM, w_router.shape[1]), lambda i: (i, 0)))
    return pl.pallas_call(
        functools.partial(_add_norm_kernel, n_add=len(adds), has_router=w_router is not None, write_h=write_h),
        grid=(t // NORM_TM,),
        in_specs=in_specs,
        out_specs=out_specs,
        out_shape=out_shape,
        compiler_params=pltpu.CompilerParams(dimension_semantics=("parallel",),
                                             vmem_limit_bytes=VMEM_LIMIT_BYTES),
        name="add_norm",
    )(*operands)


TQ = 256
TK = 512
IK = 256
SUBLANES = 8
NEG_INF = float("-inf")
INT_MIN = -2 ** 31


def _fold8(x, op):
    r, c = x.shape
    return op(x.reshape(r // SUBLANES, SUBLANES, c), axis=0)


def _attend_heads(k_ref, vt_ref, q_ref, n_heads, bias_fn, c_lo, c_hi, s_scr, acc_scr, scale, emit):
    tq = q_ref.shape[-1]

    def scores(c, qt, slot, m8):
        k0 = pl.multiple_of(c * TK, TK)
        s = jnp.dot(k_ref[pl.ds(k0, TK), :], qt, preferred_element_type=jnp.float32) * scale + bias_fn(c)
        s_scr[slot, pl.ds(k0, TK), :] = s
        return jnp.maximum(m8, _fold8(s, jnp.max))

    neg = jnp.full((SUBLANES, tq), NEG_INF, jnp.float32)
    q0 = q_ref[0]
    m8_first = lax.fori_loop(c_lo, c_hi, lambda c, m8: scores(c, q0, 0, m8), neg)

    def head(h, m8):
        slot = h % 2
        m = jnp.max(m8, axis=0, keepdims=True)
        m = jnp.where(m == NEG_INF, 0.0, m)
        acc_scr[...] = jnp.zeros_like(acc_scr)
        q_next = q_ref[jnp.minimum(h + 1, n_heads - 1)]

        def body(c, carry):
            d8, m8n = carry
            k0 = pl.multiple_of(c * TK, TK)
            p = jnp.exp(s_scr[slot, pl.ds(k0, TK), :] - m)
            acc_scr[...] += jnp.dot(vt_ref[c], p.astype(jnp.bfloat16), preferred_element_type=jnp.float32)
            return d8 + _fold8(p, jnp.sum), scores(c, q_next, 1 - slot, m8n)

        d8, m8_next = lax.fori_loop(c_lo, c_hi, body, (jnp.zeros((SUBLANES, tq), jnp.float32), neg))
        den = jnp.sum(d8, axis=0, keepdims=True)
        emit(h, acc_scr[...] * (1.0 / jnp.where(den > 0, den, 1.0)))
        return m8_next

    lax.fori_loop(0, n_heads, head, m8_first)


def _dsa_queries(qn_ref, qr_ref, cos_ref, sin_ref, wukt_ref, qcat_scr):
    bf = jnp.bfloat16
    n_heads, rank, d_nope = wukt_ref.shape
    nt = (((1,), (1,)), ((), ()))
    for h in range(n_heads):
        qn_h = qn_ref[:, h * d_nope:(h + 1) * d_nope].astype(bf)
        qcat_scr[h, 0:rank, :] = lax.dot_general(wukt_ref[h], qn_h, nt, preferred_element_type=jnp.float32).astype(bf)
    x = qr_ref[...]
    width = x.shape[1]
    d_rope = width // n_heads
    half = d_rope // 2
    reps = width // cos_ref.shape[1]
    cos = jnp.concatenate([cos_ref[...]] * reps, axis=1)
    sin = jnp.concatenate([sin_ref[...]] * reps, axis=1)
    lane = lax.broadcasted_iota(jnp.int32, (1, width), 1)
    partner = jnp.where((lane % d_rope) < half, pltpu.roll(x, width - half, axis=1), pltpu.roll(x, half, axis=1))
    r = x * cos + partner * sin
    per_tile = cos_ref.shape[1] // d_rope
    for p in range(n_heads // per_tile):
        rt = r[:, p * per_tile * d_rope:(p + 1) * per_tile * d_rope].T
        for i in range(per_tile):
            qcat_scr[p * per_tile + i, rank:rank + d_rope, :] = rt[i * d_rope:(i + 1) * d_rope, :].astype(bf)


def _dsa_kernel(iq_ref, ik_ref, iw_ref, qn_ref, qr_ref, cos_ref, sin_ref, wukt_ref, kv_ref, ckvt_ref, wuvt_ref, o_ref,
                key_scr, bias_scr, s_scr, acc_scr, q_ref, *, topk, idx_scale, scale):
    _dsa_queries(qn_ref, qr_ref, cos_ref, sin_ref, wukt_ref, q_ref)
    qb = pl.program_id(1)
    nk = ((qb + 1) * TQ + TK - 1) // TK
    n_ik = nk * (TK // IK)
    tpos = qb * TQ + lax.broadcasted_iota(jnp.int32, (1, TQ), 1)
    row = lax.broadcasted_iota(jnp.int32, (IK, 1), 0)
    w_rows = iw_ref[...] * idx_scale
    n_idx_heads = iq_ref.shape[0]

    def index_chunk(c, carry):
        k0 = pl.multiple_of(c * IK, IK)
        kblk = ik_ref[pl.ds(k0, IK), :]
        acc = jnp.zeros((IK, TQ), jnp.float32)
        for h in range(n_idx_heads):
            lg = jnp.dot(kblk, iq_ref[h], preferred_element_type=jnp.float32)
            acc = acc + w_rows[h:h + 1, :] * jnp.maximum(lg, 0.0)
        acc = acc + 0.0
        bits = lax.bitcast_convert_type(acc, jnp.int32)
        key = bits ^ ((bits >> 31) & 0x7FFFFFFF)
        key_scr[pl.ds(k0, IK), :] = jnp.where(k0 + row <= tpos, key, INT_MIN)
        return carry

    lax.fori_loop(0, n_ik, index_chunk, 0)

    def count(pred_fn):
        def body(c, cnt8):
            k0 = pl.multiple_of(c * IK, IK)
            hit = jnp.where(pred_fn(key_scr[pl.ds(k0, IK), :]), 1, 0).astype(jnp.int32)
            return cnt8 + _fold8(hit, jnp.sum)
        cnt8 = lax.fori_loop(0, n_ik, body, jnp.zeros((SUBLANES, TQ), jnp.int32))
        return jnp.sum(cnt8, axis=0, keepdims=True)

    thr = jnp.where(count(lambda kk: kk >= 0) >= topk, 0, INT_MIN).astype(jnp.int32) + jnp.zeros((1, TQ), jnp.int32)

    def bit_step(i, thr):
        cand = thr + jnp.left_shift(jnp.int32(1), 30 - i)
        return jnp.where(count(lambda kk: kk >= cand) >= topk, cand, thr)

    thr = lax.fori_loop(0, 31, bit_step, thr)
    thr = jnp.maximum(thr, INT_MIN + 1)
    need = (topk - count(lambda kk: kk > thr)).astype(jnp.float32)

    tri = jnp.where(lax.broadcasted_iota(jnp.int32, (IK, IK), 1) <= lax.broadcasted_iota(jnp.int32, (IK, IK), 0),
                    1.0, 0.0).astype(jnp.bfloat16)

    def bias_chunk(c, run):
        k0 = pl.multiple_of(c * IK, IK)
        kk = key_scr[pl.ds(k0, IK), :]
        eq = jnp.where(kk == thr, 1.0, 0.0)
        pref = jnp.dot(tri, eq.astype(jnp.bfloat16), preferred_element_type=jnp.float32) + run
        tie_ok = jnp.where(pref <= need, 0.0, NEG_INF)
        bias_scr[pl.ds(k0, IK), :] = jnp.where(kk > thr, 0.0, jnp.where(kk == thr, tie_ok, NEG_INF))
        return pref[IK - 1:IK, :]

    lax.fori_loop(0, n_ik, bias_chunk, jnp.zeros((1, TQ), jnp.float32))

    def emit(h, o_lat):
        d_v = wuvt_ref.shape[1]
        acc_scr[pl.ds(0, d_v), :] = jnp.dot(wuvt_ref[h], o_lat.astype(jnp.bfloat16),
                                            preferred_element_type=jnp.float32)
        o_ref[h] = acc_scr[pl.ds(0, d_v), :].T.astype(o_ref.dtype)

    _attend_heads(kv_ref, ckvt_ref, q_ref, q_ref.shape[0],
                  lambda c: bias_scr[pl.ds(pl.multiple_of(c * TK, TK), TK), :], 0, nk, s_scr, acc_scr, scale, emit)


def _dsa_attention_t(idx_qt, idx_k, idx_wt, proj, cos_t, sin_s, w_ukt, kv_lat, ckvt, w_uvt):
    bsz, n_ih, d_i, seq = idx_qt.shape
    n_h, _, d_nope = w_ukt.shape
    d_qk = kv_lat.shape[-1]
    d_v, rank = w_uvt.shape[1:]
    w_nope, w_rope = n_h * d_nope, n_h * (d_qk - rank)
    assert w_nope % w_rope == 0
    topk = min(IDX_TOPK, seq // 4)
    kern = functools.partial(_dsa_kernel, topk=topk, idx_scale=(IDX_DIM ** -0.5) * (IDX_HEADS ** -0.5),
                             scale=(A_NOPE + A_ROPE) ** -0.5)
    return pl.pallas_call(
        kern,
        grid=(bsz, seq // TQ),
        in_specs=[
            pl.BlockSpec((None, n_ih, d_i, TQ), lambda b, q: (b, 0, 0, q)),
            pl.BlockSpec((None, seq, d_i), lambda b, q: (b, 0, 0)),
            pl.BlockSpec((None, n_ih, TQ), lambda b, q: (b, 0, q)),
            pl.BlockSpec((None, TQ, w_nope), lambda b, q: (b, q, 0)),
            pl.BlockSpec((None, TQ, w_rope), lambda b, q: (b, q, w_nope // w_rope)),
            pl.BlockSpec((TQ, cos_t.shape[1]), lambda b, q: (q, 0)),
            pl.BlockSpec((TQ, sin_s.shape[1]), lambda b, q: (q, 0)),
            pl.BlockSpec((n_h, rank, d_nope), lambda b, q: (0, 0, 0)),
            pl.BlockSpec((None, seq, d_qk), lambda b, q: (b, 0, 0)),
            pl.BlockSpec((None, seq // TK, rank, TK), lambda b, q: (b, 0, 0, 0)),
            pl.BlockSpec((n_h, d_v, rank), lambda b, q: (0, 0, 0)),
        ],
        out_specs=pl.BlockSpec((None, n_h, TQ, d_v), lambda b, q: (b, 0, q, 0)),
        out_shape=jax.ShapeDtypeStruct((bsz, n_h, seq, d_v), jnp.bfloat16),
        scratch_shapes=[pltpu.VMEM((seq, TQ), jnp.int32), pltpu.VMEM((seq, TQ), jnp.float32),
                        pltpu.VMEM((2, seq, TQ), jnp.float32), pltpu.VMEM((rank, TQ), jnp.float32),
                        pltpu.VMEM((n_h, d_qk, TQ), jnp.bfloat16)],
        compiler_params=pltpu.CompilerParams(dimension_semantics=("parallel", "arbitrary"),
                                             vmem_limit_bytes=VMEM_LIMIT_BYTES),
        name="dsa_attention",
    )(idx_qt, idx_k, idx_wt, proj, proj, cos_t, sin_s, w_ukt, kv_lat, ckvt, w_uvt)


def _nsa_kernel(q_ref, kc_ref, vct_ref, ovt_ref, ks_ref, vst_ref, kw_ref, vwt_ref, g_ref, o_ref,
                blk_scr, bias_scr, s_scr, acc_scr, mix_scr, *, n_sel, scale):
    qb = pl.program_id(2)
    nk = ((qb + 1) * TQ + TK - 1) // TK
    tpos = qb * TQ + lax.broadcasted_iota(jnp.int32, (1, TQ), 1)
    row = lax.broadcasted_iota(jnp.int32, (TK, 1), 0)
    n_grp, hd = q_ref.shape[0], q_ref.shape[1]
    n_cmp_pad = kc_ref.shape[0]
    n_slc = ovt_ref.shape[0]
    gate = lambda g, j: jax.nn.sigmoid(g_ref[pl.ds(3 * g + j, 1), :])
    head_rows = lambda g: pl.ds(pl.multiple_of(g * hd, hd), hd)

    c_last = lax.broadcasted_iota(jnp.int32, (n_cmp_pad, 1), 0) * CMP_STRIDE + (CMP_LEN - 1)
    cbias = jnp.where(c_last <= tpos, 0.0, NEG_INF)
    psum = jnp.zeros((n_cmp_pad, TQ), jnp.float32)
    for g in range(n_grp):
        s = jnp.dot(kc_ref[...], q_ref[g], preferred_element_type=jnp.float32) * scale + cbias
        m = jnp.max(s, axis=0, keepdims=True)
        m = jnp.where(m == NEG_INF, 0.0, m)
        e = jnp.exp(s - m)
        den = jnp.sum(e, axis=0, keepdims=True)
        p = e * (1.0 / jnp.where(den > 0, den, 1.0))
        psum = psum + p
        o_cmp = jnp.dot(vct_ref[...], p.astype(jnp.bfloat16), preferred_element_type=jnp.float32)
        mix_scr[pl.ds(g * hd, hd), :] = gate(g, 0) * o_cmp

    hi = psum.astype(jnp.bfloat16)
    r1 = psum - hi.astype(jnp.float32)
    mid = r1.astype(jnp.bfloat16)
    lo = (r1 - mid.astype(jnp.float32)).astype(jnp.bfloat16)
    ovt = ovt_ref[...]
    imp = (jnp.dot(ovt, hi, preferred_element_type=jnp.float32)
           + jnp.dot(ovt, mid, preferred_element_type=jnp.float32)
           + jnp.dot(ovt, lo, preferred_element_type=jnp.float32))
    jidx = lax.broadcasted_iota(jnp.int32, (n_slc, 1), 0)
    t_blk = tpos // SLC_LEN
    forced = (jidx == 0) | (jidx == t_blk) | (jidx == t_blk - 1)
    imp = jnp.where(forced, jnp.inf, imp)
    imp = jnp.where(jidx * SLC_LEN <= tpos, imp, NEG_INF)
    rank = jnp.zeros((n_slc, TQ), jnp.int32)
    for jp in range(n_slc):
        r = imp[jp:jp + 1, :]
        rank = rank + jnp.where(r > imp, 1, jnp.where(r == imp, jnp.where(jp < jidx, 1, 0), 0))
    blk_scr[...] = jnp.where(rank < n_sel, 0.0, NEG_INF)

    blocks_per_chunk = TK // SLC_LEN

    def slc_bias_chunk(c, carry):
        k0 = pl.multiple_of(c * TK, TK)
        rows = [jnp.broadcast_to(blk_scr[pl.ds(c * blocks_per_chunk + i, 1), :], (SLC_LEN, TQ))
                for i in range(blocks_per_chunk)]
        bias_scr[pl.ds(k0, TK), :] = jnp.where(k0 + row <= tpos, jnp.concatenate(rows, axis=0), NEG_INF)
        return carry

    lax.fori_loop(0, nk, slc_bias_chunk, 0)

    def slc_bias(c):
        return bias_scr[pl.ds(pl.multiple_of(c * TK, TK), TK), :]

    def win_bias(c):
        spos = c * TK + row
        return jnp.where(spos <= tpos, jnp.where(spos > tpos - WIN_LEN, 0.0, NEG_INF), NEG_INF)

    w_lo = jnp.maximum(qb * TQ - WIN_LEN + 1, 0) // TK

    def add_branch(j):
        def emit(g, o):
            mix_scr[head_rows(g), :] += gate(g, j) * o
        return emit

    _attend_heads(ks_ref, vst_ref, q_ref, n_grp, slc_bias, 0, nk, s_scr, acc_scr, scale, add_branch(1))
    _attend_heads(kw_ref, vwt_ref, q_ref, n_grp, win_bias, w_lo, nk, s_scr, acc_scr, scale, add_branch(2))
    for g in range(n_grp):
        o_ref[g] = mix_scr[pl.ds(g * hd, hd), :].T.astype(o_ref.dtype)


def _nsa_attention_t(qt, kc, vct, ovt, ks, vst, kw, vwt, gates_t):
    bsz, kvh, grp, hd, seq = qt.shape
    ncp = kc.shape[2]
    n_slc = ovt.shape[0]
    kern = functools.partial(_nsa_kernel, n_sel=min(SLC_COUNT, n_slc), scale=hd ** -0.5)
    kv_spec = pl.BlockSpec((None, None, seq, hd), lambda b, h, q: (b, h, 0, 0))
    vt_spec = pl.BlockSpec((None, None, seq // TK, hd, TK), lambda b, h, q: (b, h, 0, 0, 0))
    return pl.pallas_call(
        kern,
        grid=(bsz, kvh, seq // TQ),
        in_specs=[
            pl.BlockSpec((None, None, grp, hd, TQ), lambda b, h, q: (b, h, 0, 0, q)),
            pl.BlockSpec((None, None, ncp, hd), lambda b, h, q: (b, h, 0, 0)),
            pl.BlockSpec((None, None, hd, ncp), lambda b, h, q: (b, h, 0, 0)),
            pl.BlockSpec((n_slc, ncp), lambda b, h, q: (0, 0)),
            kv_spec, vt_spec, kv_spec, vt_spec,
            pl.BlockSpec((None, None, 3 * grp, TQ), lambda b, h, q: (b, h, 0, q)),
        ],
        out_specs=pl.BlockSpec((None, grp, TQ, hd), lambda b, h, q: (b, h, q, 0)),
        out_shape=jax.ShapeDtypeStruct((bsz, kvh * grp, seq, hd), jnp.bfloat16),
        scratch_shapes=[pltpu.VMEM((n_slc, TQ), jnp.float32), pltpu.VMEM((seq, TQ), jnp.float32),
                        pltpu.VMEM((2, seq, TQ), jnp.float32), pltpu.VMEM((hd, TQ), jnp.float32),
                        pltpu.VMEM((grp * hd, TQ), jnp.float32)],
        compiler_params=pltpu.CompilerParams(dimension_semantics=("parallel", "parallel", "arbitrary"),
                                             vmem_limit_bytes=VMEM_LIMIT_BYTES),
        name="nsa_attention",
    )(qt, kc, vct, ovt, ks, vst, kw, vwt, gates_t)


HG_C = 64
HG_SB = 8
HG_CT = 1024


def _cumsum_rows(x):
    n = x.shape[0]
    row = lax.broadcasted_iota(jnp.int32, (n, 1), 0)
    s = 1
    while s < n:
        x = x + jnp.where(row >= s, pltpu.roll(x, s, axis=0), 0.0)
        s *= 2
    return x


def _bcast_rows(ref, first, period, n):
    return jnp.concatenate(
        [jnp.broadcast_to(ref[pl.ds(first + p * period, 1), :], (period, ref.shape[1])) for p in range(n // period)],
        axis=0)


def _hgrn_kernel(q_ref, f_ref, i_ref, g_ref, lb_ref, gn_ref, o_ref, st_scr, k_scr, b_scr, v_scr):
    @pl.when(pl.program_id(2) == 0)
    def _():
        st_scr[...] = jnp.zeros_like(st_scr)

    c_len, d = HG_C, q_ref.shape[1]
    bf = jnp.bfloat16
    lb = lb_ref[...]
    gain = gn_ref[...]
    row = lax.broadcasted_iota(jnp.int32, (c_len, 1), 0)
    col = lax.broadcasted_iota(jnp.int32, (1, c_len), 1)
    ones_b = jnp.ones((d, d), bf)
    nt = (((1,), (1,)), ((), ()))
    tn = (((0,), (0,)), ((), ()))

    def chunk(c, carry):
        sl = pl.ds(pl.multiple_of(c * c_len, c_len), c_len)
        qr, fr, v, gr = q_ref[sl, :], f_ref[sl, :], i_ref[sl, :], g_ref[sl, :]
        qv = qr * jax.nn.sigmoid(qr)
        fg = lb + (1.0 - lb) * jax.nn.sigmoid(fr)
        kk = 1.0 - fg
        b = _cumsum_rows(jnp.log(fg))
        k_scr[...] = kk
        b_scr[...] = b
        v_scr[...] = v
        v_b = v.astype(bf)

        att = jnp.zeros((c_len, c_len), jnp.float32)
        size = c_len // 2
        while size >= HG_SB:
            ref = _bcast_rows(b_scr, size - 1, 2 * size, c_len)
            is_q = ((row // size) % 2) == 1
            a_q = qv * jnp.exp(jnp.where(is_q, b - ref, NEG_INF))
            k_f = kk * jnp.exp(jnp.where(is_q, NEG_INF, ref - b))
            att_l = lax.dot_general(a_q.astype(bf), k_f.astype(bf), nt, preferred_element_type=jnp.float32)
            if 2 * size < c_len:
                att_l = jnp.where((row // (2 * size)) == (col // (2 * size)), att_l, 0.0)
            att = att + att_l
            size //= 2
        o = jnp.dot(att.astype(bf), v_b, preferred_element_type=jnp.float32)

        rmod = row % HG_SB
        for j in range(HG_SB):
            kb = _bcast_rows(k_scr, j, HG_SB, c_len)
            bb = _bcast_rows(b_scr, j, HG_SB, c_len)
            vb = _bcast_rows(v_scr, j, HG_SB, c_len)
            x = qv * kb * jnp.exp(jnp.where(rmod >= j, b - bb, NEG_INF))
            o = o + jnp.dot(x.astype(bf), ones_b, preferred_element_type=jnp.float32) * vb

        st = st_scr[...]
        o = o + lax.dot_general((qv * jnp.exp(b)).astype(bf), st.astype(bf), nt, preferred_element_type=jnp.float32)
        b_last = b[c_len - 1:c_len, :]
        k_l = kk * jnp.exp(b_last - b)
        st_scr[...] = st * jnp.exp(b_last) + lax.dot_general(v_b, k_l.astype(bf), tn,
                                                            preferred_element_type=jnp.float32)

        ms = jnp.mean(o * o, axis=-1, keepdims=True)
        o_ref[sl, :] = o * lax.rsqrt(ms + NORM_EPS) * gain * (gr * jax.nn.sigmoid(gr))
        return carry

    lax.fori_loop(0, q_ref.shape[0] // c_len, chunk, 0, unroll=4)


def _hgrn_recurrence(proj, lower_bound, out_norm):
    bsz, seq, w4 = proj.shape
    width = w4 // 4
    n_heads = width // C_HEAD_DIM
    ct = min(HG_CT, seq)
    assert seq % ct == 0 and ct % HG_C == 0
    part = lambda j: pl.BlockSpec((None, ct, C_HEAD_DIM), lambda b, h, t: (b, t, j * n_heads + h))
    vec = pl.BlockSpec((None, 1, C_HEAD_DIM), lambda b, h, t: (h, 0, 0))
    return pl.pallas_call(
        _hgrn_kernel,
        grid=(bsz, n_heads, seq // ct),
        in_specs=[part(0), part(1), part(2), part(3), vec, vec],
        out_specs=pl.BlockSpec((None, ct, C_HEAD_DIM), lambda b, h, t: (b, t, h)),
        out_shape=jax.ShapeDtypeStruct((bsz, seq, width), jnp.float32),
        scratch_shapes=[pltpu.VMEM((C_HEAD_DIM, C_HEAD_DIM), jnp.float32)]
        + [pltpu.VMEM((HG_C, C_HEAD_DIM), jnp.float32)] * 3,
        compiler_params=pltpu.CompilerParams(dimension_semantics=("parallel", "parallel", "arbitrary"),
                                             vmem_limit_bytes=VMEM_LIMIT_BYTES),
        name="hgrn2_recurrence",
    )(proj, proj, proj, proj, lower_bound.reshape(n_heads, 1, C_HEAD_DIM).astype(jnp.float32),
      out_norm.reshape(n_heads, 1, C_HEAD_DIM).astype(jnp.float32))


def rms_norm(x, g):
    xf = x.astype(jnp.float32)
    y = xf * lax.rsqrt(jnp.mean(xf * xf, axis=-1, keepdims=True) + NORM_EPS)
    return (y * g.astype(jnp.float32)).astype(x.dtype)


def rope(x, pos):
    d = x.shape[-1]
    inv = ROPE_THETA ** (-jnp.arange(0, d, 2, dtype=jnp.float32) / d)
    ang = pos.astype(jnp.float32)[:, None] * inv[None, :]
    cos = jnp.cos(ang)[:, None, :]
    sin = jnp.sin(ang)[:, None, :]
    xf = x.astype(jnp.float32)
    x1, x2 = xf[..., : d // 2], xf[..., d // 2:]
    return jnp.concatenate([x1 * cos - x2 * sin, x2 * cos + x1 * sin], axis=-1).astype(x.dtype)


def split_cols(a, widths):
    return jnp.split(a, np.cumsum(widths)[:-1].tolist(), axis=-1)


def dsa_nsa_mixer(u, w_in, kv_norm, w_uk, w_uv, cmp_pos, cmp_w1, cmp_w2, w_out):
    bsz, seq, _ = u.shape
    pos = jnp.arange(seq)
    proj = _proj(u, w_in)
    (_, _, a_ckv, a_k_rope, i_q, i_k, i_w, b_q, b_kv, b_gate) = split_cols(proj, AB_WIDTHS)
    c_kv = rms_norm(a_ckv, kv_norm)
    k_rope = rope(a_k_rope[:, :, None, :], pos)[:, :, 0]
    kv_lat = jnp.concatenate([c_kv, k_rope], axis=-1)
    idx_q = rope(i_q.reshape(bsz, seq, IDX_HEADS, IDX_DIM), pos)
    idx_k = rope(i_k[:, :, None, :], pos)[:, :, 0]
    bf = jnp.bfloat16
    inv = ROPE_THETA ** (-jnp.arange(0, A_ROPE, 2, dtype=jnp.float32) / A_ROPE)
    ang = pos.astype(jnp.float32)[:, None] * inv[None, :]
    lane_tile = 128
    cos_t = jnp.tile(jnp.cos(ang), (1, lane_tile // (A_ROPE // 2)))
    sin_s = jnp.tile(jnp.concatenate([-jnp.sin(ang), jnp.sin(ang)], axis=1), (1, lane_tile // A_ROPE))
    o_a_t = _dsa_attention_t(
        idx_q.transpose(0, 2, 3, 1).astype(bf), idx_k.astype(bf), i_w.transpose(0, 2, 1),
        proj, cos_t, sin_s, w_uk.transpose(0, 2, 1).astype(bf), kv_lat.astype(bf),
        c_kv.reshape(bsz, seq // TK, TK, A_KV_RANK).transpose(0, 1, 3, 2).astype(bf),
        w_uv.transpose(0, 2, 1).astype(bf))
    bq = rope(b_q.reshape(bsz, seq, B_HEADS, B_HEAD_DIM), pos)
    kv = b_kv.reshape(bsz, seq, 6, B_KV_HEADS, B_HEAD_DIM)
    o_b_t = nsa_attention_t(bq, rope(kv[:, :, 0], pos), kv[:, :, 1],
                            rope(kv[:, :, 2], pos), kv[:, :, 3],
                            rope(kv[:, :, 4], pos), kv[:, :, 5],
                            b_gate, cmp_pos, cmp_w1, cmp_w2)
    return _matmul_heads([o_a_t, o_b_t], w_out).reshape(bsz, seq, w_out.shape[1])


def nsa_attention_t(q, k_cmp, v_cmp, k_slc, v_slc, k_win, v_win, gates, cmp_pos, cmp_w1, cmp_w2):
    bsz, seq = q.shape[:2]
    kvh, grp, hd = B_KV_HEADS, B_GROUP, B_HEAD_DIM
    bf = jnp.bfloat16
    n_cmp = (seq - CMP_LEN) // CMP_STRIDE + 1
    n_cmp_pad = -(-n_cmp // 128) * 128
    tok = np.arange(n_cmp)[:, None] * CMP_STRIDE + np.arange(CMP_LEN)[None, :]

    def compress(a, j):
        blk = a[:, tok] + cmp_pos[j][:, None, :]
        blk = blk.transpose(0, 1, 3, 2, 4).reshape(bsz, n_cmp, kvh, CMP_LEN * hd)
        out = jax.nn.gelu(blk @ cmp_w1[j]) @ cmp_w2[j]
        return jnp.pad(out, ((0, 0), (0, n_cmp_pad - n_cmp), (0, 0), (0, 0)))

    kc = compress(k_cmp, 0).transpose(0, 2, 1, 3).astype(bf)
    vct = compress(v_cmp, 1).transpose(0, 2, 3, 1).astype(bf)
    n_slc = seq // SLC_LEN
    cmp_start = np.arange(n_cmp_pad) * CMP_STRIDE
    slc_start = np.arange(n_slc) * SLC_LEN
    ovt = ((cmp_start[None, :] < slc_start[:, None] + SLC_LEN)
           & (cmp_start[None, :] + CMP_LEN > slc_start[:, None]) & (np.arange(n_cmp_pad)[None, :] < n_cmp))
    ovt = jnp.asarray(ovt, bf)
    qt = q.reshape(bsz, seq, kvh, grp, hd).transpose(0, 2, 3, 4, 1).astype(bf)
    k_t = lambda a: a.transpose(0, 2, 1, 3).astype(bf)
    v_t = lambda a: a.reshape(bsz, seq // TK, TK, kvh, hd).transpose(0, 3, 1, 4, 2).astype(bf)
    gates_t = gates.reshape(bsz, seq, kvh, 3 * grp).transpose(0, 2, 3, 1)
    return _nsa_attention_t(qt, kc, vct, ovt, k_t(k_slc), v_t(v_slc), k_t(k_win), v_t(v_win), gates_t)


def hgrn2_mixer(u, w_in, lower_bound, out_norm, w_out):
    return _proj(_hgrn_recurrence(_proj(u, w_in), lower_bound, out_norm), w_out)


def routed_experts(xb_tok, e_id, e_w, w_gate_up, w_down, layer):
    n_as = e_id.shape[0]
    tok = jnp.arange(n_as, dtype=jnp.int32) // GROUP_TOPK
    onehot = (e_id[:, None] == jnp.arange(N_EXPERTS, dtype=e_id.dtype)[None, :]).astype(jnp.int32)
    csum = jnp.cumsum(onehot, axis=0)
    counts = csum[-1]
    rank = jnp.take_along_axis(csum, e_id[:, None], axis=1)[:, 0] - 1
    padded = (counts + MOE_BLOCK - 1) // MOE_BLOCK * MOE_BLOCK
    pad_end = jnp.cumsum(padded)
    pad_start = pad_end - padded
    dest = pad_start[e_id] + rank
    n_blk = -(-(n_as + N_EXPERTS * (MOE_BLOCK - 1)) // MOE_BLOCK)
    rows = n_blk * MOE_BLOCK
    fields = jnp.stack([tok, lax.bitcast_convert_type(e_w, jnp.int32)], axis=1)
    placed = jnp.zeros((rows, 2), jnp.int32).at[dest].set(fields)
    row_tok = placed[:, 0]
    row_w = lax.bitcast_convert_type(placed[:, 1], jnp.float32)
    blk_e = jnp.minimum(jnp.searchsorted(pad_end, jnp.arange(n_blk) * MOE_BLOCK, side='right'),
                        N_EXPERTS - 1).astype(jnp.int32)
    n_used = (pad_end[-1:] // MOE_BLOCK).astype(jnp.int32)
    assert n_blk % MOE_SPLITS == 0
    per = n_blk // MOE_SPLITS
    runs = _moe_runs(blk_e, n_used, per)
    yb = None
    for s in range(MOE_SPLITS):
        r = slice(s * per * MOE_BLOCK, (s + 1) * per * MOE_BLOCK)
        yb = _moe_blocks(blk_e, n_used, runs, xb_tok[row_tok[r]], w_gate_up, w_down, row_w[r, None], layer,
                         s * per, yb)
    return [yb[dest[s::GROUP_TOPK]] for s in range(GROUP_TOPK)]


def hier_moe(xb_tok, logits, b_group, b_expert, w_gate_up, w_down, layer):
    n_tok = xb_tok.shape[0]
    g_logits = logits[:, :N_GROUPS] + b_group.astype(jnp.float32)
    g_prob = jax.nn.softmax(g_logits, axis=-1)
    grp = jnp.argmax(g_logits, axis=-1)
    g_w = jnp.take_along_axis(g_prob, grp[:, None], axis=-1)
    e_logits = logits[:, N_GROUPS:N_GROUPS + N_EXPERTS] + b_expert.astype(jnp.float32)
    e_logits = e_logits.reshape(n_tok, N_GROUPS, EXPERTS_PER_GROUP)
    e_logits = jnp.take_along_axis(e_logits, grp[:, None, None], axis=1)[:, 0]
    top_v, top_i = lax.top_k(e_logits, GROUP_TOPK)
    e_w = jax.nn.softmax(top_v, axis=-1) * g_w
    e_id = (grp[:, None] * EXPERTS_PER_GROUP + top_i).astype(jnp.int32)
    return routed_experts(xb_tok, e_id.reshape(-1), e_w.reshape(-1).astype(jnp.float32), w_gate_up, w_down, layer)


def kernel(x, mix_norm, ffn_norm, final_norm, ab_w_in, dsa_kv_norm, dsa_w_uk, dsa_w_uv, nsa_cmp_pos, nsa_cmp_w1, nsa_cmp_w2, ab_w_out, hgrn_w_in, hgrn_lb_logits, hgrn_out_norm, hgrn_w_out, moe_w_group, moe_b_group, moe_w_expert, moe_b_expert, moe_w_gate_up, moe_w_down):
    lb_p = jax.nn.softmax(hgrn_lb_logits.astype(jnp.float32), axis=0)
    lower_bounds = jnp.cumsum(lb_p, axis=0) - lb_p[0]
    bsz, seq, d = x.shape
    bf = jnp.bfloat16
    h = x.reshape(bsz * seq, d)
    pending = []
    for layer in range(DEPTH):
        j = layer // 2
        if pending:
            h, u = _add_norm(h, pending, mix_norm[layer], bf)
        else:
            (u,) = _add_norm(h, [], mix_norm[layer], bf, write_h=False)
        u = u.reshape(bsz, seq, d)
        if layer % 2 == 0:
            m = dsa_nsa_mixer(u, ab_w_in[j], dsa_kv_norm[j], dsa_w_uk[j], dsa_w_uv[j],
                              nsa_cmp_pos[j], nsa_cmp_w1[j], nsa_cmp_w2[j], ab_w_out[j])
        else:
            m = hgrn2_mixer(u, hgrn_w_in[j], lower_bounds[layer], hgrn_out_norm[j], hgrn_w_out[j])
        w_router = jnp.concatenate([moe_w_group[layer], moe_w_expert[layer]], axis=1)
        w_router = jnp.pad(w_router, ((0, 0), (0, ROUTER_PAD - w_router.shape[1])))
        h, u, logits = _add_norm(h, [m.reshape(bsz * seq, d)], ffn_norm[layer], bf, w_router=w_router)
        pending = hier_moe(u, logits, moe_b_group[layer], moe_b_expert[layer], moe_w_gate_up, moe_w_down, layer)
    (out,) = _add_norm(h, pending, final_norm, jnp.float32, write_h=False)
    return out.reshape(bsz, seq, d)
```

```python
import functools

import numpy as np
import jax
import jax.numpy as jnp
from jax import lax
from jax.experimental import pallas as pl
from jax.experimental.pallas import tpu as pltpu

D_MODEL = 4096
DEPTH = 2

ROPE_THETA = 10000.0
NORM_EPS = 1e-6

A_HEADS = 16
A_NOPE = 128
A_ROPE = 64
A_KV_RANK = 512
A_V_DIM = 128
IDX_HEADS = 32
IDX_DIM = 64
IDX_TOPK = 256

B_HEADS = 16
B_KV_HEADS = 4
B_GROUP = B_HEADS // B_KV_HEADS
B_HEAD_DIM = 128
CMP_LEN = 32
CMP_STRIDE = 16
CMP_HIDDEN = 256
SLC_LEN = 64
SLC_COUNT = 16
WIN_LEN = 512

AB_WIDTHS = (
    A_HEADS * A_NOPE,
    A_HEADS * A_ROPE,
    A_KV_RANK,
    A_ROPE,
    IDX_HEADS * IDX_DIM,
    IDX_DIM,
    IDX_HEADS,
    B_HEADS * B_HEAD_DIM,
    6 * B_KV_HEADS * B_HEAD_DIM,
    3 * B_HEADS,
)

C_HEAD_DIM = 128

N_GROUPS = 8
EXPERTS_PER_GROUP = 8
N_EXPERTS = N_GROUPS * EXPERTS_PER_GROUP
GROUP_TOPK = 2
EXPERT_FF = 3 * D_MODEL // 32
MOE_BLOCK = 128

VMEM_LIMIT_BYTES = 48 * 1024 * 1024


MM_TM = 1024
MM_TN = 512


def _mm_kernel(x_ref, w_ref, o_ref):
    o_ref[...] = jnp.dot(x_ref[...], w_ref[...].astype(jnp.bfloat16), preferred_element_type=jnp.float32)


def _matmul(x, w):
    m, kd = x.shape
    n = w.shape[1]
    assert m % MM_TM == 0
    return pl.pallas_call(
        _mm_kernel,
        grid=(pl.cdiv(n, MM_TN), m // MM_TM),
        in_specs=[pl.BlockSpec((MM_TM, kd), lambda j, i: (i, 0)),
                  pl.BlockSpec((kd, MM_TN), lambda j, i: (0, j))],
        out_specs=pl.BlockSpec((MM_TM, MM_TN), lambda j, i: (i, j)),
        out_shape=jax.ShapeDtypeStruct((m, n), jnp.float32),
        compiler_params=pltpu.CompilerParams(
            dimension_semantics=("parallel", "parallel"),
            vmem_limit_bytes=VMEM_LIMIT_BYTES),
        name="matmul",
    )(x, w)


def _mm_heads_kernel(*refs):
    *x_refs, w_ref, o_ref = refs
    x = jnp.concatenate([x_ref[h] for x_ref in x_refs for h in range(x_ref.shape[0])], axis=1)
    o_ref[...] = jnp.dot(x, w_ref[...].astype(jnp.bfloat16), preferred_element_type=jnp.float32)


def _matmul_heads(xs, w):
    bsz, _, seq, hd = xs[0].shape
    kd, n = w.shape
    assert sum(x.shape[1] for x in xs) * hd == kd and seq % MM_TM == 0
    tiles = seq // MM_TM
    return pl.pallas_call(
        _mm_heads_kernel,
        grid=(pl.cdiv(n, MM_TN), bsz * tiles),
        in_specs=[pl.BlockSpec((None, x.shape[1], MM_TM, hd), lambda j, i: (i // tiles, 0, i % tiles, 0)) for x in xs]
        + [pl.BlockSpec((kd, MM_TN), lambda j, i: (0, j))],
        out_specs=pl.BlockSpec((MM_TM, MM_TN), lambda j, i: (i, j)),
        out_shape=jax.ShapeDtypeStruct((bsz * seq, n), jnp.float32),
        compiler_params=pltpu.CompilerParams(
            dimension_semantics=("parallel", "parallel"),
            vmem_limit_bytes=VMEM_LIMIT_BYTES),
        name="matmul_heads",
    )(*xs, w)


def _proj(u, w):
    lead = u.shape[:-1]
    return _matmul(u.reshape(-1, u.shape[-1]).astype(jnp.bfloat16), w).reshape(*lead, w.shape[1])


MOE_KC = 1024
MOE_VMEM_LIMIT_BYTES = 58 * 1024 * 1024
MOE_SPLITS = 4


def _moe_kernel(blk_e_ref, n_used_ref, run_ref, nxt_ref, slot_ref, x_ref, wgu_hbm, wd_hbm, rw_ref, *rest,
                first_blk, layer):
    o_ref, wgu_buf, wd_buf, sem = rest[-4:]
    step = pl.program_id(0)
    i = step + first_blk
    _, d_model, ff2 = wgu_buf.shape
    ff = ff2 // 2
    used = i < n_used_ref[0]
    slot = slot_ref[i]

    def weight_copies(e, s):
        return (pltpu.make_async_copy(wgu_hbm.at[layer, e], wgu_buf.at[s], sem.at[0, s]),
                pltpu.make_async_copy(wd_hbm.at[layer, e], wd_buf.at[s], sem.at[1, s]))

    @pl.when(used & (step == 0))
    def _():
        for cp in weight_copies(blk_e_ref[i], slot):
            cp.start()

    @pl.when(used & (run_ref[i] == 1))
    def _():
        nxt = nxt_ref[i]

        @pl.when(nxt >= 0)
        def _():
            for cp in weight_copies(nxt, 1 - slot):
                cp.start()

        for cp in weight_copies(blk_e_ref[i], slot):
            cp.wait()

    @pl.when(used)
    def _():
        gu = jnp.zeros((x_ref.shape[0], ff2), jnp.float32)
        for k0 in range(0, d_model, MOE_KC):
            gu = gu + jnp.dot(x_ref[:, k0:k0 + MOE_KC], wgu_buf[slot, k0:k0 + MOE_KC, :].astype(jnp.bfloat16),
                              preferred_element_type=jnp.float32)
        hid = (jax.nn.silu(gu[:, :ff]) * gu[:, ff:]).astype(jnp.bfloat16)
        rw = rw_ref[...]
        for n0 in range(0, d_model, MOE_KC):
            y = rw * jnp.dot(hid, wd_buf[slot, :, n0:n0 + MOE_KC].astype(jnp.bfloat16),
                             preferred_element_type=jnp.float32)
            o_ref[:, n0:n0 + MOE_KC] = y.astype(o_ref.dtype)

    @pl.when(jnp.logical_not(used))
    def _():
        o_ref[...] = jnp.zeros_like(o_ref)


def _moe_runs(blk_e, n_used, per):
    n = blk_e.shape[0]
    idx = jnp.arange(n, dtype=jnp.int32)
    used = idx < n_used[0]
    prev_e = jnp.concatenate([blk_e[:1], blk_e[:-1]])
    run = used & ((idx % per == 0) | (blk_e != prev_e))
    run_i = run.astype(jnp.int32)
    runs_in_call = jnp.cumsum(run_i.reshape(n // per, per), axis=1).reshape(n)
    slot = (runs_in_call + 1) % 2
    start_e = jnp.where(run, blk_e, -1).reshape(n // per, per)

    def nxt_scan(carry, col):
        return jnp.where(col >= 0, col, carry), carry

    _, nxt_cols = lax.scan(nxt_scan, jnp.full((n // per,), -1, jnp.int32), start_e.T[::-1])
    nxt = nxt_cols[::-1].T.reshape(n)
    return run_i, nxt.astype(jnp.int32), slot.astype(jnp.int32)


def _moe_blocks(blk_e, n_used, runs, xb, w_gate_up, w_down, row_w, layer, first_blk, prev):
    rows, d = xb.shape
    n_blk = rows // MOE_BLOCK
    ff2 = w_gate_up.shape[-1]
    in_specs = [pl.BlockSpec((MOE_BLOCK, d), lambda i, *_: (i, 0)),
                pl.BlockSpec(memory_space=pl.ANY),
                pl.BlockSpec(memory_space=pl.ANY),
                pl.BlockSpec((MOE_BLOCK, 1), lambda i, *_: (i, 0))]
    operands = [blk_e, n_used, *runs, xb, w_gate_up, w_down, row_w]
    aliases = {}
    if prev is not None:
        in_specs.append(pl.BlockSpec(memory_space=pl.ANY))
        aliases = {len(operands): 0}
        operands.append(prev)
    grid_spec = pltpu.PrefetchScalarGridSpec(
        num_scalar_prefetch=2 + len(runs),
        grid=(n_blk,),
        in_specs=in_specs,
        out_specs=pl.BlockSpec((MOE_BLOCK, d), lambda i, *_: (i + first_blk, 0)),
        scratch_shapes=[pltpu.VMEM((2, d, ff2), jnp.float32), pltpu.VMEM((2, ff2 // 2, d), jnp.float32),
                        pltpu.SemaphoreType.DMA((2, 2))],
    )
    return pl.pallas_call(
        functools.partial(_moe_kernel, first_blk=first_blk, layer=layer),
        grid_spec=grid_spec,
        out_shape=jax.ShapeDtypeStruct((blk_e.shape[0] * MOE_BLOCK, d), jnp.bfloat16),
        input_output_aliases=aliases,
        compiler_params=pltpu.CompilerParams(dimension_semantics=("arbitrary",),
                                             vmem_limit_bytes=MOE_VMEM_LIMIT_BYTES),
        name="moe_experts",
    )(*operands)


NORM_TM = 256
ROUTER_PAD = 128


def _add_norm_kernel(*refs, n_add, has_router, write_h):
    it = iter(refs)
    h_ref = next(it)
    add_refs = [next(it) for _ in range(n_add)]
    g_ref = next(it)
    wr_ref = next(it) if has_router else None
    h_out = next(it) if write_h else None
    u_ref = next(it)
    lg_ref = next(it) if has_router else None
    hv = h_ref[...]
    for a_ref in add_refs:
        hv = hv + a_ref[...].astype(jnp.float32)
    if write_h:
        h_out[...] = hv
    y = hv * lax.rsqrt(jnp.mean(hv * hv, axis=-1, keepdims=True) + NORM_EPS) * g_ref[...]
    u_ref[...] = y.astype(u_ref.dtype)
    if has_router:
        lg_ref[...] = jnp.dot(y, wr_ref[...], precision=lax.Precision.HIGHEST, preferred_element_type=jnp.float32)


def _add_norm(h, adds, gain, out_dtype, w_router=None, write_h=True):
    t, d = h.shape
    row = pl.BlockSpec((NORM_TM, d), lambda i: (i, 0))
    operands = [h, *adds, gain.reshape(1, d).astype(jnp.float32)]
    in_specs = [row] * (1 + len(adds)) + [pl.BlockSpec((1, d), lambda i: (0, 0))]
    out_shape, out_specs = [], []
    if w_router is not None:
        operands.append(w_router)
        in_specs.append(pl.BlockSpec(w_router.shape, lambda i: (0, 0)))
    if write_h:
        out_shape.append(jax.ShapeDtypeStruct((t, d), jnp.float32))
        out_specs.append(row)
    out_shape.append(jax.ShapeDtypeStruct((t, d), out_dtype))
    out_specs.append(row)
    if w_router is not None:
        out_shape.append(jax.ShapeDtypeStruct((t, w_router.shape[1]), jnp.float32))
        out_specs.append(pl.BlockSpec((NORM_TM, w_router.shape[1]), lambda i: (i, 0)))
    return pl.pallas_call(
        functools.partial(_add_norm_kernel, n_add=len(adds), has_router=w_router is not None, write_h=write_h),
        grid=(t // NORM_TM,),
        in_specs=in_specs,
        out_specs=out_specs,
        out_shape=out_shape,
        compiler_params=pltpu.CompilerParams(dimension_semantics=("parallel",),
                                             vmem_limit_bytes=VMEM_LIMIT_BYTES),
        name="add_norm",
    )(*operands)


TQ = 256
TK = 512
IK = 256
SUBLANES = 8
NEG_INF = float("-inf")
INT_MIN = -2 ** 31


def _fold8(x, op):
    r, c = x.shape
    return op(x.reshape(r // SUBLANES, SUBLANES, c), axis=0)


def _attend_heads(k_ref, vt_ref, q_ref, n_heads, bias_fn, c_lo, c_hi, s_scr, acc_scr, scale, emit):
    tq = q_ref.shape[-1]

    def scores(c, qt, slot, m8):
        k0 = pl.multiple_of(c * TK, TK)
        s = jnp.dot(k_ref[pl.ds(k0, TK), :], qt, preferred_element_type=jnp.float32) * scale + bias_fn(c)
        s_scr[slot, pl.ds(k0, TK), :] = s
        return jnp.maximum(m8, _fold8(s, jnp.max))

    neg = jnp.full((SUBLANES, tq), NEG_INF, jnp.float32)
    q0 = q_ref[0]
    m8_first = lax.fori_loop(c_lo, c_hi, lambda c, m8: scores(c, q0, 0, m8), neg)

    def head(h, m8):
        slot = h % 2
        m = jnp.max(m8, axis=0, keepdims=True)
        m = jnp.where(m == NEG_INF, 0.0, m)
        acc_scr[...] = jnp.zeros_like(acc_scr)
        q_next = q_ref[jnp.minimum(h + 1, n_heads - 1)]

        def body(c, carry):
            d8, m8n = carry
            k0 = pl.multiple_of(c * TK, TK)
            p = jnp.exp(s_scr[slot, pl.ds(k0, TK), :] - m)
            acc_scr[...] += jnp.dot(vt_ref[c], p.astype(jnp.bfloat16), preferred_element_type=jnp.float32)
            return d8 + _fold8(p, jnp.sum), scores(c, q_next, 1 - slot, m8n)

        d8, m8_next = lax.fori_loop(c_lo, c_hi, body, (jnp.zeros((SUBLANES, tq), jnp.float32), neg))
        den = jnp.sum(d8, axis=0, keepdims=True)
        emit(h, acc_scr[...] * (1.0 / jnp.where(den > 0, den, 1.0)))
        return m8_next

    lax.fori_loop(0, n_heads, head, m8_first)


def _dsa_queries(qn_ref, qr_ref, cos_ref, sin_ref, wukt_ref, qcat_scr):
    bf = jnp.bfloat16
    n_heads, rank, d_nope = wukt_ref.shape
    nt = (((1,), (1,)), ((), ()))
    for h in range(n_heads):
        qn_h = qn_ref[:, h * d_nope:(h + 1) * d_nope].astype(bf)
        qcat_scr[h, 0:rank, :] = lax.dot_general(wukt_ref[h], qn_h, nt, preferred_element_type=jnp.float32).astype(bf)
    x = qr_ref[...]
    width = x.shape[1]
    d_rope = width // n_heads
    half = d_rope // 2
    reps = width // cos_ref.shape[1]
    cos = jnp.concatenate([cos_ref[...]] * reps, axis=1)
    sin = jnp.concatenate([sin_ref[...]] * reps, axis=1)
    lane = lax.broadcasted_iota(jnp.int32, (1, width), 1)
    partner = jnp.where((lane % d_rope) < half, pltpu.roll(x, width - half, axis=1), pltpu.roll(x, half, axis=1))
    r = x * cos + partner * sin
    per_tile = cos_ref.shape[1] // d_rope
    for p in range(n_heads // per_tile):
        rt = r[:, p * per_tile * d_rope:(p + 1) * per_tile * d_rope].T
        for i in range(per_tile):
            qcat_scr[p * per_tile + i, rank:rank + d_rope, :] = rt[i * d_rope:(i + 1) * d_rope, :].astype(bf)


def _dsa_kernel(iq_ref, ik_ref, iw_ref, qn_ref, qr_ref, cos_ref, sin_ref, wukt_ref, kv_ref, ckvt_ref, wuvt_ref, o_ref,
                key_scr, bias_scr, s_scr, acc_scr, q_ref, *, topk, idx_scale, scale):
    _dsa_queries(qn_ref, qr_ref, cos_ref, sin_ref, wukt_ref, q_ref)
    qb = pl.program_id(1)
    nk = ((qb + 1) * TQ + TK - 1) // TK
    n_ik = nk * (TK // IK)
    tpos = qb * TQ + lax.broadcasted_iota(jnp.int32, (1, TQ), 1)
    row = lax.broadcasted_iota(jnp.int32, (IK, 1), 0)
    w_rows = iw_ref[...] * idx_scale
    n_idx_heads = iq_ref.shape[0]

    def index_chunk(c, carry):
        k0 = pl.multiple_of(c * IK, IK)
        kblk = ik_ref[pl.ds(k0, IK), :]
        acc = jnp.zeros((IK, TQ), jnp.float32)
        for h in range(n_idx_heads):
            lg = jnp.dot(kblk, iq_ref[h], preferred_element_type=jnp.float32)
            acc = acc + w_rows[h:h + 1, :] * jnp.maximum(lg, 0.0)
        acc = acc + 0.0
        bits = lax.bitcast_convert_type(acc, jnp.int32)
        key = bits ^ ((bits >> 31) & 0x7FFFFFFF)
        key_scr[pl.ds(k0, IK), :] = jnp.where(k0 + row <= tpos, key, INT_MIN)
        return carry

    lax.fori_loop(0, n_ik, index_chunk, 0)

    def count(pred_fn):
        def body(c, cnt8):
            k0 = pl.multiple_of(c * IK, IK)
            hit = jnp.where(pred_fn(key_scr[pl.ds(k0, IK), :]), 1, 0).astype(jnp.int32)
            return cnt8 + _fold8(hit, jnp.sum)
        cnt8 = lax.fori_loop(0, n_ik, body, jnp.zeros((SUBLANES, TQ), jnp.int32))
        return jnp.sum(cnt8, axis=0, keepdims=True)

    thr = jnp.where(count(lambda kk: kk >= 0) >= topk, 0, INT_MIN).astype(jnp.int32) + jnp.zeros((1, TQ), jnp.int32)

    def bit_step(i, thr):
        cand = thr + jnp.left_shift(jnp.int32(1), 30 - i)
        return jnp.where(count(lambda kk: kk >= cand) >= topk, cand, thr)

    thr = lax.fori_loop(0, 31, bit_step, thr)
    thr = jnp.maximum(thr, INT_MIN + 1)
    need = (topk - count(lambda kk: kk > thr)).astype(jnp.float32)

    tri = jnp.where(lax.broadcasted_iota(jnp.int32, (IK, IK), 1) <= lax.broadcasted_iota(jnp.int32, (IK, IK), 0),
                    1.0, 0.0).astype(jnp.bfloat16)

    def bias_chunk(c, run):
        k0 = pl.multiple_of(c * IK, IK)
        kk = key_scr[pl.ds(k0, IK), :]
        eq = jnp.where(kk == thr, 1.0, 0.0)
        pref = jnp.dot(tri, eq.astype(jnp.bfloat16), preferred_element_type=jnp.float32) + run
        tie_ok = jnp.where(pref <= need, 0.0, NEG_INF)
        bias_scr[pl.ds(k0, IK), :] = jnp.where(kk > thr, 0.0, jnp.where(kk == thr, tie_ok, NEG_INF))
        return pref[IK - 1:IK, :]

    lax.fori_loop(0, n_ik, bias_chunk, jnp.zeros((1, TQ), jnp.float32))

    def emit(h, o_lat):
        d_v = wuvt_ref.shape[1]
        acc_scr[pl.ds(0, d_v), :] = jnp.dot(wuvt_ref[h], o_lat.astype(jnp.bfloat16),
                                            preferred_element_type=jnp.float32)
        o_ref[h] = acc_scr[pl.ds(0, d_v), :].T.astype(o_ref.dtype)

    _attend_heads(kv_ref, ckvt_ref, q_ref, q_ref.shape[0],
                  lambda c: bias_scr[pl.ds(pl.multiple_of(c * TK, TK), TK), :], 0, nk, s_scr, acc_scr, scale, emit)


def _dsa_attention_t(idx_qt, idx_k, idx_wt, proj, cos_t, sin_s, w_ukt, kv_lat, ckvt, w_uvt):
    bsz, n_ih, d_i, seq = idx_qt.shape
    n_h, _, d_nope = w_ukt.shape
    d_qk = kv_lat.shape[-1]
    d_v, rank = w_uvt.shape[1:]
    w_nope, w_rope = n_h * d_nope, n_h * (d_qk - rank)
    assert w_nope % w_rope == 0
    topk = min(IDX_TOPK, seq // 4)
    kern = functools.partial(_dsa_kernel, topk=topk, idx_scale=(IDX_DIM ** -0.5) * (IDX_HEADS ** -0.5),
                             scale=(A_NOPE + A_ROPE) ** -0.5)
    return pl.pallas_call(
        kern,
        grid=(bsz, seq // TQ),
        in_specs=[
            pl.BlockSpec((None, n_ih, d_i, TQ), lambda b, q: (b, 0, 0, q)),
            pl.BlockSpec((None, seq, d_i), lambda b, q: (b, 0, 0)),
            pl.BlockSpec((None, n_ih, TQ), lambda b, q: (b, 0, q)),
            pl.BlockSpec((None, TQ, w_nope), lambda b, q: (b, q, 0)),
            pl.BlockSpec((None, TQ, w_rope), lambda b, q: (b, q, w_nope // w_rope)),
            pl.BlockSpec((TQ, cos_t.shape[1]), lambda b, q: (q, 0)),
            pl.BlockSpec((TQ, sin_s.shape[1]), lambda b, q: (q, 0)),
            pl.BlockSpec((n_h, rank, d_nope), lambda b, q: (0, 0, 0)),
            pl.BlockSpec((None, seq, d_qk), lambda b, q: (b, 0, 0)),
            pl.BlockSpec((None, seq // TK, rank, TK), lambda b, q: (b, 0, 0, 0)),
            pl.BlockSpec((n_h, d_v, rank), lambda b, q: (0, 0, 0)),
        ],
        out_specs=pl.BlockSpec((None, n_h, TQ, d_v), lambda b, q: (b, 0, q, 0)),
        out_shape=jax.ShapeDtypeStruct((bsz, n_h, seq, d_v), jnp.bfloat16),
        scratch_shapes=[pltpu.VMEM((seq, TQ), jnp.int32), pltpu.VMEM((seq, TQ), jnp.float32),
                        pltpu.VMEM((2, seq, TQ), jnp.float32), pltpu.VMEM((rank, TQ), jnp.float32),
                        pltpu.VMEM((n_h, d_qk, TQ), jnp.bfloat16)],
        compiler_params=pltpu.CompilerParams(dimension_semantics=("parallel", "arbitrary"),
                                             vmem_limit_bytes=VMEM_LIMIT_BYTES),
        name="dsa_attention",
    )(idx_qt, idx_k, idx_wt, proj, proj, cos_t, sin_s, w_ukt, kv_lat, ckvt, w_uvt)


def _nsa_kernel(q_ref, kc_ref, vct_ref, ovt_ref, ks_ref, vst_ref, kw_ref, vwt_ref, g_ref, o_ref,
                blk_scr, bias_scr, s_scr, acc_scr, mix_scr, *, n_sel, scale):
    qb = pl.program_id(2)
    nk = ((qb + 1) * TQ + TK - 1) // TK
    tpos = qb * TQ + lax.broadcasted_iota(jnp.int32, (1, TQ), 1)
    row = lax.broadcasted_iota(jnp.int32, (TK, 1), 0)
    n_grp, hd = q_ref.shape[0], q_ref.shape[1]
    n_cmp_pad = kc_ref.shape[0]
    n_slc = ovt_ref.shape[0]
    gate = lambda g, j: jax.nn.sigmoid(g_ref[pl.ds(3 * g + j, 1), :])
    head_rows = lambda g: pl.ds(pl.multiple_of(g * hd, hd), hd)

    c_last = lax.broadcasted_iota(jnp.int32, (n_cmp_pad, 1), 0) * CMP_STRIDE + (CMP_LEN - 1)
    cbias = jnp.where(c_last <= tpos, 0.0, NEG_INF)
    psum = jnp.zeros((n_cmp_pad, TQ), jnp.float32)
    for g in range(n_grp):
        s = jnp.dot(kc_ref[...], q_ref[g], preferred_element_type=jnp.float32) * scale + cbias
        m = jnp.max(s, axis=0, keepdims=True)
        m = jnp.where(m == NEG_INF, 0.0, m)
        e = jnp.exp(s - m)
        den = jnp.sum(e, axis=0, keepdims=True)
        p = e * (1.0 / jnp.where(den > 0, den, 1.0))
        psum = psum + p
        o_cmp = jnp.dot(vct_ref[...], p.astype(jnp.bfloat16), preferred_element_type=jnp.float32)
        mix_scr[pl.ds(g * hd, hd), :] = gate(g, 0) * o_cmp

    hi = psum.astype(jnp.bfloat16)
    r1 = psum - hi.astype(jnp.float32)
    mid = r1.astype(jnp.bfloat16)
    lo = (r1 - mid.astype(jnp.float32)).astype(jnp.bfloat16)
    ovt = ovt_ref[...]
    imp = (jnp.dot(ovt, hi, preferred_element_type=jnp.float32)
           + jnp.dot(ovt, mid, preferred_element_type=jnp.float32)
           + jnp.dot(ovt, lo, preferred_element_type=jnp.float32))
    jidx = lax.broadcasted_iota(jnp.int32, (n_slc, 1), 0)
    t_blk = tpos // SLC_LEN
    forced = (jidx == 0) | (jidx == t_blk) | (jidx == t_blk - 1)
    imp = jnp.where(forced, jnp.inf, imp)
    imp = jnp.where(jidx * SLC_LEN <= tpos, imp, NEG_INF)
    rank = jnp.zeros((n_slc, TQ), jnp.int32)
    for jp in range(n_slc):
        r = imp[jp:jp + 1, :]
        rank = rank + jnp.where(r > imp, 1, jnp.where(r == imp, jnp.where(jp < jidx, 1, 0), 0))
    blk_scr[...] = jnp.where(rank < n_sel, 0.0, NEG_INF)

    blocks_per_chunk = TK // SLC_LEN

    def slc_bias_chunk(c, carry):
        k0 = pl.multiple_of(c * TK, TK)
        rows = [jnp.broadcast_to(blk_scr[pl.ds(c * blocks_per_chunk + i, 1), :], (SLC_LEN, TQ))
                for i in range(blocks_per_chunk)]
        bias_scr[pl.ds(k0, TK), :] = jnp.where(k0 + row <= tpos, jnp.concatenate(rows, axis=0), NEG_INF)
        return carry

    lax.fori_loop(0, nk, slc_bias_chunk, 0)

    def slc_bias(c):
        return bias_scr[pl.ds(pl.multiple_of(c * TK, TK), TK), :]

    def win_bias(c):
        spos = c * TK + row
        return jnp.where(spos <= tpos, jnp.where(spos > tpos - WIN_LEN, 0.0, NEG_INF), NEG_INF)

    w_lo = jnp.maximum(qb * TQ - WIN_LEN + 1, 0) // TK

    def add_branch(j):
        def emit(g, o):
            mix_scr[head_rows(g), :] += gate(g, j) * o
        return emit

    _attend_heads(ks_ref, vst_ref, q_ref, n_grp, slc_bias, 0, nk, s_scr, acc_scr, scale, add_branch(1))
    _attend_heads(kw_ref, vwt_ref, q_ref, n_grp, win_bias, w_lo, nk, s_scr, acc_scr, scale, add_branch(2))
    for g in range(n_grp):
        o_ref[g] = mix_scr[pl.ds(g * hd, hd), :].T.astype(o_ref.dtype)


def _nsa_attention_t(qt, kc, vct, ovt, ks, vst, kw, vwt, gates_t):
    bsz, kvh, grp, hd, seq = qt.shape
    ncp = kc.shape[2]
    n_slc = ovt.shape[0]
    kern = functools.partial(_nsa_kernel, n_sel=min(SLC_COUNT, n_slc), scale=hd ** -0.5)
    kv_spec = pl.BlockSpec((None, None, seq, hd), lambda b, h, q: (b, h, 0, 0))
    vt_spec = pl.BlockSpec((None, None, seq // TK, hd, TK), lambda b, h, q: (b, h, 0, 0, 0))
    return pl.pallas_call(
        kern,
        grid=(bsz, kvh, seq // TQ),
        in_specs=[
            pl.BlockSpec((None, None, grp, hd, TQ), lambda b, h, q: (b, h, 0, 0, q)),
            pl.BlockSpec((None, None, ncp, hd), lambda b, h, q: (b, h, 0, 0)),
            pl.BlockSpec((None, None, hd, ncp), lambda b, h, q: (b, h, 0, 0)),
            pl.BlockSpec((n_slc, ncp), lambda b, h, q: (0, 0)),
            kv_spec, vt_spec, kv_spec, vt_spec,
            pl.BlockSpec((None, None, 3 * grp, TQ), lambda b, h, q: (b, h, 0, q)),
        ],
        out_specs=pl.BlockSpec((None, grp, TQ, hd), lambda b, h, q: (b, h, q, 0)),
        out_shape=jax.ShapeDtypeStruct((bsz, kvh * grp, seq, hd), jnp.bfloat16),
        scratch_shapes=[pltpu.VMEM((n_slc, TQ), jnp.float32), pltpu.VMEM((seq, TQ), jnp.float32),
                        pltpu.VMEM((2, seq, TQ), jnp.float32), pltpu.VMEM((hd, TQ), jnp.float32),
                        pltpu.VMEM((grp * hd, TQ), jnp.float32)],
        compiler_params=pltpu.CompilerParams(dimension_semantics=("parallel", "parallel", "arbitrary"),
                                             vmem_limit_bytes=VMEM_LIMIT_BYTES),
        name="nsa_attention",
    )(qt, kc, vct, ovt, ks, vst, kw, vwt, gates_t)


HG_C = 64
HG_SB = 8
HG_CT = 1024


def _cumsum_rows(x):
    n = x.shape[0]
    row = lax.broadcasted_iota(jnp.int32, (n, 1), 0)
    s = 1
    while s < n:
        x = x + jnp.where(row >= s, pltpu.roll(x, s, axis=0), 0.0)
        s *= 2
    return x


def _bcast_rows(ref, first, period, n):
    return jnp.concatenate(
        [jnp.broadcast_to(ref[pl.ds(first + p * period, 1), :], (period, ref.shape[1])) for p in range(n // period)],
        axis=0)


def _hgrn_kernel(q_ref, f_ref, i_ref, g_ref, lb_ref, gn_ref, o_ref, st_scr, k_scr, b_scr, v_scr):
    @pl.when(pl.program_id(2) == 0)
    def _():
        st_scr[...] = jnp.zeros_like(st_scr)

    c_len, d = HG_C, q_ref.shape[1]
    bf = jnp.bfloat16
    lb = lb_ref[...]
    gain = gn_ref[...]
    row = lax.broadcasted_iota(jnp.int32, (c_len, 1), 0)
    col = lax.broadcasted_iota(jnp.int32, (1, c_len), 1)
    ones_b = jnp.ones((d, d), bf)
    nt = (((1,), (1,)), ((), ()))
    tn = (((0,), (0,)), ((), ()))

    def chunk(c, carry):
        sl = pl.ds(pl.multiple_of(c * c_len, c_len), c_len)
        qr, fr, v, gr = q_ref[sl, :], f_ref[sl, :], i_ref[sl, :], g_ref[sl, :]
        qv = qr * jax.nn.sigmoid(qr)
        fg = lb + (1.0 - lb) * jax.nn.sigmoid(fr)
        kk = 1.0 - fg
        b = _cumsum_rows(jnp.log(fg))
        k_scr[...] = kk
        b_scr[...] = b
        v_scr[...] = v
        v_b = v.astype(bf)

        att = jnp.zeros((c_len, c_len), jnp.float32)
        size = c_len // 2
        while size >= HG_SB:
            ref = _bcast_rows(b_scr, size - 1, 2 * size, c_len)
            is_q = ((row // size) % 2) == 1
            a_q = qv * jnp.exp(jnp.where(is_q, b - ref, NEG_INF))
            k_f = kk * jnp.exp(jnp.where(is_q, NEG_INF, ref - b))
            att_l = lax.dot_general(a_q.astype(bf), k_f.astype(bf), nt, preferred_element_type=jnp.float32)
            if 2 * size < c_len:
                att_l = jnp.where((row // (2 * size)) == (col // (2 * size)), att_l, 0.0)
            att = att + att_l
            size //= 2
        o = jnp.dot(att.astype(bf), v_b, preferred_element_type=jnp.float32)

        rmod = row % HG_SB
        for j in range(HG_SB):
            kb = _bcast_rows(k_scr, j, HG_SB, c_len)
            bb = _bcast_rows(b_scr, j, HG_SB, c_len)
            vb = _bcast_rows(v_scr, j, HG_SB, c_len)
            x = qv * kb * jnp.exp(jnp.where(rmod >= j, b - bb, NEG_INF))
            o = o + jnp.dot(x.astype(bf), ones_b, preferred_element_type=jnp.float32) * vb

        st = st_scr[...]
        o = o + lax.dot_general((qv * jnp.exp(b)).astype(bf), st.astype(bf), nt, preferred_element_type=jnp.float32)
        b_last = b[c_len - 1:c_len, :]
        k_l = kk * jnp.exp(b_last - b)
        st_scr[...] = st * jnp.exp(b_last) + lax.dot_general(v_b, k_l.astype(bf), tn,
                                                            preferred_element_type=jnp.float32)

        ms = jnp.mean(o * o, axis=-1, keepdims=True)
        o_ref[sl, :] = o * lax.rsqrt(ms + NORM_EPS) * gain * (gr * jax.nn.sigmoid(gr))
        return carry

    lax.fori_loop(0, q_ref.shape[0] // c_len, chunk, 0, unroll=4)


def _hgrn_recurrence(proj, lower_bound, out_norm):
    bsz, seq, w4 = proj.shape
    width = w4 // 4
    n_heads = width // C_HEAD_DIM
    ct = min(HG_CT, seq)
    assert seq % ct == 0 and ct % HG_C == 0
    part = lambda j: pl.BlockSpec((None, ct, C_HEAD_DIM), lambda b, h, t: (b, t, j * n_heads + h))
    vec = pl.BlockSpec((None, 1, C_HEAD_DIM), lambda b, h, t: (h, 0, 0))
    return pl.pallas_call(
        _hgrn_kernel,
        grid=(bsz, n_heads, seq // ct),
        in_specs=[part(0), part(1), part(2), part(3), vec, vec],
        out_specs=pl.BlockSpec((None, ct, C_HEAD_DIM), lambda b, h, t: (b, t, h)),
        out_shape=jax.ShapeDtypeStruct((bsz, seq, width), jnp.float32),
        scratch_shapes=[pltpu.VMEM((C_HEAD_DIM, C_HEAD_DIM), jnp.float32)]
        + [pltpu.VMEM((HG_C, C_HEAD_DIM), jnp.float32)] * 3,
        compiler_params=pltpu.CompilerParams(dimension_semantics=("parallel", "parallel", "arbitrary"),
                                             vmem_limit_bytes=VMEM_LIMIT_BYTES),
        name="hgrn2_recurrence",
    )(proj, proj, proj, proj, lower_bound.reshape(n_heads, 1, C_HEAD_DIM).astype(jnp.float32),
      out_norm.reshape(n_heads, 1, C_HEAD_DIM).astype(jnp.float32))


def rms_norm(x, g):
    xf = x.astype(jnp.float32)
    y = xf * lax.rsqrt(jnp.mean(xf * xf, axis=-1, keepdims=True) + NORM_EPS)
    return (y * g.astype(jnp.float32)).astype(x.dtype)


def rope(x, pos):
    d = x.shape[-1]
    inv = ROPE_THETA ** (-jnp.arange(0, d, 2, dtype=jnp.float32) / d)
    ang = pos.astype(jnp.float32)[:, None] * inv[None, :]
    cos = jnp.cos(ang)[:, None, :]
    sin = jnp.sin(ang)[:, None, :]
    xf = x.astype(jnp.float32)
    x1, x2 = xf[..., : d // 2], xf[..., d // 2:]
    return jnp.concatenate([x1 * cos - x2 * sin, x2 * cos + x1 * sin], axis=-1).astype(x.dtype)


def split_cols(a, widths):
    return jnp.split(a, np.cumsum(widths)[:-1].tolist(), axis=-1)


def dsa_nsa_mixer(u, w_in, kv_norm, w_uk, w_uv, cmp_pos, cmp_w1, cmp_w2, w_out):
    bsz, seq, _ = u.shape
    pos = jnp.arange(seq)
    proj = _proj(u, w_in)
    (_, _, a_ckv, a_k_rope, i_q, i_k, i_w, b_q, b_kv, b_gate) = split_cols(proj, AB_WIDTHS)
    c_kv = rms_norm(a_ckv, kv_norm)
    k_rope = rope(a_k_rope[:, :, None, :], pos)[:, :, 0]
    kv_lat = jnp.concatenate([c_kv, k_rope], axis=-1)
    idx_q = rope(i_q.reshape(bsz, seq, IDX_HEADS, IDX_DIM), pos)
    idx_k = rope(i_k[:, :, None, :], pos)[:, :, 0]
    bf = jnp.bfloat16
    inv = ROPE_THETA ** (-jnp.arange(0, A_ROPE, 2, dtype=jnp.float32) / A_ROPE)
    ang = pos.astype(jnp.float32)[:, None] * inv[None, :]
    lane_tile = 128
    cos_t = jnp.tile(jnp.cos(ang), (1, lane_tile // (A_ROPE // 2)))
    sin_s = jnp.tile(jnp.concatenate([-jnp.sin(ang), jnp.sin(ang)], axis=1), (1, lane_tile // A_ROPE))
    o_a_t = _dsa_attention_t(
        idx_q.transpose(0, 2, 3, 1).astype(bf), idx_k.astype(bf), i_w.transpose(0, 2, 1),
        proj, cos_t, sin_s, w_uk.transpose(0, 2, 1).astype(bf), kv_lat.astype(bf),
        c_kv.reshape(bsz, seq // TK, TK, A_KV_RANK).transpose(0, 1, 3, 2).astype(bf),
        w_uv.transpose(0, 2, 1).astype(bf))
    bq = rope(b_q.reshape(bsz, seq, B_HEADS, B_HEAD_DIM), pos)
    kv = b_kv.reshape(bsz, seq, 6, B_KV_HEADS, B_HEAD_DIM)
    o_b_t = nsa_attention_t(bq, rope(kv[:, :, 0], pos), kv[:, :, 1],
                            rope(kv[:, :, 2], pos), kv[:, :, 3],
                            rope(kv[:, :, 4], pos), kv[:, :, 5],
                            b_gate, cmp_pos, cmp_w1, cmp_w2)
    return _matmul_heads([o_a_t, o_b_t], w_out).reshape(bsz, seq, w_out.shape[1])


def nsa_attention_t(q, k_cmp, v_cmp, k_slc, v_slc, k_win, v_win, gates, cmp_pos, cmp_w1, cmp_w2):
    bsz, seq = q.shape[:2]
    kvh, grp, hd = B_KV_HEADS, B_GROUP, B_HEAD_DIM
    bf = jnp.bfloat16
    n_cmp = (seq - CMP_LEN) // CMP_STRIDE + 1
    n_cmp_pad = -(-n_cmp // 128) * 128
    tok = np.arange(n_cmp)[:, None] * CMP_STRIDE + np.arange(CMP_LEN)[None, :]

    def compress(a, j):
        blk = a[:, tok] + cmp_pos[j][:, None, :]
        blk = blk.transpose(0, 1, 3, 2, 4).reshape(bsz, n_cmp, kvh, CMP_LEN * hd)
        out = jax.nn.gelu(blk @ cmp_w1[j]) @ cmp_w2[j]
        return jnp.pad(out, ((0, 0), (0, n_cmp_pad - n_cmp), (0, 0), (0, 0)))

    kc = compress(k_cmp, 0).transpose(0, 2, 1, 3).astype(bf)
    vct = compress(v_cmp, 1).transpose(0, 2, 3, 1).astype(bf)
    n_slc = seq // SLC_LEN
    cmp_start = np.arange(n_cmp_pad) * CMP_STRIDE
    slc_start = np.arange(n_slc) * SLC_LEN
    ovt = ((cmp_start[None, :] < slc_start[:, None] + SLC_LEN)
           & (cmp_start[None, :] + CMP_LEN > slc_start[:, None]) & (np.arange(n_cmp_pad)[None, :] < n_cmp))
    ovt = jnp.asarray(ovt, bf)
    qt = q.reshape(bsz, seq, kvh, grp, hd).transpose(0, 2, 3, 4, 1).astype(bf)
    k_t = lambda a: a.transpose(0, 2, 1, 3).astype(bf)
    v_t = lambda a: a.reshape(bsz, seq // TK, TK, kvh, hd).transpose(0, 3, 1, 4, 2).astype(bf)
    gates_t = gates.reshape(bsz, seq, kvh, 3 * grp).transpose(0, 2, 3, 1)
    return _nsa_attention_t(qt, kc, vct, ovt, k_t(k_slc), v_t(v_slc), k_t(k_win), v_t(v_win), gates_t)


def hgrn2_mixer(u, w_in, lower_bound, out_norm, w_out):
    return _proj(_hgrn_recurrence(_proj(u, w_in), lower_bound, out_norm), w_out)


def routed_experts(xb_tok, e_id, e_w, w_gate_up, w_down, layer):
    n_as = e_id.shape[0]
    tok = jnp.arange(n_as, dtype=jnp.int32) // GROUP_TOPK
    onehot = (e_id[:, None] == jnp.arange(N_EXPERTS, dtype=e_id.dtype)[None, :]).astype(jnp.int32)
    csum = jnp.cumsum(onehot, axis=0)
    counts = csum[-1]
    rank = jnp.take_along_axis(csum, e_id[:, None], axis=1)[:, 0] - 1
    padded = (counts + MOE_BLOCK - 1) // MOE_BLOCK * MOE_BLOCK
    pad_end = jnp.cumsum(padded)
    pad_start = pad_end - padded
    dest = pad_start[e_id] + rank
    n_blk = -(-(n_as + N_EXPERTS * (MOE_BLOCK - 1)) // MOE_BLOCK)
    rows = n_blk * MOE_BLOCK
    fields = jnp.stack([tok, lax.bitcast_convert_type(e_w, jnp.int32)], axis=1)
    filler = jnp.stack([jnp.arange(rows, dtype=jnp.int32) % xb_tok.shape[0], jnp.zeros((rows,), jnp.int32)], axis=1)
    placed = filler.at[dest].set(fields)
    row_tok = placed[:, 0]
    row_w = lax.bitcast_convert_type(placed[:, 1], jnp.float32)
    blk_e = jnp.minimum(jnp.searchsorted(pad_end, jnp.arange(n_blk) * MOE_BLOCK, side='right'),
                        N_EXPERTS - 1).astype(jnp.int32)
    n_used = (pad_end[-1:] // MOE_BLOCK).astype(jnp.int32)
    assert n_blk % MOE_SPLITS == 0
    per = n_blk // MOE_SPLITS
    runs = _moe_runs(blk_e, n_used, per)
    yb = None
    for s in range(MOE_SPLITS):
        r = slice(s * per * MOE_BLOCK, (s + 1) * per * MOE_BLOCK)
        yb = _moe_blocks(blk_e, n_used, runs, xb_tok[row_tok[r]], w_gate_up, w_down, row_w[r, None], layer,
                         s * per, yb)
    return [yb[dest[s::GROUP_TOPK]] for s in range(GROUP_TOPK)]


def hier_moe(xb_tok, logits, b_group, b_expert, w_gate_up, w_down, layer):
    n_tok = xb_tok.shape[0]
    g_logits = logits[:, :N_GROUPS] + b_group.astype(jnp.float32)
    g_prob = jax.nn.softmax(g_logits, axis=-1)
    grp = jnp.argmax(g_logits, axis=-1)
    g_w = jnp.take_along_axis(g_prob, grp[:, None], axis=-1)
    e_logits = logits[:, N_GROUPS:N_GROUPS + N_EXPERTS] + b_expert.astype(jnp.float32)
    e_logits = e_logits.reshape(n_tok, N_GROUPS, EXPERTS_PER_GROUP)
    e_logits = jnp.take_along_axis(e_logits, grp[:, None, None], axis=1)[:, 0]
    top_v, top_i = lax.top_k(e_logits, GROUP_TOPK)
    e_w = jax.nn.softmax(top_v, axis=-1) * g_w
    e_id = (grp[:, None] * EXPERTS_PER_GROUP + top_i).astype(jnp.int32)
    return routed_experts(xb_tok, e_id.reshape(-1), e_w.reshape(-1).astype(jnp.float32), w_gate_up, w_down, layer)


def kernel(x, mix_norm, ffn_norm, final_norm, ab_w_in, dsa_kv_norm, dsa_w_uk, dsa_w_uv, nsa_cmp_pos, nsa_cmp_w1, nsa_cmp_w2, ab_w_out, hgrn_w_in, hgrn_lb_logits, hgrn_out_norm, hgrn_w_out, moe_w_group, moe_b_group, moe_w_expert, moe_b_expert, moe_w_gate_up, moe_w_down):
    lb_p = jax.nn.softmax(hgrn_lb_logits.astype(jnp.float32), axis=0)
    lower_bounds = jnp.cumsum(lb_p, axis=0) - lb_p[0]
    bsz, seq, d = x.shape
    bf = jnp.bfloat16
    h = x.reshape(bsz * seq, d)
    pending = []
    for layer in range(DEPTH):
        j = layer // 2
        if pending:
            h, u = _add_norm(h, pending, mix_norm[layer], bf)
        else:
            (u,) = _add_norm(h, [], mix_norm[layer], bf, write_h=False)
        u = u.reshape(bsz, seq, d)
        if layer % 2 == 0:
            m = dsa_nsa_mixer(u, ab_w_in[j], dsa_kv_norm[j], dsa_w_uk[j], dsa_w_uv[j],
                              nsa_cmp_pos[j], nsa_cmp_w1[j], nsa_cmp_w2[j], ab_w_out[j])
        else:
            m = hgrn2_mixer(u, hgrn_w_in[j], lower_bounds[layer], hgrn_out_norm[j], hgrn_w_out[j])
        w_router = jnp.concatenate([moe_w_group[layer], moe_w_expert[layer]], axis=1)
        w_router = jnp.pad(w_router, ((0, 0), (0, ROUTER_PAD - w_router.shape[1])))
        h, u, logits = _add_norm(h, [m.reshape(bsz * seq, d)], ffn_norm[layer], bf, w_router=w_router)
        pending = hier_moe(u, logits, moe_b_group[layer], moe_b_expert[layer], moe_w_gate_up, moe_w_down, layer)
    (out,) = _add_norm(h, pending, final_norm, jnp.float32, write_h=False)
    return out.reshape(bsz, seq, d)
```

```python
import functools

import numpy as np
import jax
import jax.numpy as jnp
from jax import lax
from jax.experimental import pallas as pl
from jax.experimental.pallas import tpu as pltpu

D_MODEL = 4096
DEPTH = 2

ROPE_THETA = 10000.0
NORM_EPS = 1e-6

A_HEADS = 16
A_NOPE = 128
A_ROPE = 64
A_KV_RANK = 512
A_V_DIM = 128
IDX_HEADS = 32
IDX_DIM = 64
IDX_TOPK = 256

B_HEADS = 16
B_KV_HEADS = 4
B_GROUP = B_HEADS // B_KV_HEADS
B_HEAD_DIM = 128
CMP_LEN = 32
CMP_STRIDE = 16
CMP_HIDDEN = 256
SLC_LEN = 64
SLC_COUNT = 16
WIN_LEN = 512

AB_WIDTHS = (
    A_HEADS * A_NOPE,
    A_HEADS * A_ROPE,
    A_KV_RANK,
    A_ROPE,
    IDX_HEADS * IDX_DIM,
    IDX_DIM,
    IDX_HEADS,
    B_HEADS * B_HEAD_DIM,
    6 * B_KV_HEADS * B_HEAD_DIM,
    3 * B_HEADS,
)

C_HEAD_DIM = 128

N_GROUPS = 8
EXPERTS_PER_GROUP = 8
N_EXPERTS = N_GROUPS * EXPERTS_PER_GROUP
GROUP_TOPK = 2
EXPERT_FF = 3 * D_MODEL // 32
MOE_BLOCK = 128

VMEM_LIMIT_BYTES = 48 * 1024 * 1024


MM_TM = 1024
MM_TN = 512


def _mm_kernel(x_ref, w_ref, o_ref):
    o_ref[...] = jnp.dot(x_ref[...], w_ref[...].astype(jnp.bfloat16), preferred_element_type=jnp.float32)


def _matmul(x, w):
    m, kd = x.shape
    n = w.shape[1]
    assert m % MM_TM == 0
    return pl.pallas_call(
        _mm_kernel,
        grid=(pl.cdiv(n, MM_TN), m // MM_TM),
        in_specs=[pl.BlockSpec((MM_TM, kd), lambda j, i: (i, 0)),
                  pl.BlockSpec((kd, MM_TN), lambda j, i: (0, j))],
        out_specs=pl.BlockSpec((MM_TM, MM_TN), lambda j, i: (i, j)),
        out_shape=jax.ShapeDtypeStruct((m, n), jnp.float32),
        compiler_params=pltpu.CompilerParams(
            dimension_semantics=("parallel", "parallel"),
            vmem_limit_bytes=VMEM_LIMIT_BYTES),
        name="matmul",
    )(x, w)


def _mm_heads_kernel(*refs):
    *x_refs, w_ref, o_ref = refs
    x = jnp.concatenate([x_ref[h] for x_ref in x_refs for h in range(x_ref.shape[0])], axis=1)
    o_ref[...] = jnp.dot(x, w_ref[...].astype(jnp.bfloat16), preferred_element_type=jnp.float32)


def _matmul_heads(xs, w):
    bsz, _, seq, hd = xs[0].shape
    kd, n = w.shape
    assert sum(x.shape[1] for x in xs) * hd == kd and seq % MM_TM == 0
    tiles = seq // MM_TM
    return pl.pallas_call(
        _mm_heads_kernel,
        grid=(pl.cdiv(n, MM_TN), bsz * tiles),
        in_specs=[pl.BlockSpec((None, x.shape[1], MM_TM, hd), lambda j, i: (i // tiles, 0, i % tiles, 0)) for x in xs]
        + [pl.BlockSpec((kd, MM_TN), lambda j, i: (0, j))],
        out_specs=pl.BlockSpec((MM_TM, MM_TN), lambda j, i: (i, j)),
        out_shape=jax.ShapeDtypeStruct((bsz * seq, n), jnp.float32),
        compiler_params=pltpu.CompilerParams(
            dimension_semantics=("parallel", "parallel"),
            vmem_limit_bytes=VMEM_LIMIT_BYTES),
        name="matmul_heads",
    )(*xs, w)


def _proj(u, w):
    lead = u.shape[:-1]
    return _matmul(u.reshape(-1, u.shape[-1]).astype(jnp.bfloat16), w).reshape(*lead, w.shape[1])


MOE_KC = 1024
MOE_VMEM_LIMIT_BYTES = 58 * 1024 * 1024
MOE_SPLITS = 4
MOE_WEIGHT_DMA_PRIORITY = 1


def _moe_kernel(blk_e_ref, n_used_ref, run_ref, nxt_ref, slot_ref, x_ref, wgu_hbm, wd_hbm, rw_ref, *rest,
                first_blk, layer):
    o_ref, wgu_buf, wd_buf, sem = rest[-4:]
    step = pl.program_id(0)
    i = step + first_blk
    _, d_model, ff2 = wgu_buf.shape
    ff = ff2 // 2
    used = i < n_used_ref[0]
    slot = slot_ref[i]

    def weight_copies(e, s):
        return (pltpu.make_async_copy(wgu_hbm.at[layer, e], wgu_buf.at[s], sem.at[0, s]),
                pltpu.make_async_copy(wd_hbm.at[layer, e], wd_buf.at[s], sem.at[1, s]))

    @pl.when(used & (step == 0))
    def _():
        for cp in weight_copies(blk_e_ref[i], slot):
            cp.start(priority=MOE_WEIGHT_DMA_PRIORITY)

    @pl.when(used & (run_ref[i] == 1))
    def _():
        nxt = nxt_ref[i]

        @pl.when(nxt >= 0)
        def _():
            for cp in weight_copies(nxt, 1 - slot):
                cp.start(priority=MOE_WEIGHT_DMA_PRIORITY)

        for cp in weight_copies(blk_e_ref[i], slot):
            cp.wait()

    @pl.when(used)
    def _():
        gu = jnp.zeros((x_ref.shape[0], ff2), jnp.float32)
        for k0 in range(0, d_model, MOE_KC):
            gu = gu + jnp.dot(x_ref[:, k0:k0 + MOE_KC], wgu_buf[slot, k0:k0 + MOE_KC, :].astype(jnp.bfloat16),
                              preferred_element_type=jnp.float32)
        hid = (jax.nn.silu(gu[:, :ff]) * gu[:, ff:]).astype(jnp.bfloat16)
        rw = rw_ref[...]
        for n0 in range(0, d_model, MOE_KC):
            y = rw * jnp.dot(hid, wd_buf[slot, :, n0:n0 + MOE_KC].astype(jnp.bfloat16),
                             preferred_element_type=jnp.float32)
            o_ref[:, n0:n0 + MOE_KC] = y.astype(o_ref.dtype)

    @pl.when(jnp.logical_not(used))
    def _():
        o_ref[...] = jnp.zeros_like(o_ref)


def _moe_runs(blk_e, n_used, per):
    n = blk_e.shape[0]
    idx = jnp.arange(n, dtype=jnp.int32)
    used = idx < n_used[0]
    prev_e = jnp.concatenate([blk_e[:1], blk_e[:-1]])
    run = used & ((idx % per == 0) | (blk_e != prev_e))
    run_i = run.astype(jnp.int32)
    runs_in_call = jnp.cumsum(run_i.reshape(n // per, per), axis=1).reshape(n)
    slot = (runs_in_call + 1) % 2
    start_e = jnp.where(run, blk_e, -1).reshape(n // per, per)

    def nxt_scan(carry, col):
        return jnp.where(col >= 0, col, carry), carry

    _, nxt_cols = lax.scan(nxt_scan, jnp.full((n // per,), -1, jnp.int32), start_e.T[::-1])
    nxt = nxt_cols[::-1].T.reshape(n)
    return run_i, nxt.astype(jnp.int32), slot.astype(jnp.int32)


def _moe_blocks(blk_e, n_used, runs, xb, w_gate_up, w_down, row_w, layer, first_blk, prev):
    rows, d = xb.shape
    n_blk = rows // MOE_BLOCK
    ff2 = w_gate_up.shape[-1]
    in_specs = [pl.BlockSpec((MOE_BLOCK, d), lambda i, *_: (i, 0)),
                pl.BlockSpec(memory_space=pl.ANY),
                pl.BlockSpec(memory_space=pl.ANY),
                pl.BlockSpec((MOE_BLOCK, 1), lambda i, *_: (i, 0))]
    operands = [blk_e, n_used, *runs, xb, w_gate_up, w_down, row_w]
    aliases = {}
    if prev is not None:
        in_specs.append(pl.BlockSpec(memory_space=pl.ANY))
        aliases = {len(operands): 0}
        operands.append(prev)
    grid_spec = pltpu.PrefetchScalarGridSpec(
        num_scalar_prefetch=2 + len(runs),
        grid=(n_blk,),
        in_specs=in_specs,
        out_specs=pl.BlockSpec((MOE_BLOCK, d), lambda i, *_: (i + first_blk, 0)),
        scratch_shapes=[pltpu.VMEM((2, d, ff2), jnp.float32), pltpu.VMEM((2, ff2 // 2, d), jnp.float32),
                        pltpu.SemaphoreType.DMA((2, 2))],
    )
    return pl.pallas_call(
        functools.partial(_moe_kernel, first_blk=first_blk, layer=layer),
        grid_spec=grid_spec,
        out_shape=jax.ShapeDtypeStruct((blk_e.shape[0] * MOE_BLOCK, d), jnp.bfloat16),
        input_output_aliases=aliases,
        compiler_params=pltpu.CompilerParams(dimension_semantics=("arbitrary",),
                                             vmem_limit_bytes=MOE_VMEM_LIMIT_BYTES),
        name="moe_experts",
    )(*operands)


NORM_TM = 256
ROUTER_PAD = 128


def _add_norm_kernel(*refs, n_add, has_router, write_h):
    it = iter(refs)
    h_ref = next(it)
    add_refs = [next(it) for _ in range(n_add)]
    g_ref = next(it)
    wr_ref = next(it) if has_router else None
    h_out = next(it) if write_h else None
    u_ref = next(it)
    lg_ref = next(it) if has_router else None
    hv = h_ref[...]
    for a_ref in add_refs:
        hv = hv + a_ref[...].astype(jnp.float32)
    if write_h:
        h_out[...] = hv
    y = hv * lax.rsqrt(jnp.mean(hv * hv, axis=-1, keepdims=True) + NORM_EPS) * g_ref[...]
    u_ref[...] = y.astype(u_ref.dtype)
    if has_router:
        lg_ref[...] = jnp.dot(y, wr_ref[...], precision=lax.Precision.HIGHEST, preferred_element_type=jnp.float32)


def _add_norm(h, adds, gain, out_dtype, w_router=None, write_h=True):
    t, d = h.shape
    row = pl.BlockSpec((NORM_TM, d), lambda i: (i, 0))
    operands = [h, *adds, gain.reshape(1, d).astype(jnp.float32)]
    in_specs = [row] * (1 + len(adds)) + [pl.BlockSpec((1, d), lambda i: (0, 0))]
    out_shape, out_specs = [], []
    if w_router is not None:
        operands.append(w_router)
        in_specs.append(pl.BlockSpec(w_router.shape, lambda i: (0, 0)))
    if write_h:
        out_shape.append(jax.ShapeDtypeStruct((t, d), jnp.float32))
        out_specs.append(row)
    out_shape.append(jax.ShapeDtypeStruct((t, d), out_dtype))
    out_specs.append(row)
    if w_router is not None:
        out_shape.append(jax.ShapeDtypeStruct((t, w_router.shape[1]), jnp.float32))
        out_specs.append(pl.BlockSpec((NORM_TM, w_router.shape[1]), lambda i: (i, 0)))
    return pl.pallas_call(
        functools.partial(_add_norm_kernel, n_add=len(adds), has_router=w_router is not None, write_h=write_h),
        grid=(t // NORM_TM,),
        in_specs=in_specs,
        out_specs=out_specs,
        out_shape=out_shape,
        compiler_params=pltpu.CompilerParams(dimension_semantics=("parallel",),
                                             vmem_limit_bytes=VMEM_LIMIT_BYTES),
        name="add_norm",
    )(*operands)


TQ = 256
TK = 512
IK = 256
SUBLANES = 8
NEG_INF = float("-inf")
INT_MIN = -2 ** 31


def _fold8(x, op):
    r, c = x.shape
    return op(x.reshape(r // SUBLANES, SUBLANES, c), axis=0)


def _attend_heads(k_ref, vt_ref, q_ref, n_heads, bias_fn, c_lo, c_hi, s_scr, acc_scr, scale, emit):
    tq = q_ref.shape[-1]

    def scores(c, qt, slot, m8):
        k0 = pl.multiple_of(c * TK, TK)
        s = jnp.dot(k_ref[pl.ds(k0, TK), :], qt, preferred_element_type=jnp.float32) * scale + bias_fn(c)
        s_scr[slot, pl.ds(k0, TK), :] = s
        return jnp.maximum(m8, _fold8(s, jnp.max))

    neg = jnp.full((SUBLANES, tq), NEG_INF, jnp.float32)
    q0 = q_ref[0]
    m8_first = lax.fori_loop(c_lo, c_hi, lambda c, m8: scores(c, q0, 0, m8), neg)

    def head(h, m8):
        slot = h % 2
        m = jnp.max(m8, axis=0, keepdims=True)
        m = jnp.where(m == NEG_INF, 0.0, m)
        acc_scr[...] = jnp.zeros_like(acc_scr)
        q_next = q_ref[jnp.minimum(h + 1, n_heads - 1)]

        def body(c, carry):
            d8, m8n = carry
            k0 = pl.multiple_of(c * TK, TK)
            p = jnp.exp(s_scr[slot, pl.ds(k0, TK), :] - m)
            acc_scr[...] += jnp.dot(vt_ref[c], p.astype(jnp.bfloat16), preferred_element_type=jnp.float32)
            return d8 + _fold8(p, jnp.sum), scores(c, q_next, 1 - slot, m8n)

        d8, m8_next = lax.fori_loop(c_lo, c_hi, body, (jnp.zeros((SUBLANES, tq), jnp.float32), neg))
        den = jnp.sum(d8, axis=0, keepdims=True)
        emit(h, acc_scr[...] * (1.0 / jnp.where(den > 0, den, 1.0)))
        return m8_next

    lax.fori_loop(0, n_heads, head, m8_first)


def _dsa_queries(qn_ref, qr_ref, cos_ref, sin_ref, wukt_ref, qcat_scr):
    bf = jnp.bfloat16
    n_heads, rank, d_nope = wukt_ref.shape
    nt = (((1,), (1,)), ((), ()))
    for h in range(n_heads):
        qn_h = qn_ref[:, h * d_nope:(h + 1) * d_nope].astype(bf)
        qcat_scr[h, 0:rank, :] = lax.dot_general(wukt_ref[h], qn_h, nt, preferred_element_type=jnp.float32).astype(bf)
    x = qr_ref[...]
    width = x.shape[1]
    d_rope = width // n_heads
    half = d_rope // 2
    reps = width // cos_ref.shape[1]
    cos = jnp.concatenate([cos_ref[...]] * reps, axis=1)
    sin = jnp.concatenate([sin_ref[...]] * reps, axis=1)
    lane = lax.broadcasted_iota(jnp.int32, (1, width), 1)
    partner = jnp.where((lane % d_rope) < half, pltpu.roll(x, width - half, axis=1), pltpu.roll(x, half, axis=1))
    r = x * cos + partner * sin
    per_tile = cos_ref.shape[1] // d_rope
    for p in range(n_heads // per_tile):
        rt = r[:, p * per_tile * d_rope:(p + 1) * per_tile * d_rope].T
        for i in range(per_tile):
            qcat_scr[p * per_tile + i, rank:rank + d_rope, :] = rt[i * d_rope:(i + 1) * d_rope, :].astype(bf)


def _dsa_kernel(iq_ref, ik_ref, iw_ref, qn_ref, qr_ref, cos_ref, sin_ref, wukt_ref, kv_ref, ckvt_ref, wuvt_ref, o_ref,
                key_scr, bias_scr, s_scr, acc_scr, q_ref, *, topk, idx_scale, scale):
    _dsa_queries(qn_ref, qr_ref, cos_ref, sin_ref, wukt_ref, q_ref)
    qb = pl.program_id(1)
    nk = ((qb + 1) * TQ + TK - 1) // TK
    n_ik = nk * (TK // IK)
    tpos = qb * TQ + lax.broadcasted_iota(jnp.int32, (1, TQ), 1)
    row = lax.broadcasted_iota(jnp.int32, (IK, 1), 0)
    w_rows = iw_ref[...] * idx_scale
    n_idx_heads = iq_ref.shape[0]

    def index_chunk(c, carry):
        k0 = pl.multiple_of(c * IK, IK)
        kblk = ik_ref[pl.ds(k0, IK), :]
        acc = jnp.zeros((IK, TQ), jnp.float32)
        for h in range(n_idx_heads):
            lg = jnp.dot(kblk, iq_ref[h], preferred_element_type=jnp.float32)
            acc = acc + w_rows[h:h + 1, :] * jnp.maximum(lg, 0.0)
        acc = acc + 0.0
        bits = lax.bitcast_convert_type(acc, jnp.int32)
        key = bits ^ ((bits >> 31) & 0x7FFFFFFF)
        key_scr[pl.ds(k0, IK), :] = jnp.where(k0 + row <= tpos, key, INT_MIN)
        return carry

    lax.fori_loop(0, n_ik, index_chunk, 0)

    def count(pred_fn):
        def body(c, cnt8):
            k0 = pl.multiple_of(c * IK, IK)
            hit = jnp.where(pred_fn(key_scr[pl.ds(k0, IK), :]), 1, 0).astype(jnp.int32)
            return cnt8 + _fold8(hit, jnp.sum)
        cnt8 = lax.fori_loop(0, n_ik, body, jnp.zeros((SUBLANES, TQ), jnp.int32))
        return jnp.sum(cnt8, axis=0, keepdims=True)

    thr = jnp.where(count(lambda kk: kk >= 0) >= topk, 0, INT_MIN).astype(jnp.int32) + jnp.zeros((1, TQ), jnp.int32)

    def bit_step(i, thr):
        cand = thr + jnp.left_shift(jnp.int32(1), 30 - i)
        return jnp.where(count(lambda kk: kk >= cand) >= topk, cand, thr)

    thr = lax.fori_loop(0, 31, bit_step, thr)
    thr = jnp.maximum(thr, INT_MIN + 1)
    need = (topk - count(lambda kk: kk > thr)).astype(jnp.float32)

    tri = jnp.where(lax.broadcasted_iota(jnp.int32, (IK, IK), 1) <= lax.broadcasted_iota(jnp.int32, (IK, IK), 0),
                    1.0, 0.0).astype(jnp.bfloat16)

    def bias_chunk(c, run):
        k0 = pl.multiple_of(c * IK, IK)
        kk = key_scr[pl.ds(k0, IK), :]
        eq = jnp.where(kk == thr, 1.0, 0.0)
        pref = jnp.dot(tri, eq.astype(jnp.bfloat16), preferred_element_type=jnp.float32) + run
        tie_ok = jnp.where(pref <= need, 0.0, NEG_INF)
        bias_scr[pl.ds(k0, IK), :] = jnp.where(kk > thr, 0.0, jnp.where(kk == thr, tie_ok, NEG_INF))
        return pref[IK - 1:IK, :]

    lax.fori_loop(0, n_ik, bias_chunk, jnp.zeros((1, TQ), jnp.float32))

    def emit(h, o_lat):
        d_v = wuvt_ref.shape[1]
        acc_scr[pl.ds(0, d_v), :] = jnp.dot(wuvt_ref[h], o_lat.astype(jnp.bfloat16),
                                            preferred_element_type=jnp.float32)
        o_ref[h] = acc_scr[pl.ds(0, d_v), :].T.astype(o_ref.dtype)

    _attend_heads(kv_ref, ckvt_ref, q_ref, q_ref.shape[0],
                  lambda c: bias_scr[pl.ds(pl.multiple_of(c * TK, TK), TK), :], 0, nk, s_scr, acc_scr, scale, emit)


def _dsa_attention_t(idx_qt, idx_k, idx_wt, proj, cos_t, sin_s, w_ukt, kv_lat, ckvt, w_uvt):
    bsz, n_ih, d_i, seq = idx_qt.shape
    n_h, _, d_nope = w_ukt.shape
    d_qk = kv_lat.shape[-1]
    d_v, rank = w_uvt.shape[1:]
    w_nope, w_rope = n_h * d_nope, n_h * (d_qk - rank)
    assert w_nope % w_rope == 0
    topk = min(IDX_TOPK, seq // 4)
    kern = functools.partial(_dsa_kernel, topk=topk, idx_scale=(IDX_DIM ** -0.5) * (IDX_HEADS ** -0.5),
                             scale=(A_NOPE + A_ROPE) ** -0.5)
    return pl.pallas_call(
        kern,
        grid=(bsz, seq // TQ),
        in_specs=[
            pl.BlockSpec((None, n_ih, d_i, TQ), lambda b, q: (b, 0, 0, q)),
            pl.BlockSpec((None, seq, d_i), lambda b, q: (b, 0, 0)),
            pl.BlockSpec((None, n_ih, TQ), lambda b, q: (b, 0, q)),
            pl.BlockSpec((None, TQ, w_nope), lambda b, q: (b, q, 0)),
            pl.BlockSpec((None, TQ, w_rope), lambda b, q: (b, q, w_nope // w_rope)),
            pl.BlockSpec((TQ, cos_t.shape[1]), lambda b, q: (q, 0)),
            pl.BlockSpec((TQ, sin_s.shape[1]), lambda b, q: (q, 0)),
            pl.BlockSpec((n_h, rank, d_nope), lambda b, q: (0, 0, 0)),
            pl.BlockSpec((None, seq, d_qk), lambda b, q: (b, 0, 0)),
            pl.BlockSpec((None, seq // TK, rank, TK), lambda b, q: (b, 0, 0, 0)),
            pl.BlockSpec((n_h, d_v, rank), lambda b, q: (0, 0, 0)),
        ],
        out_specs=pl.BlockSpec((None, n_h, TQ, d_v), lambda b, q: (b, 0, q, 0)),
        out_shape=jax.ShapeDtypeStruct((bsz, n_h, seq, d_v), jnp.bfloat16),
        scratch_shapes=[pltpu.VMEM((seq, TQ), jnp.int32), pltpu.VMEM((seq, TQ), jnp.float32),
                        pltpu.VMEM((2, seq, TQ), jnp.float32), pltpu.VMEM((rank, TQ), jnp.float32),
                        pltpu.VMEM((n_h, d_qk, TQ), jnp.bfloat16)],
        compiler_params=pltpu.CompilerParams(dimension_semantics=("parallel", "arbitrary"),
                                             vmem_limit_bytes=VMEM_LIMIT_BYTES),
        name="dsa_attention",
    )(idx_qt, idx_k, idx_wt, proj, proj, cos_t, sin_s, w_ukt, kv_lat, ckvt, w_uvt)


def _nsa_kernel(q_ref, kc_ref, vct_ref, ovt_ref, ks_ref, vst_ref, kw_ref, vwt_ref, g_ref, o_ref,
                blk_scr, bias_scr, s_scr, acc_scr, mix_scr, *, n_sel, scale):
    qb = pl.program_id(2)
    nk = ((qb + 1) * TQ + TK - 1) // TK
    tpos = qb * TQ + lax.broadcasted_iota(jnp.int32, (1, TQ), 1)
    row = lax.broadcasted_iota(jnp.int32, (TK, 1), 0)
    n_grp, hd = q_ref.shape[0], q_ref.shape[1]
    n_cmp_pad = kc_ref.shape[0]
    n_slc = ovt_ref.shape[0]
    gate = lambda g, j: jax.nn.sigmoid(g_ref[pl.ds(3 * g + j, 1), :])
    head_rows = lambda g: pl.ds(pl.multiple_of(g * hd, hd), hd)

    c_last = lax.broadcasted_iota(jnp.int32, (n_cmp_pad, 1), 0) * CMP_STRIDE + (CMP_LEN - 1)
    cbias = jnp.where(c_last <= tpos, 0.0, NEG_INF)
    psum = jnp.zeros((n_cmp_pad, TQ), jnp.float32)
    for g in range(n_grp):
        s = jnp.dot(kc_ref[...], q_ref[g], preferred_element_type=jnp.float32) * scale + cbias
        m = jnp.max(s, axis=0, keepdims=True)
        m = jnp.where(m == NEG_INF, 0.0, m)
        e = jnp.exp(s - m)
        den = jnp.sum(e, axis=0, keepdims=True)
        p = e * (1.0 / jnp.where(den > 0, den, 1.0))
        psum = psum + p
        o_cmp = jnp.dot(vct_ref[...], p.astype(jnp.bfloat16), preferred_element_type=jnp.float32)
        mix_scr[pl.ds(g * hd, hd), :] = gate(g, 0) * o_cmp

    hi = psum.astype(jnp.bfloat16)
    r1 = psum - hi.astype(jnp.float32)
    mid = r1.astype(jnp.bfloat16)
    lo = (r1 - mid.astype(jnp.float32)).astype(jnp.bfloat16)
    ovt = ovt_ref[...]
    imp = (jnp.dot(ovt, hi, preferred_element_type=jnp.float32)
           + jnp.dot(ovt, mid, preferred_element_type=jnp.float32)
           + jnp.dot(ovt, lo, preferred_element_type=jnp.float32))
    jidx = lax.broadcasted_iota(jnp.int32, (n_slc, 1), 0)
    t_blk = tpos // SLC_LEN
    forced = (jidx == 0) | (jidx == t_blk) | (jidx == t_blk - 1)
    imp = jnp.where(forced, jnp.inf, imp)
    imp = jnp.where(jidx * SLC_LEN <= tpos, imp, NEG_INF)
    rank = jnp.zeros((n_slc, TQ), jnp.int32)
    for jp in range(n_slc):
        r = imp[jp:jp + 1, :]
        rank = rank + jnp.where(r > imp, 1, jnp.where(r == imp, jnp.where(jp < jidx, 1, 0), 0))
    blk_scr[...] = jnp.where(rank < n_sel, 0.0, NEG_INF)

    blocks_per_chunk = TK // SLC_LEN

    def slc_bias_chunk(c, carry):
        k0 = pl.multiple_of(c * TK, TK)
        rows = [jnp.broadcast_to(blk_scr[pl.ds(c * blocks_per_chunk + i, 1), :], (SLC_LEN, TQ))
                for i in range(blocks_per_chunk)]
        bias_scr[pl.ds(k0, TK), :] = jnp.where(k0 + row <= tpos, jnp.concatenate(rows, axis=0), NEG_INF)
        return carry

    lax.fori_loop(0, nk, slc_bias_chunk, 0)

    def slc_bias(c):
        return bias_scr[pl.ds(pl.multiple_of(c * TK, TK), TK), :]

    def win_bias(c):
        spos = c * TK + row
        return jnp.where(spos <= tpos, jnp.where(spos > tpos - WIN_LEN, 0.0, NEG_INF), NEG_INF)

    w_lo = jnp.maximum(qb * TQ - WIN_LEN + 1, 0) // TK

    def add_branch(j):
        def emit(g, o):
            mix_scr[head_rows(g), :] += gate(g, j) * o
        return emit

    _attend_heads(ks_ref, vst_ref, q_ref, n_grp, slc_bias, 0, nk, s_scr, acc_scr, scale, add_branch(1))
    _attend_heads(kw_ref, vwt_ref, q_ref, n_grp, win_bias, w_lo, nk, s_scr, acc_scr, scale, add_branch(2))
    for g in range(n_grp):
        o_ref[g] = mix_scr[pl.ds(g * hd, hd), :].T.astype(o_ref.dtype)


def _nsa_attention_t(qt, kc, vct, ovt, ks, vst, kw, vwt, gates_t):
    bsz, kvh, grp, hd, seq = qt.shape
    ncp = kc.shape[2]
    n_slc = ovt.shape[0]
    kern = functools.partial(_nsa_kernel, n_sel=min(SLC_COUNT, n_slc), scale=hd ** -0.5)
    kv_spec = pl.BlockSpec((None, None, seq, hd), lambda b, h, q: (b, h, 0, 0))
    vt_spec = pl.BlockSpec((None, None, seq // TK, hd, TK), lambda b, h, q: (b, h, 0, 0, 0))
    return pl.pallas_call(
        kern,
        grid=(bsz, kvh, seq // TQ),
        in_specs=[
            pl.BlockSpec((None, None, grp, hd, TQ), lambda b, h, q: (b, h, 0, 0, q)),
            pl.BlockSpec((None, None, ncp, hd), lambda b, h, q: (b, h, 0, 0)),
            pl.BlockSpec((None, None, hd, ncp), lambda b, h, q: (b, h, 0, 0)),
            pl.BlockSpec((n_slc, ncp), lambda b, h, q: (0, 0)),
            kv_spec, vt_spec, kv_spec, vt_spec,
            pl.BlockSpec((None, None, 3 * grp, TQ), lambda b, h, q: (b, h, 0, q)),
        ],
        out_specs=pl.BlockSpec((None, grp, TQ, hd), lambda b, h, q: (b, h, q, 0)),
        out_shape=jax.ShapeDtypeStruct((bsz, kvh * grp, seq, hd), jnp.bfloat16),
        scratch_shapes=[pltpu.VMEM((n_slc, TQ), jnp.float32), pltpu.VMEM((seq, TQ), jnp.float32),
                        pltpu.VMEM((2, seq, TQ), jnp.float32), pltpu.VMEM((hd, TQ), jnp.float32),
                        pltpu.VMEM((grp * hd, TQ), jnp.float32)],
        compiler_params=pltpu.CompilerParams(dimension_semantics=("parallel", "parallel", "arbitrary"),
                                             vmem_limit_bytes=VMEM_LIMIT_BYTES),
        name="nsa_attention",
    )(qt, kc, vct, ovt, ks, vst, kw, vwt, gates_t)


HG_C = 64
HG_SB = 8
HG_CT = 1024


def _cumsum_rows(x):
    n = x.shape[0]
    row = lax.broadcasted_iota(jnp.int32, (n, 1), 0)
    s = 1
    while s < n:
        x = x + jnp.where(row >= s, pltpu.roll(x, s, axis=0), 0.0)
        s *= 2
    return x


def _bcast_rows(ref, first, period, n):
    return jnp.concatenate(
        [jnp.broadcast_to(ref[pl.ds(first + p * period, 1), :], (period, ref.shape[1])) for p in range(n // period)],
        axis=0)


def _hgrn_kernel(q_ref, f_ref, i_ref, g_ref, lb_ref, gn_ref, o_ref, st_scr, k_scr, b_scr, v_scr):
    @pl.when(pl.program_id(2) == 0)
    def _():
        st_scr[...] = jnp.zeros_like(st_scr)

    c_len, d = HG_C, q_ref.shape[1]
    bf = jnp.bfloat16
    lb = lb_ref[...]
    gain = gn_ref[...]
    row = lax.broadcasted_iota(jnp.int32, (c_len, 1), 0)
    col = lax.broadcasted_iota(jnp.int32, (1, c_len), 1)
    ones_b = jnp.ones((d, d), bf)
    nt = (((1,), (1,)), ((), ()))
    tn = (((0,), (0,)), ((), ()))

    def chunk(c, carry):
        sl = pl.ds(pl.multiple_of(c * c_len, c_len), c_len)
        qr, fr, v, gr = q_ref[sl, :], f_ref[sl, :], i_ref[sl, :], g_ref[sl, :]
        qv = qr * jax.nn.sigmoid(qr)
        fg = lb + (1.0 - lb) * jax.nn.sigmoid(fr)
        kk = 1.0 - fg
        b = _cumsum_rows(jnp.log(fg))
        k_scr[...] = kk
        b_scr[...] = b
        v_scr[...] = v
        v_b = v.astype(bf)

        att = jnp.zeros((c_len, c_len), jnp.float32)
        size = c_len // 2
        while size >= HG_SB:
            ref = _bcast_rows(b_scr, size - 1, 2 * size, c_len)
            is_q = ((row // size) % 2) == 1
            a_q = qv * jnp.exp(jnp.where(is_q, b - ref, NEG_INF))
            k_f = kk * jnp.exp(jnp.where(is_q, NEG_INF, ref - b))
            att_l = lax.dot_general(a_q.astype(bf), k_f.astype(bf), nt, preferred_element_type=jnp.float32)
            if 2 * size < c_len:
                att_l = jnp.where((row // (2 * size)) == (col // (2 * size)), att_l, 0.0)
            att = att + att_l
            size //= 2
        o = jnp.dot(att.astype(bf), v_b, preferred_element_type=jnp.float32)

        rmod = row % HG_SB
        for j in range(HG_SB):
            kb = _bcast_rows(k_scr, j, HG_SB, c_len)
            bb = _bcast_rows(b_scr, j, HG_SB, c_len)
            vb = _bcast_rows(v_scr, j, HG_SB, c_len)
            x = qv * kb * jnp.exp(jnp.where(rmod >= j, b - bb, NEG_INF))
            o = o + jnp.dot(x.astype(bf), ones_b, preferred_element_type=jnp.float32) * vb

        st = st_scr[...]
        o = o + lax.dot_general((qv * jnp.exp(b)).astype(bf), st.astype(bf), nt, preferred_element_type=jnp.float32)
        b_last = b[c_len - 1:c_len, :]
        k_l = kk * jnp.exp(b_last - b)
        st_scr[...] = st * jnp.exp(b_last) + lax.dot_general(v_b, k_l.astype(bf), tn,
                                                            preferred_element_type=jnp.float32)

        ms = jnp.mean(o * o, axis=-1, keepdims=True)
        o_ref[sl, :] = o * lax.rsqrt(ms + NORM_EPS) * gain * (gr * jax.nn.sigmoid(gr))
        return carry

    lax.fori_loop(0, q_ref.shape[0] // c_len, chunk, 0, unroll=4)


def _hgrn_recurrence(proj, lower_bound, out_norm):
    bsz, seq, w4 = proj.shape
    width = w4 // 4
    n_heads = width // C_HEAD_DIM
    ct = min(HG_CT, seq)
    assert seq % ct == 0 and ct % HG_C == 0
    part = lambda j: pl.BlockSpec((None, ct, C_HEAD_DIM), lambda b, h, t: (b, t, j * n_heads + h))
    vec = pl.BlockSpec((None, 1, C_HEAD_DIM), lambda b, h, t: (h, 0, 0))
    return pl.pallas_call(
        _hgrn_kernel,
        grid=(bsz, n_heads, seq // ct),
        in_specs=[part(0), part(1), part(2), part(3), vec, vec],
        out_specs=pl.BlockSpec((None, ct, C_HEAD_DIM), lambda b, h, t: (b, t, h)),
        out_shape=jax.ShapeDtypeStruct((bsz, seq, width), jnp.float32),
        scratch_shapes=[pltpu.VMEM((C_HEAD_DIM, C_HEAD_DIM), jnp.float32)]
        + [pltpu.VMEM((HG_C, C_HEAD_DIM), jnp.float32)] * 3,
        compiler_params=pltpu.CompilerParams(dimension_semantics=("parallel", "parallel", "arbitrary"),
                                             vmem_limit_bytes=VMEM_LIMIT_BYTES),
        name="hgrn2_recurrence",
    )(proj, proj, proj, proj, lower_bound.reshape(n_heads, 1, C_HEAD_DIM).astype(jnp.float32),
      out_norm.reshape(n_heads, 1, C_HEAD_DIM).astype(jnp.float32))


def rms_norm(x, g):
    xf = x.astype(jnp.float32)
    y = xf * lax.rsqrt(jnp.mean(xf * xf, axis=-1, keepdims=True) + NORM_EPS)
    return (y * g.astype(jnp.float32)).astype(x.dtype)


def rope(x, pos):
    d = x.shape[-1]
    inv = ROPE_THETA ** (-jnp.arange(0, d, 2, dtype=jnp.float32) / d)
    ang = pos.astype(jnp.float32)[:, None] * inv[None, :]
    cos = jnp.cos(ang)[:, None, :]
    sin = jnp.sin(ang)[:, None, :]
    xf = x.astype(jnp.float32)
    x1, x2 = xf[..., : d // 2], xf[..., d // 2:]
    return jnp.concatenate([x1 * cos - x2 * sin, x2 * cos + x1 * sin], axis=-1).astype(x.dtype)


def split_cols(a, widths):
    return jnp.split(a, np.cumsum(widths)[:-1].tolist(), axis=-1)


def dsa_nsa_mixer(u, w_in, kv_norm, w_uk, w_uv, cmp_pos, cmp_w1, cmp_w2, w_out):
    bsz, seq, _ = u.shape
    pos = jnp.arange(seq)
    proj = _proj(u, w_in)
    (_, _, a_ckv, a_k_rope, i_q, i_k, i_w, b_q, b_kv, b_gate) = split_cols(proj, AB_WIDTHS)
    c_kv = rms_norm(a_ckv, kv_norm)
    k_rope = rope(a_k_rope[:, :, None, :], pos)[:, :, 0]
    kv_lat = jnp.concatenate([c_kv, k_rope], axis=-1)
    idx_q = rope(i_q.reshape(bsz, seq, IDX_HEADS, IDX_DIM), pos)
    idx_k = rope(i_k[:, :, None, :], pos)[:, :, 0]
    bf = jnp.bfloat16
    inv = ROPE_THETA ** (-jnp.arange(0, A_ROPE, 2, dtype=jnp.float32) / A_ROPE)
    ang = pos.astype(jnp.float32)[:, None] * inv[None, :]
    lane_tile = 128
    cos_t = jnp.tile(jnp.cos(ang), (1, lane_tile // (A_ROPE // 2)))
    sin_s = jnp.tile(jnp.concatenate([-jnp.sin(ang), jnp.sin(ang)], axis=1), (1, lane_tile // A_ROPE))
    o_a_t = _dsa_attention_t(
        idx_q.transpose(0, 2, 3, 1).astype(bf), idx_k.astype(bf), i_w.transpose(0, 2, 1),
        proj, cos_t, sin_s, w_uk.transpose(0, 2, 1).astype(bf), kv_lat.astype(bf),
        c_kv.reshape(bsz, seq // TK, TK, A_KV_RANK).transpose(0, 1, 3, 2).astype(bf),
        w_uv.transpose(0, 2, 1).astype(bf))
    bq = rope(b_q.reshape(bsz, seq, B_HEADS, B_HEAD_DIM), pos)
    kv = b_kv.reshape(bsz, seq, 6, B_KV_HEADS, B_HEAD_DIM)
    o_b_t = nsa_attention_t(bq, rope(kv[:, :, 0], pos), kv[:, :, 1],
                            rope(kv[:, :, 2], pos), kv[:, :, 3],
                            rope(kv[:, :, 4], pos), kv[:, :, 5],
                            b_gate, cmp_pos, cmp_w1, cmp_w2)
    return _matmul_heads([o_a_t, o_b_t], w_out).reshape(bsz, seq, w_out.shape[1])


def nsa_attention_t(q, k_cmp, v_cmp, k_slc, v_slc, k_win, v_win, gates, cmp_pos, cmp_w1, cmp_w2):
    bsz, seq = q.shape[:2]
    kvh, grp, hd = B_KV_HEADS, B_GROUP, B_HEAD_DIM
    bf = jnp.bfloat16
    n_cmp = (seq - CMP_LEN) // CMP_STRIDE + 1
    n_cmp_pad = -(-n_cmp // 128) * 128
    tok = np.arange(n_cmp)[:, None] * CMP_STRIDE + np.arange(CMP_LEN)[None, :]

    def compress(a, j):
        blk = a[:, tok] + cmp_pos[j][:, None, :]
        blk = blk.transpose(0, 1, 3, 2, 4).reshape(bsz, n_cmp, kvh, CMP_LEN * hd)
        out = jax.nn.gelu(blk @ cmp_w1[j]) @ cmp_w2[j]
        return jnp.pad(out, ((0, 0), (0, n_cmp_pad - n_cmp), (0, 0), (0, 0)))

    kc = compress(k_cmp, 0).transpose(0, 2, 1, 3).astype(bf)
    vct = compress(v_cmp, 1).transpose(0, 2, 3, 1).astype(bf)
    n_slc = seq // SLC_LEN
    cmp_start = np.arange(n_cmp_pad) * CMP_STRIDE
    slc_start = np.arange(n_slc) * SLC_LEN
    ovt = ((cmp_start[None, :] < slc_start[:, None] + SLC_LEN)
           & (cmp_start[None, :] + CMP_LEN > slc_start[:, None]) & (np.arange(n_cmp_pad)[None, :] < n_cmp))
    ovt = jnp.asarray(ovt, bf)
    qt = q.reshape(bsz, seq, kvh, grp, hd).transpose(0, 2, 3, 4, 1).astype(bf)
    k_t = lambda a: a.transpose(0, 2, 1, 3).astype(bf)
    v_t = lambda a: a.reshape(bsz, seq // TK, TK, kvh, hd).transpose(0, 3, 1, 4, 2).astype(bf)
    gates_t = gates.reshape(bsz, seq, kvh, 3 * grp).transpose(0, 2, 3, 1)
    return _nsa_attention_t(qt, kc, vct, ovt, k_t(k_slc), v_t(v_slc), k_t(k_win), v_t(v_win), gates_t)


def hgrn2_mixer(u, w_in, lower_bound, out_norm, w_out):
    return _proj(_hgrn_recurrence(_proj(u, w_in), lower_bound, out_norm), w_out)


def routed_experts(xb_tok, e_id, e_w, w_gate_up, w_down, layer):
    n_as = e_id.shape[0]
    tok = jnp.arange(n_as, dtype=jnp.int32) // GROUP_TOPK
    onehot = (e_id[:, None] == jnp.arange(N_EXPERTS, dtype=e_id.dtype)[None, :]).astype(jnp.int32)
    csum = jnp.cumsum(onehot, axis=0)
    counts = csum[-1]
    rank = jnp.take_along_axis(csum, e_id[:, None], axis=1)[:, 0] - 1
    padded = (counts + MOE_BLOCK - 1) // MOE_BLOCK * MOE_BLOCK
    pad_end = jnp.cumsum(padded)
    pad_start = pad_end - padded
    dest = pad_start[e_id] + rank
    n_blk = -(-(n_as + N_EXPERTS * (MOE_BLOCK - 1)) // MOE_BLOCK)
    rows = n_blk * MOE_BLOCK
    fields = jnp.stack([tok, lax.bitcast_convert_type(e_w, jnp.int32)], axis=1)
    filler = jnp.stack([jnp.arange(rows, dtype=jnp.int32) % xb_tok.shape[0], jnp.zeros((rows,), jnp.int32)], axis=1)
    placed = filler.at[dest].set(fields)
    row_tok = placed[:, 0]
    row_w = lax.bitcast_convert_type(placed[:, 1], jnp.float32)
    blk_e = jnp.minimum(jnp.searchsorted(pad_end, jnp.arange(n_blk) * MOE_BLOCK, side='right'),
                        N_EXPERTS - 1).astype(jnp.int32)
    n_used = (pad_end[-1:] // MOE_BLOCK).astype(jnp.int32)
    assert n_blk % MOE_SPLITS == 0
    per = n_blk // MOE_SPLITS
    runs = _moe_runs(blk_e, n_used, per)
    yb = None
    for s in range(MOE_SPLITS):
        r = slice(s * per * MOE_BLOCK, (s + 1) * per * MOE_BLOCK)
        yb = _moe_blocks(blk_e, n_used, runs, xb_tok[row_tok[r]], w_gate_up, w_down, row_w[r, None], layer,
                         s * per, yb)
    return [yb[dest[s::GROUP_TOPK]] for s in range(GROUP_TOPK)]


def hier_moe(xb_tok, logits, b_group, b_expert, w_gate_up, w_down, layer):
    n_tok = xb_tok.shape[0]
    g_logits = logits[:, :N_GROUPS] + b_group.astype(jnp.float32)
    g_prob = jax.nn.softmax(g_logits, axis=-1)
    grp = jnp.argmax(g_logits, axis=-1)
    g_w = jnp.take_along_axis(g_prob, grp[:, None], axis=-1)
    e_logits = logits[:, N_GROUPS:N_GROUPS + N_EXPERTS] + b_expert.astype(jnp.float32)
    e_logits = e_logits.reshape(n_tok, N_GROUPS, EXPERTS_PER_GROUP)
    e_logits = jnp.take_along_axis(e_logits, grp[:, None, None], axis=1)[:, 0]
    top_v, top_i = lax.top_k(e_logits, GROUP_TOPK)
    e_w = jax.nn.softmax(top_v, axis=-1) * g_w
    e_id = (grp[:, None] * EXPERTS_PER_GROUP + top_i).astype(jnp.int32)
    return routed_experts(xb_tok, e_id.reshape(-1), e_w.reshape(-1).astype(jnp.float32), w_gate_up, w_down, layer)


def kernel(x, mix_norm, ffn_norm, final_norm, ab_w_in, dsa_kv_norm, dsa_w_uk, dsa_w_uv, nsa_cmp_pos, nsa_cmp_w1, nsa_cmp_w2, ab_w_out, hgrn_w_in, hgrn_lb_logits, hgrn_out_norm, hgrn_w_out, moe_w_group, moe_b_group, moe_w_expert, moe_b_expert, moe_w_gate_up, moe_w_down):
    lb_p = jax.nn.softmax(hgrn_lb_logits.astype(jnp.float32), axis=0)
    lower_bounds = jnp.cumsum(lb_p, axis=0) - lb_p[0]
    bsz, seq, d = x.shape
    bf = jnp.bfloat16
    h = x.reshape(bsz * seq, d)
    pending = []
    for layer in range(DEPTH):
        j = layer // 2
        if pending:
            h, u = _add_norm(h, pending, mix_norm[layer], bf)
        else:
            (u,) = _add_norm(h, [], mix_norm[layer], bf, write_h=False)
        u = u.reshape(bsz, seq, d)
        if layer % 2 == 0:
            m = dsa_nsa_mixer(u, ab_w_in[j], dsa_kv_norm[j], dsa_w_uk[j], dsa_w_uv[j],
                              nsa_cmp_pos[j], nsa_cmp_w1[j], nsa_cmp_w2[j], ab_w_out[j])
        else:
            m = hgrn2_mixer(u, hgrn_w_in[j], lower_bounds[layer], hgrn_out_norm[j], hgrn_w_out[j])
        w_router = jnp.concatenate([moe_w_group[layer], moe_w_expert[layer]], axis=1)
        w_router = jnp.pad(w_router, ((0, 0), (0, ROUTER_PAD - w_router.shape[1])))
        h, u, logits = _add_norm(h, [m.reshape(bsz * seq, d)], ffn_norm[layer], bf, w_router=w_router)
        pending = hier_moe(u, logits, moe_b_group[layer], moe_b_expert[layer], moe_w_gate_up, moe_w_down, layer)
    (out,) = _add_norm(h, pending, final_norm, jnp.float32, write_h=False)
    return out.reshape(bsz, seq, d)
```
